```python
import jax, jax.numpy as jnp
from jax import lax
import numpy as np

D_MODEL = 1024
BATCH = 16
SEQ = 2048
DEPTH = 2

ATT_GROUPS = ((128, 1), (512, 4), (2048, 16))
HEADS_PER_GROUP = 4
N_ATT_HEADS = HEADS_PER_GROUP * len(ATT_GROUPS)
ATT_HEAD_DIM = 64
ATT_WIDTH = N_ATT_HEADS * ATT_HEAD_DIM
ATT_OUT_WIDTH = HEADS_PER_GROUP * ATT_HEAD_DIM
BLOCK = 128
ALIBI_MAX = 8.0
NEG_INF = -1e30
RET_HEADS = 8
RET_KEY_DIM = 64
RET_VAL_DIM = 128
RET_QK_WIDTH = RET_HEADS * RET_KEY_DIM
RET_V_WIDTH = RET_HEADS * RET_VAL_DIM
RET_CHUNK = 128
ROPE_BASE = 10000.0
D_FF = 2816
N_EXPERTS = 8
TOP_K = 2
EPS = 1e-6
IN_SIZES = (ATT_WIDTH, ATT_WIDTH, ATT_WIDTH, RET_QK_WIDTH, RET_QK_WIDTH, RET_V_WIDTH, RET_V_WIDTH, D_MODEL, D_MODEL)
IN_COLS = sum(IN_SIZES)
IN_SPLIT_POINTS = tuple(int(v) for v in np.cumsum(IN_SIZES)[:-1])

kernel_name = 'hybrid_dilated_retention_moe_block'


def rms_norm(x, gain):
    xf = x.astype(jnp.float32)
    y = xf * lax.rsqrt(jnp.mean(xf * xf, axis=-1, keepdims=True) + EPS)
    return (y * gain.astype(jnp.float32)).astype(x.dtype)


def alibi_slopes():
    h = jnp.arange(1, N_ATT_HEADS + 1, dtype=jnp.float32)
    return jnp.exp2(-ALIBI_MAX * h / N_ATT_HEADS)


def retention_gammas():
    return jnp.log1p(-jnp.exp2(-5.0 - jnp.arange(RET_HEADS, dtype=jnp.float32)))


def rotary(t, positions):
    half = t.shape[-1] // 2
    inv_freq = 1.0 / (ROPE_BASE ** (jnp.arange(half, dtype=jnp.float32) / half))
    ang = positions.astype(jnp.float32)[:, None] * inv_freq[None]
    cos = jnp.cos(ang)[None, :, None, :]
    sin = jnp.sin(ang)[None, :, None, :]
    t1, t2 = t[..., :half], t[..., half:]
    return jnp.concatenate([t1 * cos - t2 * sin, t1 * sin + t2 * cos], axis=-1)


def dilated_attention(q, k, v, window, dilation, slopes):
    B, S, H, Dh = q.shape
    n_w = window // dilation
    n = S // dilation
    nb = -(-n // BLOCK)
    pad = nb * BLOCK - n

    def to_strided(t):
        t = t.reshape(B, n, dilation, H, Dh)
        t = jnp.pad(t, ((0, 0), (0, pad), (0, 0), (0, 0), (0, 0)))
        return t.reshape(B, nb, BLOCK, dilation, H, Dh)

    def with_prev(t):
        prev = jnp.pad(t, ((0, 0), (1, 0), (0, 0), (0, 0), (0, 0), (0, 0)))[:, :-1]
        return jnp.concatenate([prev, t], axis=2)

    qs = to_strided(q)
    kb = with_prev(to_strided(k))
    vb = with_prev(to_strided(v))
    scores = jnp.einsum('bnqrhd,bnkrhd->bnrhqk', qs, kb) * (Dh ** -0.5)
    qi = jnp.arange(BLOCK)[:, None] + BLOCK
    kj = jnp.arange(2 * BLOCK)[None, :]
    dist = qi - kj
    blk = jnp.arange(nb)
    valid = ((dist >= 0) & (dist <= n_w))[None] & ((blk[:, None, None] > 0) | (kj[None] >= BLOCK))
    alibi = -slopes[:, None, None] * (dilation * dist).astype(jnp.float32)[None]
    scores = jnp.where(valid[None, :, None, None], scores + alibi, NEG_INF)
    m = jnp.max(scores, axis=-1, keepdims=True)
    p = jnp.exp(scores - m)
    den = jnp.sum(p, axis=-1, keepdims=True)
    out = jnp.einsum('bnrhqk,bnkrhd->bnqrhd', p / den, vb)
    lse = (m + jnp.log(den))[..., 0]
    out = out.reshape(B, nb * BLOCK, dilation, H, Dh)[:, :n].reshape(B, S, H, Dh)
    lse = jnp.transpose(lse, (0, 1, 4, 2, 3)).reshape(B, nb * BLOCK, dilation, H)[:, :n].reshape(B, S, H)
    return out, lse


def retention(q, k, v, log_gamma):
    B, S, H, Dk = q.shape
    Dv = v.shape[-1]
    C = RET_CHUNK
    nc = S // C
    k = k * (Dk ** -0.5)
    qc = q.reshape(B, nc, C, H, Dk)
    kc = k.reshape(B, nc, C, H, Dk)
    vc = v.reshape(B, nc, C, H, Dv)
    idx = jnp.arange(C, dtype=jnp.float32)
    rel = idx[:, None] - idx[None, :]
    decay = jnp.where(rel[None] >= 0, jnp.exp(jnp.maximum(rel, 0.0)[None] * log_gamma[:, None, None]), 0.0)
    intra = jnp.einsum('bcihd,bcjhd->bchij', qc, kc) * decay[None, None]
    intra_out = jnp.einsum('bchij,bcjhv->bcihv', intra, vc)
    zeta = jnp.exp((C - 1 - idx)[None] * log_gamma[:, None])
    xi = jnp.exp((idx + 1)[None] * log_gamma[:, None])
    chunk_kv = jnp.einsum('bcjhd,hj,bcjhv->bchdv', kc, zeta, vc)
    g_chunk = jnp.exp(C * log_gamma)[:, None, None]

    def step(state, kv):
        return state * g_chunk + kv, state

    _, r_prev = lax.scan(step, jnp.zeros((B, H, Dk, Dv), jnp.float32), jnp.moveaxis(chunk_kv, 1, 0))
    r_prev = jnp.moveaxis(r_prev, 0, 1)
    inter = jnp.einsum('bcihd,hi,bchdv->bcihv', qc, xi, r_prev)
    return (intra_out + inter).reshape(B, S, H, Dv)


def token_mixer(h, w_in, w_att_proj, w_ret_proj, w_mix_out):
    B, S, _ = h.shape
    proj = h @ w_in
    q_a, k_a, v_a, q_r, k_r, v_r, g_r, gate_a, gate_r = jnp.split(proj, IN_SPLIT_POINTS, axis=-1)
    heads = lambda t: t.astype(jnp.float32).reshape(B, S, N_ATT_HEADS, ATT_HEAD_DIM)
    q_a, k_a, v_a = heads(q_a), heads(k_a), heads(v_a)
    slopes = alibi_slopes()
    outs, lses = [], []
    for g, (window, dilation) in enumerate(ATT_GROUPS):
        sl = slice(g * HEADS_PER_GROUP, (g + 1) * HEADS_PER_GROUP)
        o, l = dilated_attention(q_a[:, :, sl], k_a[:, :, sl], v_a[:, :, sl], window, dilation, slopes[sl])
        outs.append(o)
        lses.append(l)
    weights = jax.nn.softmax(jnp.stack(lses), axis=0)
    att = jnp.sum(weights[..., None] * jnp.stack(outs), axis=0).reshape(B, S, ATT_OUT_WIDTH).astype(h.dtype)
    pos = jnp.arange(S)
    q_r = rotary(q_r.astype(jnp.float32).reshape(B, S, RET_HEADS, RET_KEY_DIM), pos)
    k_r = rotary(k_r.astype(jnp.float32).reshape(B, S, RET_HEADS, RET_KEY_DIM), pos)
    v_r = v_r.astype(jnp.float32).reshape(B, S, RET_HEADS, RET_VAL_DIM)
    ret = retention(q_r, k_r, v_r, retention_gammas())
    ret = ret * lax.rsqrt(jnp.mean(ret * ret, axis=-1, keepdims=True) + EPS)
    ret = (jax.nn.silu(g_r.astype(jnp.float32)) * ret.reshape(B, S, RET_V_WIDTH)).astype(h.dtype)
    merged = jax.nn.sigmoid(gate_a) * (att @ w_att_proj) + jax.nn.sigmoid(gate_r) * (ret @ w_ret_proj)
    return merged @ w_mix_out


def swiglu(h, w_gate, w_up, w_down):
    return (jax.nn.silu(h @ w_gate) * (h @ w_up)) @ w_down


def moe_swiglu(h, w_router, b_router, w_exp_gate, w_exp_up, w_exp_down):
    logits = (h @ w_router).astype(jnp.float32) + b_router.astype(jnp.float32)
    top_vals, top_idx = lax.top_k(logits, TOP_K)
    top_w = jax.nn.softmax(top_vals, axis=-1)
    combine = jnp.sum(jax.nn.one_hot(top_idx, N_EXPERTS, dtype=jnp.float32) * top_w[..., None], axis=-2)
    out = jnp.zeros_like(h)
    for e in range(N_EXPERTS):
        y = swiglu(h, w_exp_gate[e], w_exp_up[e], w_exp_down[e])
        out = out + combine[..., e:e + 1].astype(h.dtype) * y
    return out


def setup_inputs(seed: int = 0) -> dict:
    key = jax.random.key(seed)
    ks = jax.random.split(key, 24)
    n_dense = (DEPTH + 1) // 2
    n_moe = DEPTH // 2
    f32 = jnp.float32

    def w(k, shape, fan_in):
        return jax.random.normal(k, shape, f32) * (fan_in ** -0.5)

    def gain(k, shape):
        return 1.0 + 0.05 * jax.random.normal(k, shape, f32)

    def bias(k, shape, scale):
        return scale * jax.random.normal(k, shape, f32)

    D = D_MODEL
    return {
        'x': jax.random.normal(ks[0], (BATCH, SEQ, D), f32),
        'c': jax.random.normal(ks[1], (BATCH, D), f32),
        'w_ada': w(ks[2], (DEPTH, D, 6 * D), D),
        'b_ada': bias(ks[3], (DEPTH, 6 * D), 0.02),
        'norm_pre_mix': gain(ks[4], (DEPTH, D)),
        'w_in': w(ks[5], (DEPTH, D, IN_COLS), D),
        'w_att_proj': w(ks[6], (DEPTH, ATT_OUT_WIDTH, D), ATT_OUT_WIDTH),
        'w_ret_proj': w(ks[7], (DEPTH, RET_V_WIDTH, D), RET_V_WIDTH),
        'w_mix_out': w(ks[8], (DEPTH, D, D), D),
        'norm_post_mix': gain(ks[9], (DEPTH, D)),
        'norm_pre_ffn': gain(ks[10], (DEPTH, D)),
        'w_ffn_gate': w(ks[11], (n_dense, D, D_FF), D),
        'w_ffn_up': w(ks[12], (n_dense, D, D_FF), D),
        'w_ffn_down': w(ks[13], (n_dense, D_FF, D), D_FF),
        'w_router': w(ks[14], (n_moe, D, N_EXPERTS), D),
        'b_router': bias(ks[15], (n_moe, N_EXPERTS), 0.01),
        'w_exp_gate': w(ks[16], (n_moe, N_EXPERTS, D, D_FF), D),
        'w_exp_up': w(ks[17], (n_moe, N_EXPERTS, D, D_FF), D),
        'w_exp_down': w(ks[18], (n_moe, N_EXPERTS, D_FF, D), D_FF),
        'norm_post_ffn': gain(ks[19], (DEPTH, D)),
    }


def reference(x, c, w_ada, b_ada, norm_pre_mix, w_in, w_att_proj, w_ret_proj, w_mix_out, norm_post_mix,
              norm_pre_ffn, w_ffn_gate, w_ffn_up, w_ffn_down, w_router, b_router, w_exp_gate, w_exp_up,
              w_exp_down, norm_post_ffn):
    c_act = jax.nn.silu(c)
    for layer in range(DEPTH):
        mod = c_act @ w_ada[layer] + b_ada[layer]
        sh1, sc1, g1, sh2, sc2, g2 = jnp.split(mod[:, None, :], 6, axis=-1)
        h = rms_norm(x, norm_pre_mix[layer]) * (1.0 + sc1) + sh1
        y = token_mixer(h, w_in[layer], w_att_proj[layer], w_ret_proj[layer], w_mix_out[layer])
        x = x + g1 * rms_norm(y, norm_post_mix[layer])
        h = rms_norm(x, norm_pre_ffn[layer]) * (1.0 + sc2) + sh2
        j = layer // 2
        if layer % 2 == 0:
            y = swiglu(h, w_ffn_gate[j], w_ffn_up[j], w_ffn_down[j])
        else:
            y = moe_swiglu(h, w_router[j], b_router[j], w_exp_gate[j], w_exp_up[j], w_exp_down[j])
        x = x + g2 * rms_norm(y, norm_post_ffn[layer])
    return x
```

```python
import functools

import numpy as np
import jax
import jax.numpy as jnp
from jax import lax
from jax.experimental import pallas as pl
from jax.experimental.pallas import tpu as pltpu

F32 = jnp.float32
BF16 = jnp.bfloat16

D_MODEL = 1024
DEPTH = 2
ATT_GROUPS = ((128, 1), (512, 4), (2048, 16))
HEADS_PER_GROUP = 4
N_ATT_HEADS = HEADS_PER_GROUP * len(ATT_GROUPS)
ATT_HEAD_DIM = 64
ATT_WIDTH = N_ATT_HEADS * ATT_HEAD_DIM
ATT_OUT_WIDTH = HEADS_PER_GROUP * ATT_HEAD_DIM
BLOCK = 128
ALIBI_MAX = 8.0
NEG_INF = -1e30
RET_HEADS = 8
RET_KEY_DIM = 64
RET_VAL_DIM = 128
RET_QK_WIDTH = RET_HEADS * RET_KEY_DIM
RET_V_WIDTH = RET_HEADS * RET_VAL_DIM
RET_CHUNK = 128
ROPE_BASE = 10000.0
D_FF = 2816
N_EXPERTS = 8
TOP_K = 2
EPS = 1e-6
IN_SIZES = (ATT_WIDTH, ATT_WIDTH, ATT_WIDTH, RET_QK_WIDTH, RET_QK_WIDTH, RET_V_WIDTH, RET_V_WIDTH, D_MODEL, D_MODEL)
IN_COLS = sum(IN_SIZES)

_SEG_ORDER = (5, 6, 7, 8, 3, 4, 0, 1, 2)
LANE = 128
ATT_COL0 = (4 * 1024 + 2 * RET_QK_WIDTH) // ATT_OUT_WIDTH
N_COLBLK = IN_COLS // ATT_OUT_WIDTH

MiB = 1024 * 1024


def _cparams(sem, vmem_mib):
    return pltpu.CompilerParams(dimension_semantics=sem, vmem_limit_bytes=vmem_mib * MiB)


def _rms(x, gain):
    return x * lax.rsqrt(jnp.mean(x * x, axis=-1, keepdims=True) + EPS) * gain


def _silu(x):
    return x * jax.nn.sigmoid(x)


def _ada_kernel(c_ref, w_ref, b_ref, o_ref):
    ca = _silu(c_ref[...])
    o_ref[0] = jnp.dot(ca, w_ref[0], preferred_element_type=F32, precision=lax.Precision.HIGHEST) + b_ref[0]


def _ada(c, w_ada, b_ada):
    B, D = c.shape
    out = pl.pallas_call(
        _ada_kernel,
        out_shape=jax.ShapeDtypeStruct((DEPTH * 6, B, D), F32),
        grid=(DEPTH, 6),
        in_specs=[
            pl.BlockSpec((B, D), lambda l, k: (0, 0)),
            pl.BlockSpec((1, D, D), lambda l, k: (l, 0, k)),
            pl.BlockSpec((1, 1, D), lambda l, k: (l, 0, k)),
        ],
        out_specs=pl.BlockSpec((1, B, D), lambda l, k: (l * 6 + k, 0, 0)),
        compiler_params=_cparams(("arbitrary", "arbitrary"), 32),
        name="ada",
    )(c, w_ada, b_ada.reshape(DEPTH, 1, 6 * D))
    return out.reshape(DEPTH, 6, B, 1, D)


def _mod_spec(layer, k, rows_per_batch_tile):
    return pl.BlockSpec((None, None, None, 1, D_MODEL),
                        lambda i, *_: (layer, k, i // rows_per_batch_tile, 0, 0))


IN_CHUNK = 256


def _inproj_kernel(x_ref, gain_ref, sc_ref, sh_ref, w_ref, o_ref):
    h = _rms(x_ref[...], gain_ref[...]) * (1.0 + sc_ref[...]) + sh_ref[...]
    hb = h.astype(BF16)
    for c in range(IN_COLS // IN_CHUNK):
        sl = slice(c * IN_CHUNK, (c + 1) * IN_CHUNK)
        o_ref[:, sl] = jnp.dot(hb, w_ref[:, sl], preferred_element_type=F32).astype(BF16)


def _inproj(x2, gain, mod, layer, w_in_b, S, tm=512):
    R, D = x2.shape
    tpb = S // tm
    return pl.pallas_call(
        _inproj_kernel,
        out_shape=jax.ShapeDtypeStruct((R, IN_COLS), BF16),
        grid=(R // tm,),
        in_specs=[
            pl.BlockSpec((tm, D), lambda i: (i, 0)),
            pl.BlockSpec((1, D), lambda i: (0, 0)),
            _mod_spec(layer, 1, tpb),
            _mod_spec(layer, 0, tpb),
            pl.BlockSpec((D, IN_COLS), lambda i: (0, 0), pipeline_mode=pl.Buffered(1)),
        ],
        out_specs=pl.BlockSpec((tm, IN_COLS), lambda i: (i, 0)),
        compiler_params=_cparams(("parallel",), 48),
        name="inproj",
    )(x2, gain, mod, mod, w_in_b)


def _alibi_slopes():
    return [2.0 ** (-ALIBI_MAX * h / N_ATT_HEADS) for h in range(1, N_ATT_HEADS + 1)]


def _attn_kernel(q_ref, k_ref, v_ref, o_ref, l_ref, *, n, n_w, dilation, slopes):
    nb = n // BLOCK
    win = min(2 * BLOCK, n)

    def block(i, carry):
        q0 = pl.multiple_of(i * BLOCK, BLOCK)
        k0 = pl.multiple_of(jnp.clip((i - 1) * BLOCK, 0, n - win), BLOCK)
        q = q_ref[pl.ds(q0, BLOCK), :]
        k = k_ref[pl.ds(k0, win), :]
        v = v_ref[pl.ds(k0, win), :]
        qi = q0 + lax.broadcasted_iota(jnp.int32, (BLOCK, win), 0)
        kj = k0 + lax.broadcasted_iota(jnp.int32, (BLOCK, win), 1)
        dist = qi - kj
        valid = (dist >= 0) & (dist <= n_w)
        dist_f = (dilation * dist).astype(F32)
        for h in range(HEADS_PER_GROUP):
            hs = slice(h * ATT_HEAD_DIM, (h + 1) * ATT_HEAD_DIM)
            s = lax.dot_general(q[:, hs], k[:, hs], (((1,), (1,)), ((), ())), preferred_element_type=F32)
            s = s * (ATT_HEAD_DIM ** -0.5) - slopes[h] * dist_f
            s = jnp.where(valid, s, NEG_INF)
            m = jnp.max(s, axis=-1, keepdims=True)
            p = jnp.exp(s - m)
            den = jnp.sum(p, axis=-1, keepdims=True)
            o = jnp.dot(p.astype(BF16), v[:, hs], preferred_element_type=F32) / den
            o_ref[pl.ds(q0, BLOCK), hs] = o.astype(o_ref.dtype)
            l_ref[pl.ds(q0, BLOCK), hs] = jnp.broadcast_to(m + jnp.log(den), (BLOCK, ATT_HEAD_DIM))
        return carry

    lax.fori_loop(0, nb, block, 0)


def _attn_group(proj3, g, B, S):
    window, d = ATT_GROUPS[g]
    n = S // d
    assert n % BLOCK == 0
    view = proj3.reshape(B, n, d * IN_COLS)
    slopes = _alibi_slopes()[g * HEADS_PER_GROUP:(g + 1) * HEADS_PER_GROUP]
    kern = functools.partial(_attn_kernel, n=n, n_w=window // d, dilation=d, slopes=slopes)

    def in_spec(seg):
        cb = ATT_COL0 + 3 * seg + g
        return pl.BlockSpec((None, n, ATT_OUT_WIDTH), lambda b, r: (b, 0, r * N_COLBLK + cb))

    out_spec = pl.BlockSpec((None, n, ATT_OUT_WIDTH), lambda b, r: (b, 0, r))
    o, l = pl.pallas_call(
        kern,
        out_shape=(jax.ShapeDtypeStruct((B, n, d * ATT_OUT_WIDTH), BF16),
                   jax.ShapeDtypeStruct((B, n, d * ATT_OUT_WIDTH), F32)),
        grid=(B, d),
        in_specs=[in_spec(0), in_spec(1), in_spec(2)],
        out_specs=(out_spec, out_spec),
        compiler_params=_cparams(("parallel", "parallel"), 32),
        name=f"attn{g}",
    )(view, view, view)
    return o.reshape(B * S, ATT_OUT_WIDTH), l.reshape(B * S, ATT_OUT_WIDTH)


def _ret_gamma():
    return [1.0 - 2.0 ** (-5.0 - h) for h in range(RET_HEADS)]


def _ret_tables(S):
    half = RET_KEY_DIM // 2
    inv_freq = 1.0 / (ROPE_BASE ** (jnp.arange(half, dtype=F32) / half))
    ang = jnp.arange(S).astype(F32)[:, None] * inv_freq[None]
    cos = jnp.cos(ang)
    sin = jnp.sin(ang)
    cos_t = jnp.tile(jnp.concatenate([cos, cos], axis=-1), (1, RET_HEADS))
    sin_t = jnp.tile(jnp.concatenate([-sin, sin], axis=-1), (1, RET_HEADS))
    log_gamma = jnp.log1p(-jnp.exp2(-5.0 - jnp.arange(RET_HEADS, dtype=F32)))
    idx = jnp.arange(RET_CHUNK, dtype=F32)
    rel = idx[:, None] - idx[None, :]
    decay = jnp.where(rel[None] >= 0, jnp.exp(jnp.maximum(rel, 0.0)[None] * log_gamma[:, None, None]), 0.0)
    zeta = jnp.exp((RET_CHUNK - 1 - idx)[None] * log_gamma[:, None])
    xi = jnp.exp((idx + 1)[None] * log_gamma[:, None])
    zeta_t = jnp.repeat(zeta.T, RET_KEY_DIM, axis=1) * (RET_KEY_DIM ** -0.5)
    xi_t = jnp.repeat(xi.T, RET_KEY_DIM, axis=1)
    g_chunk = jnp.exp(RET_CHUNK * log_gamma)
    g_t = jnp.broadcast_to(g_chunk[:, None, None], (RET_HEADS, 1, RET_VAL_DIM))
    return cos_t, sin_t, decay, zeta_t, xi_t, g_t


def _rot_half_swap(t):
    w = t.shape[-1]
    lane = lax.broadcasted_iota(jnp.int32, t.shape, 1) % RET_KEY_DIM
    half = RET_KEY_DIM // 2
    return jnp.where(lane < half, pltpu.roll(t, w - half, 1), pltpu.roll(t, half, 1))


def _retn_kernel(q_ref, k_ref, v_ref, g_ref, cos_ref, sin_ref, decay_ref, zeta_ref, xi_ref, gch_ref,
                 o_ref, state_ref):
    @pl.when(pl.program_id(1) == 0)
    def _():
        state_ref[...] = jnp.zeros_like(state_ref)

    cos = cos_ref[...]
    sin = sin_ref[...]
    q = q_ref[...].astype(F32)
    k = k_ref[...].astype(F32)
    q = q * cos + _rot_half_swap(q) * sin
    k = k * cos + _rot_half_swap(k) * sin
    qb = q.astype(BF16)
    kb = (k * (RET_KEY_DIM ** -0.5)).astype(BF16)
    q_xi = (q * xi_ref[...]).astype(BF16)
    kz_t = (k * zeta_ref[...]).T
    for h in range(RET_HEADS):
        ks = slice(h * RET_KEY_DIM, (h + 1) * RET_KEY_DIM)
        vs = slice(h * RET_VAL_DIM, (h + 1) * RET_VAL_DIM)
        v = v_ref[:, vs]
        s = lax.dot_general(qb[:, ks], kb[:, ks], (((1,), (1,)), ((), ())), preferred_element_type=F32)
        s = (s * decay_ref[h]).astype(BF16)
        st = state_ref[h]
        o = jnp.dot(s, v, preferred_element_type=F32)
        o = o + jnp.dot(q_xi[:, ks], st.astype(BF16), preferred_element_type=F32)
        state_ref[h] = st * gch_ref[h] + jnp.dot(kz_t[ks, :].astype(BF16), v, preferred_element_type=F32)
        o = o * lax.rsqrt(jnp.mean(o * o, axis=-1, keepdims=True) + EPS)
        o_ref[:, vs] = (_silu(g_ref[:, vs].astype(F32)) * o).astype(o_ref.dtype)


def _retention(proj3, B, S):
    C = RET_CHUNK
    cos_t, sin_t, decay, zeta_t, xi_t, g_t = _ret_tables(S)
    const2 = lambda b, c: (0, 0)
    const3 = lambda b, c: (0, 0, 0)
    out = pl.pallas_call(
        _retn_kernel,
        out_shape=jax.ShapeDtypeStruct((B, S, RET_V_WIDTH), BF16),
        grid=(B, S // C),
        in_specs=[
            pl.BlockSpec((None, C, RET_QK_WIDTH), lambda b, c: (b, c, 8)),
            pl.BlockSpec((None, C, RET_QK_WIDTH), lambda b, c: (b, c, 9)),
            pl.BlockSpec((None, C, RET_V_WIDTH), lambda b, c: (b, c, 0)),
            pl.BlockSpec((None, C, RET_V_WIDTH), lambda b, c: (b, c, 1)),
            pl.BlockSpec((C, RET_QK_WIDTH), lambda b, c: (c, 0)),
            pl.BlockSpec((C, RET_QK_WIDTH), lambda b, c: (c, 0)),
            pl.BlockSpec((RET_HEADS, C, C), const3),
            pl.BlockSpec((C, RET_QK_WIDTH), const2),
            pl.BlockSpec((C, RET_QK_WIDTH), const2),
            pl.BlockSpec((RET_HEADS, 1, RET_VAL_DIM), const3),
        ],
        out_specs=pl.BlockSpec((None, C, RET_V_WIDTH), lambda b, c: (b, c, 0)),
        scratch_shapes=[pltpu.VMEM((RET_HEADS, RET_KEY_DIM, RET_VAL_DIM), F32)],
        compiler_params=_cparams(("parallel", "arbitrary"), 32),
        name="retn",
    )(proj3, proj3, proj3, proj3, cos_t, sin_t, decay, zeta_t, xi_t, g_t)
    return out.reshape(B * S, RET_V_WIDTH)


def _postmix_kernel(o0_ref, o1_ref, o2_ref, l0_ref, l1_ref, l2_ref, ret_ref, ga_ref, gr_ref, x_ref,
                    wa_ref, wr_ref, wo_ref, npost_ref, g1_ref, npre_ref, sc2_ref, sh2_ref, *rest, with_router):
    if with_router:
        wrt_ref, brt_ref, xo_ref, h_ref, lg_ref = rest
    else:
        xo_ref, h_ref = rest
    l0, l1, l2 = l0_ref[...], l1_ref[...], l2_ref[...]
    m = jnp.maximum(jnp.maximum(l0, l1), l2)
    e0, e1, e2 = jnp.exp(l0 - m), jnp.exp(l1 - m), jnp.exp(l2 - m)
    att = (e0 * o0_ref[...].astype(F32) + e1 * o1_ref[...].astype(F32) + e2 * o2_ref[...].astype(F32)) / (e0 + e1 + e2)
    a = jnp.dot(att.astype(BF16), wa_ref[...], preferred_element_type=F32)
    r = jnp.dot(ret_ref[...], wr_ref[...], preferred_element_type=F32)
    merged = jax.nn.sigmoid(ga_ref[...].astype(F32)) * a + jax.nn.sigmoid(gr_ref[...].astype(F32)) * r
    y = jnp.dot(merged.astype(BF16), wo_ref[...], preferred_element_type=F32)
    x = x_ref[...] + g1_ref[...] * _rms(y, npost_ref[...])
    xo_ref[...] = x
    h = _rms(x, npre_ref[...]) * (1.0 + sc2_ref[...]) + sh2_ref[...]
    h_ref[...] = h.astype(BF16)
    if with_router:
        lg_ref[...] = jnp.dot(h, wrt_ref[...], preferred_element_type=F32,
                              precision=lax.Precision.HIGHEST) + brt_ref[...]


def _postmix(outs, lses, ret, proj, x2, wa, wr, wo, npost, npre, mod, layer, S, router=None, tm=512):
    R, D = x2.shape
    tpb = S // tm
    row = lambda w: pl.BlockSpec((tm, w), lambda i: (i, 0))
    const = lambda a: pl.BlockSpec(a.shape, lambda i: (0, 0))
    in_specs = ([row(ATT_OUT_WIDTH)] * 6 + [
        row(RET_V_WIDTH),
        pl.BlockSpec((tm, D), lambda i: (i, 2)),
        pl.BlockSpec((tm, D), lambda i: (i, 3)),
        row(D), const(wa), const(wr), const(wo), const(npost),
        _mod_spec(layer, 2, tpb), const(npre), _mod_spec(layer, 4, tpb), _mod_spec(layer, 3, tpb)])
    args = list(outs) + list(lses) + [ret, proj, proj, x2, wa, wr, wo, npost, mod, npre, mod, mod]
    out_shape = [jax.ShapeDtypeStruct((R, D), F32), jax.ShapeDtypeStruct((R, D), BF16)]
    out_specs = [row(D), row(D)]
    if router is not None:
        in_specs += [const(router[0]), const(router[1])]
        args += list(router)
        out_shape.append(jax.ShapeDtypeStruct((R, LANE), F32))
        out_specs.append(row(LANE))
    return pl.pallas_call(
        functools.partial(_postmix_kernel, with_router=router is not None),
        out_shape=tuple(out_shape),
        grid=(R // tm,),
        in_specs=in_specs,
        out_specs=tuple(out_specs),
        compiler_params=_cparams(("parallel",), 48),
        name="postmix",
    )(*args)


FF_TILE = D_FF // 2


def _ffn_kernel(h_ref, x_ref, wg_ref, wu_ref, wd_ref, npost_ref, g2_ref, o_ref, acc_ref):
    j = pl.program_id(1)

    @pl.when(j == 0)
    def _():
        acc_ref[...] = jnp.zeros_like(acc_ref)

    h = h_ref[...]
    act = _silu(jnp.dot(h, wg_ref[...], preferred_element_type=F32)) * jnp.dot(h, wu_ref[...], preferred_element_type=F32)
    acc_ref[...] += jnp.dot(act.astype(BF16), wd_ref[...], preferred_element_type=F32)

    @pl.when(j == pl.num_programs(1) - 1)
    def _():
        o_ref[...] = x_ref[...] + g2_ref[...] * _rms(acc_ref[...], npost_ref[...])


def _ffn(h2, x2, wg, wu, wd, npost, mod, layer, S, tm=512):
    R, D = x2.shape
    tpb = S // tm
    nf = D_FF // FF_TILE
    return pl.pallas_call(
        _ffn_kernel,
        out_shape=jax.ShapeDtypeStruct((R, D), F32),
        grid=(R // tm, nf),
        in_specs=[
            pl.BlockSpec((tm, D), lambda i, j: (i, 0)),
            pl.BlockSpec((tm, D), lambda i, j: (i, 0)),
            pl.BlockSpec((D, FF_TILE), lambda i, j: (0, j)),
            pl.BlockSpec((D, FF_TILE), lambda i, j: (0, j)),
            pl.BlockSpec((FF_TILE, D), lambda i, j: (j, 0)),
            pl.BlockSpec((1, D), lambda i, j: (0, 0)),
            _mod_spec(layer, 5, tpb),
        ],
        out_specs=pl.BlockSpec((tm, D), lambda i, j: (i, 0)),
        scratch_shapes=[pltpu.VMEM((tm, D), F32)],
        compiler_params=_cparams(("parallel", "arbitrary"), 48),
        name="ffn",
    )(h2, x2, wg, wu, wd, npost, mod)


def _moe_kernel(h_ref, lg_ref, x_ref, wg_ref, wu_ref, wd_ref, npost_ref, g2_ref, o_ref, acc_ref, comb_ref):
    e = pl.program_id(1)
    j = pl.program_id(2)
    first = (e == 0) & (j == 0)
    last = (e == pl.num_programs(1) - 1) & (j == pl.num_programs(2) - 1)

    @pl.when(first)
    def _():
        acc_ref[...] = jnp.zeros_like(acc_ref)
        lg = lg_ref[...]
        lane = lax.broadcasted_iota(jnp.int32, lg.shape, 1)
        lg = jnp.where(lane < N_EXPERTS, lg, -jnp.inf)
        m1 = jnp.max(lg, axis=-1, keepdims=True)
        i1 = jnp.min(jnp.where(lg == m1, lane, LANE), axis=-1, keepdims=True)
        lg2 = jnp.where(lane == i1, -jnp.inf, lg)
        m2 = jnp.max(lg2, axis=-1, keepdims=True)
        i2 = jnp.min(jnp.where(lg2 == m2, lane, LANE), axis=-1, keepdims=True)
        t = jnp.exp(m2 - m1)
        w1 = 1.0 / (1.0 + t)
        w2 = t / (1.0 + t)
        comb_ref[...] = jnp.where(lane == i1, w1, 0.0) + jnp.where(lane == i2, w2, 0.0)

    comb = comb_ref[...]
    lane = lax.broadcasted_iota(jnp.int32, comb.shape, 1)
    cw = jnp.sum(jnp.where(lane == e, comb, 0.0), axis=-1, keepdims=True)
    h = h_ref[...]
    act = _silu(jnp.dot(h, wg_ref[...], preferred_element_type=F32)) * jnp.dot(h, wu_ref[...], preferred_element_type=F32)
    y = jnp.dot(act.astype(BF16), wd_ref[...], preferred_element_type=F32)
    acc_ref[...] += cw * y

    @pl.when(last)
    def _():
        o_ref[...] = x_ref[...] + g2_ref[...] * _rms(acc_ref[...], npost_ref[...])


def _moe(h2, logits, x2, wg, wu, wd, npost, mod, layer, S, tm=512):
    R, D = x2.shape
    tpb = S // tm
    nf = D_FF // FF_TILE
    return pl.pallas_call(
        _moe_kernel,
        out_shape=jax.ShapeDtypeStruct((R, D), F32),
        grid=(R // tm, N_EXPERTS, nf),
        in_specs=[
            pl.BlockSpec((tm, D), lambda i, e, j: (i, 0)),
            pl.BlockSpec((tm, LANE), lambda i, e, j: (i, 0)),
            pl.BlockSpec((tm, D), lambda i, e, j: (i, 0)),
            pl.BlockSpec((None, D, FF_TILE), lambda i, e, j: (e, 0, j)),
            pl.BlockSpec((None, D, FF_TILE), lambda i, e, j: (e, 0, j)),
            pl.BlockSpec((None, FF_TILE, D), lambda i, e, j: (e, j, 0)),
            pl.BlockSpec((1, D), lambda i, e, j: (0, 0)),
            _mod_spec(layer, 5, tpb),
        ],
        out_specs=pl.BlockSpec((tm, D), lambda i, e, j: (i, 0)),
        scratch_shapes=[pltpu.VMEM((tm, D), F32), pltpu.VMEM((tm, LANE), F32)],
        compiler_params=_cparams(("parallel", "arbitrary", "arbitrary"), 48),
        name="moe",
    )(h2, logits, x2, wg, wu, wd, npost, mod)


def _permute_in_cols(w):
    bounds = np.concatenate([[0], np.cumsum(IN_SIZES)])
    return jnp.concatenate([w[:, bounds[s]:bounds[s + 1]] for s in _SEG_ORDER], axis=1)


def kernel(x, c, w_ada, b_ada, norm_pre_mix, w_in, w_att_proj, w_ret_proj, w_mix_out, norm_post_mix, norm_pre_ffn, w_ffn_gate, w_ffn_up, w_ffn_down, w_router, b_router, w_exp_gate, w_exp_up, w_exp_down, norm_post_ffn):
    B, S, D = x.shape
    R = B * S
    mod = _ada(c, w_ada, b_ada)
    x2 = x.reshape(R, D)
    row = lambda v: v.reshape(1, D)
    for layer in range(DEPTH):
        w_in_b = _permute_in_cols(w_in[layer]).astype(BF16)
        proj = _inproj(x2, row(norm_pre_mix[layer]), mod, layer, w_in_b, S)
        proj3 = proj.reshape(B, S, IN_COLS)
        outs, lses = zip(*[_attn_group(proj3, g, B, S) for g in range(len(ATT_GROUPS))])
        ret = _retention(proj3, B, S)
        j = layer // 2
        router = None
        if layer % 2 == 1:
            wrt = jnp.zeros((D, LANE), F32).at[:, :N_EXPERTS].set(w_router[j])
            brt = jnp.zeros((1, LANE), F32).at[0, :N_EXPERTS].set(b_router[j])
            router = (wrt, brt)
        res = _postmix(outs, lses, ret, proj, x2, w_att_proj[layer].astype(BF16), w_ret_proj[layer].astype(BF16),
                       w_mix_out[layer].astype(BF16), row(norm_post_mix[layer]), row(norm_pre_ffn[layer]),
                       mod, layer, S, router=router)
        if layer % 2 == 0:
            x2, h2 = res
            x2 = _ffn(h2, x2, w_ffn_gate[j].astype(BF16), w_ffn_up[j].astype(BF16), w_ffn_down[j].astype(BF16),
                      row(norm_post_ffn[layer]), mod, layer, S)
        else:
            x2, h2, logits = res
            x2 = _moe(h2, logits, x2, w_exp_gate[j].astype(BF16), w_exp_up[j].astype(BF16),
                      w_exp_down[j].astype(BF16), row(norm_post_ffn[layer]), mod, layer, S)
    return x2.reshape(B, S, D)
```

```python
import functools

import numpy as np
import jax
import jax.numpy as jnp
from jax import lax
from jax.experimental import pallas as pl
from jax.experimental.pallas import tpu as pltpu

F32 = jnp.float32
BF16 = jnp.bfloat16
I32 = jnp.int32

D_MODEL = 1024
DEPTH = 2
ATT_GROUPS = ((128, 1), (512, 4), (2048, 16))
HEADS_PER_GROUP = 4
N_ATT_HEADS = HEADS_PER_GROUP * len(ATT_GROUPS)
ATT_HEAD_DIM = 64
ATT_WIDTH = N_ATT_HEADS * ATT_HEAD_DIM
ATT_OUT_WIDTH = HEADS_PER_GROUP * ATT_HEAD_DIM
BLOCK = 128
ALIBI_MAX = 8.0
NEG_INF = -1e30
RET_HEADS = 8
RET_KEY_DIM = 64
RET_VAL_DIM = 128
RET_QK_WIDTH = RET_HEADS * RET_KEY_DIM
RET_V_WIDTH = RET_HEADS * RET_VAL_DIM
RET_CHUNK = 128
ROPE_BASE = 10000.0
D_FF = 2816
N_EXPERTS = 8
TOP_K = 2
EPS = 1e-6
IN_SIZES = (ATT_WIDTH, ATT_WIDTH, ATT_WIDTH, RET_QK_WIDTH, RET_QK_WIDTH, RET_V_WIDTH, RET_V_WIDTH, D_MODEL, D_MODEL)
IN_COLS = sum(IN_SIZES)

LANE = 128
MAIN_COLS = 4 * D_MODEL + 2 * RET_QK_WIDTH
GROUP_COLS = 3 * ATT_OUT_WIDTH
MXU_N = 256

MiB = 1024 * 1024


def _cparams(sem, vmem_mib):
    return pltpu.CompilerParams(dimension_semantics=sem, vmem_limit_bytes=vmem_mib * MiB)


def _rms(x, gain):
    return x * lax.rsqrt(jnp.mean(x * x, axis=-1, keepdims=True) + EPS) * gain


def _silu(x):
    return x * jax.nn.sigmoid(x)


def _ada_kernel(c_ref, w_ref, b_ref, o_ref):
    ca = _silu(c_ref[...])
    o_ref[0] = jnp.dot(ca, w_ref[0], preferred_element_type=F32, precision=lax.Precision.HIGHEST) + b_ref[0]


def _ada(c, w_ada, b_ada):
    B, D = c.shape
    out = pl.pallas_call(
        _ada_kernel,
        out_shape=jax.ShapeDtypeStruct((DEPTH * 6, B, D), F32),
        grid=(DEPTH, 6),
        in_specs=[
            pl.BlockSpec((B, D), lambda l, k: (0, 0)),
            pl.BlockSpec((1, D, D), lambda l, k: (l, 0, k)),
            pl.BlockSpec((1, 1, D), lambda l, k: (l, 0, k)),
        ],
        out_specs=pl.BlockSpec((1, B, D), lambda l, k: (l * 6 + k, 0, 0)),
        compiler_params=_cparams(("arbitrary", "arbitrary"), 32),
        name="ada",
    )(c, w_ada, b_ada.reshape(DEPTH, 1, 6 * D))
    return out.reshape(DEPTH, 6, B, 1, D)


def _mod_spec(layer, k, tiles_per_batch):
    return pl.BlockSpec((None, None, None, 1, D_MODEL),
                        lambda i, *_: (layer, k, i // tiles_per_batch, 0, 0))


def _inproj_kernel(x_ref, gain_ref, sc_ref, sh_ref, w_ref, main_ref, a0_ref, a1_ref, a2_ref, scr_ref):
    tm = x_ref.shape[0]
    h = _rms(x_ref[...], gain_ref[...]) * (1.0 + sc_ref[...]) + sh_ref[...]
    hb = h.astype(BF16)
    for c in range(MAIN_COLS // MXU_N):
        sl = slice(c * MXU_N, (c + 1) * MXU_N)
        main_ref[:, sl] = jnp.dot(hb, w_ref[:, sl], preferred_element_type=F32).astype(BF16)
    att_refs = (a0_ref, a1_ref, a2_ref)
    slot = 0
    for g, (_, d) in enumerate(ATT_GROUPS):
        for seg in range(3):
            col = MAIN_COLS + g * GROUP_COLS + seg * ATT_OUT_WIDTH
            res = jnp.dot(hb, w_ref[:, col:col + ATT_OUT_WIDTH], preferred_element_type=F32)
            if d == 1:
                att_refs[g][0, :, seg * ATT_OUT_WIDTH:(seg + 1) * ATT_OUT_WIDTH] = res.astype(BF16)
                continue
            for s in range(ATT_OUT_WIDTH // LANE):
                scr_ref[slot + s] = res[:, s * LANE:(s + 1) * LANE]
            for r in range(d):
                for s in range(ATT_OUT_WIDTH // LANE):
                    c0 = seg * ATT_OUT_WIDTH + s * LANE
                    att_refs[g][r, :, c0:c0 + LANE] = scr_ref[slot + s, pl.ds(r, tm // d, stride=d), :].astype(BF16)
            slot += ATT_OUT_WIDTH // LANE


def _inproj(x2, gain, mod, layer, w_in_b, B, S, tm=512):
    R, D = x2.shape
    tpb = S // tm
    n_slabs = sum(3 * (ATT_OUT_WIDTH // LANE) for _, d in ATT_GROUPS if d > 1)

    def att_shape(d):
        return jax.ShapeDtypeStruct((B, d, S // d, GROUP_COLS), BF16)

    def att_spec(d):
        return pl.BlockSpec((None, d, tm // d, GROUP_COLS), lambda i: (i // tpb, 0, i % tpb, 0))

    return pl.pallas_call(
        _inproj_kernel,
        out_shape=(jax.ShapeDtypeStruct((R, MAIN_COLS), BF16),) + tuple(att_shape(d) for _, d in ATT_GROUPS),
        grid=(R // tm,),
        in_specs=[
            pl.BlockSpec((tm, D), lambda i: (i, 0)),
            pl.BlockSpec((1, D), lambda i: (0, 0)),
            _mod_spec(layer, 1, tpb),
            _mod_spec(layer, 0, tpb),
            pl.BlockSpec((D, IN_COLS), lambda i: (0, 0), pipeline_mode=pl.Buffered(1)),
        ],
        out_specs=(pl.BlockSpec((tm, MAIN_COLS), lambda i: (i, 0)),) + tuple(att_spec(d) for _, d in ATT_GROUPS),
        scratch_shapes=[pltpu.VMEM((n_slabs, tm, LANE), F32)],
        compiler_params=_cparams(("arbitrary",), 48),
        name="inproj",
    )(x2, gain, mod, mod, w_in_b)


def _prep_w_in(w):
    b = np.concatenate([[0], np.cumsum(IN_SIZES)])
    seg = lambda s: w[:, b[s]:b[s + 1]]
    cols = [seg(5), seg(6), seg(7), seg(8), seg(3), seg(4)]
    q_a, k_a, v_a = seg(0) * (ATT_HEAD_DIM ** -0.5), seg(1), seg(2)
    for g in range(len(ATT_GROUPS)):
        gs = slice(g * ATT_OUT_WIDTH, (g + 1) * ATT_OUT_WIDTH)
        cols += [q_a[:, gs], k_a[:, gs], v_a[:, gs]]
    return jnp.concatenate(cols, axis=1).astype(BF16)


def _alibi_slopes():
    return [2.0 ** (-ALIBI_MAX * h / N_ATT_HEADS) for h in range(1, N_ATT_HEADS + 1)]


def _attn_kernel(a_ref, o_ref, l_ref, *, n, n_w, dilation, slopes):
    nb = n // BLOCK
    win = min(2 * BLOCK, n)
    W = ATT_OUT_WIDTH

    def unit(u, carry):
        r = u // nb
        i = u % nb
        q0 = pl.multiple_of(i * BLOCK, BLOCK)
        k0 = pl.multiple_of(jnp.clip((i - 1) * BLOCK, 0, n - win), BLOCK)
        q = a_ref[r, pl.ds(q0, BLOCK), 0:W]
        k = a_ref[r, pl.ds(k0, win), W:2 * W]
        v = a_ref[r, pl.ds(k0, win), 2 * W:3 * W]
        qi = q0 + lax.broadcasted_iota(I32, (BLOCK, win), 0)
        kj = k0 + lax.broadcasted_iota(I32, (BLOCK, win), 1)
        dist = qi - kj
        valid = (dist >= 0) & (dist <= n_w)
        dist_f = (dilation * dist).astype(F32)
        for h in range(HEADS_PER_GROUP):
            hs = slice(h * ATT_HEAD_DIM, (h + 1) * ATT_HEAD_DIM)
            s = lax.dot_general(q[:, hs], k[:, hs], (((1,), (1,)), ((), ())), preferred_element_type=F32)
            s = jnp.where(valid, s - slopes[h] * dist_f, NEG_INF)
            m = jnp.max(s, axis=-1, keepdims=True)
            p = jnp.exp(s - m)
            den = jnp.sum(p, axis=-1, keepdims=True)
            o = jnp.dot(p.astype(BF16), v[:, hs], preferred_element_type=F32) / den
            o_ref[r, pl.ds(q0, BLOCK), hs] = o.astype(o_ref.dtype)
            l_ref[r, pl.ds(q0, BLOCK), hs] = jnp.broadcast_to(m + jnp.log(den), (BLOCK, ATT_HEAD_DIM))
        return carry

    lax.fori_loop(0, dilation * nb, unit, 0)


def _attn_group(att, g):
    window, d = ATT_GROUPS[g]
    B, _, n, _ = att.shape
    assert n % BLOCK == 0
    slopes = _alibi_slopes()[g * HEADS_PER_GROUP:(g + 1) * HEADS_PER_GROUP]
    kern = functools.partial(_attn_kernel, n=n, n_w=window // d, dilation=d, slopes=slopes)
    out_spec = pl.BlockSpec((None, d, n, ATT_OUT_WIDTH), lambda b: (b, 0, 0, 0))
    return pl.pallas_call(
        kern,
        out_shape=(jax.ShapeDtypeStruct((B, d, n, ATT_OUT_WIDTH), BF16),
                   jax.ShapeDtypeStruct((B, d, n, ATT_OUT_WIDTH), F32)),
        grid=(B,),
        in_specs=[pl.BlockSpec((None, d, n, GROUP_COLS), lambda b: (b, 0, 0, 0))],
        out_specs=(out_spec, out_spec),
        compiler_params=_cparams(("arbitrary",), 40),
        name=f"attn{g}",
    )(att)


def _ret_tables(S):
    half = RET_KEY_DIM // 2
    inv_freq = 1.0 / (ROPE_BASE ** (jnp.arange(half, dtype=F32) / half))
    ang = jnp.arange(S).astype(F32)[:, None] * inv_freq[None]
    cos = jnp.cos(ang)
    sin = jnp.sin(ang)
    cos_t = jnp.tile(jnp.concatenate([cos, cos], axis=-1), (1, RET_HEADS))
    sin_t = jnp.tile(jnp.concatenate([-sin, sin], axis=-1), (1, RET_HEADS))
    log_gamma = jnp.log1p(-jnp.exp2(-5.0 - jnp.arange(RET_HEADS, dtype=F32)))
    idx = jnp.arange(RET_CHUNK, dtype=F32)
    rel = idx[:, None] - idx[None, :]
    decay = jnp.where(rel[None] >= 0, jnp.exp(jnp.maximum(rel, 0.0)[None] * log_gamma[:, None, None]), 0.0)
    zeta = jnp.exp((RET_CHUNK - 1 - idx)[None] * log_gamma[:, None])
    xi = jnp.exp((idx + 1)[None] * log_gamma[:, None])
    zeta_t = jnp.repeat(zeta.T, RET_KEY_DIM, axis=1) * (RET_KEY_DIM ** -0.5)
    xi_t = jnp.repeat(xi.T, RET_KEY_DIM, axis=1)
    g_chunk = jnp.exp(RET_CHUNK * log_gamma)
    g_t = jnp.broadcast_to(g_chunk[:, None, None], (RET_HEADS, 1, RET_VAL_DIM))
    return cos_t, sin_t, decay, zeta_t, xi_t, g_t


def _rot_half_swap(t):
    w = t.shape[-1]
    lane = lax.broadcasted_iota(I32, t.shape, 1) % RET_KEY_DIM
    half = RET_KEY_DIM // 2
    return jnp.where(lane < half, pltpu.roll(t, w - half, 1), pltpu.roll(t, half, 1))


def _retn_kernel(q_ref, k_ref, v_ref, g_ref, cos_ref, sin_ref, decay_ref, zeta_ref, xi_ref, gch_ref,
                 o_ref, state_ref):
    @pl.when(pl.program_id(1) == 0)
    def _():
        state_ref[...] = jnp.zeros_like(state_ref)

    cos = cos_ref[...]
    sin = sin_ref[...]
    q = q_ref[...].astype(F32)
    k = k_ref[...].astype(F32)
    q = q * cos + _rot_half_swap(q) * sin
    k = k * cos + _rot_half_swap(k) * sin
    qb = q.astype(BF16)
    kb = (k * (RET_KEY_DIM ** -0.5)).astype(BF16)
    q_xi = (q * xi_ref[...]).astype(BF16)
    kz_t = (k * zeta_ref[...]).T
    for h in range(RET_HEADS):
        ks = slice(h * RET_KEY_DIM, (h + 1) * RET_KEY_DIM)
        vs = slice(h * RET_VAL_DIM, (h + 1) * RET_VAL_DIM)
        v = v_ref[:, vs]
        s = lax.dot_general(qb[:, ks], kb[:, ks], (((1,), (1,)), ((), ())), preferred_element_type=F32)
        s = (s * decay_ref[h]).astype(BF16)
        st = state_ref[h]
        o = jnp.dot(s, v, preferred_element_type=F32)
        o = o + jnp.dot(q_xi[:, ks], st.astype(BF16), preferred_element_type=F32)
        state_ref[h] = st * gch_ref[h] + jnp.dot(kz_t[ks, :].astype(BF16), v, preferred_element_type=F32)
        o = o * lax.rsqrt(jnp.mean(o * o, axis=-1, keepdims=True) + EPS)
        o_ref[:, vs] = (_silu(g_ref[:, vs].astype(F32)) * o).astype(o_ref.dtype)


def _retention(main3, B, S):
    C = RET_CHUNK
    cos_t, sin_t, decay, zeta_t, xi_t, g_t = _ret_tables(S)
    const2 = lambda b, c: (0, 0)
    const3 = lambda b, c: (0, 0, 0)
    qk0 = 4 * D_MODEL // RET_QK_WIDTH
    out = pl.pallas_call(
        _retn_kernel,
        out_shape=jax.ShapeDtypeStruct((B, S, RET_V_WIDTH), BF16),
        grid=(B, S // C),
        in_specs=[
            pl.BlockSpec((None, C, RET_QK_WIDTH), lambda b, c: (b, c, qk0)),
            pl.BlockSpec((None, C, RET_QK_WIDTH), lambda b, c: (b, c, qk0 + 1)),
            pl.BlockSpec((None, C, RET_V_WIDTH), lambda b, c: (b, c, 0)),
            pl.BlockSpec((None, C, RET_V_WIDTH), lambda b, c: (b, c, 1)),
            pl.BlockSpec((C, RET_QK_WIDTH), lambda b, c: (c, 0)),
            pl.BlockSpec((C, RET_QK_WIDTH), lambda b, c: (c, 0)),
            pl.BlockSpec((RET_HEADS, C, C), const3),
            pl.BlockSpec((C, RET_QK_WIDTH), const2),
            pl.BlockSpec((C, RET_QK_WIDTH), const2),
            pl.BlockSpec((RET_HEADS, 1, RET_VAL_DIM), const3),
        ],
        out_specs=pl.BlockSpec((None, C, RET_V_WIDTH), lambda b, c: (b, c, 0)),
        scratch_shapes=[pltpu.VMEM((RET_HEADS, RET_KEY_DIM, RET_VAL_DIM), F32)],
        compiler_params=_cparams(("arbitrary", "arbitrary"), 32),
        name="retn",
    )(main3, main3, main3, main3, cos_t, sin_t, decay, zeta_t, xi_t, g_t)
    return out.reshape(B * S, RET_V_WIDTH)


N_META = 6


def _unstride(ref, scr_ref, slot, d, tm):
    if d == 1:
        return ref[0].astype(F32)
    for r in range(d):
        val = ref[r].astype(F32)
        for s in range(ATT_OUT_WIDTH // LANE):
            scr_ref[slot + s, pl.ds(r, tm // d, stride=d), :] = val[:, s * LANE:(s + 1) * LANE]
    return jnp.concatenate([scr_ref[slot + s] for s in range(ATT_OUT_WIDTH // LANE)], axis=1)


def _postmix_kernel(o0_ref, o1_ref, o2_ref, l0_ref, l1_ref, l2_ref, ret_ref, ga_ref, gr_ref, x_ref,
                    wa_ref, wr_ref, wo_ref, npost_ref, g1_ref, npre_ref, sc2_ref, sh2_ref, *rest, with_router):
    if with_router:
        wrt_ref, brt_ref, xo_ref, h_ref, meta_ref, cnt_ref, scr_ref, carry_ref = rest
    else:
        xo_ref, h_ref, scr_ref = rest
    tm = x_ref.shape[0]
    per = ATT_OUT_WIDTH // LANE
    os_, ls_ = [], []
    slot = 0
    for (_, d), o_ref, l_ref in zip(ATT_GROUPS, (o0_ref, o1_ref, o2_ref), (l0_ref, l1_ref, l2_ref)):
        os_.append(_unstride(o_ref, scr_ref, slot, d, tm))
        ls_.append(_unstride(l_ref, scr_ref, slot + per, d, tm))
        if d > 1:
            slot += 2 * per
    l0, l1, l2 = ls_
    m = jnp.maximum(jnp.maximum(l0, l1), l2)
    e0, e1, e2 = jnp.exp(l0 - m), jnp.exp(l1 - m), jnp.exp(l2 - m)
    att = (e0 * os_[0] + e1 * os_[1] + e2 * os_[2]) / (e0 + e1 + e2)
    a = jnp.dot(att.astype(BF16), wa_ref[...], preferred_element_type=F32)
    r = jnp.dot(ret_ref[...], wr_ref[...], preferred_element_type=F32)
    merged = jax.nn.sigmoid(ga_ref[...].astype(F32)) * a + jax.nn.sigmoid(gr_ref[...].astype(F32)) * r
    y = jnp.dot(merged.astype(BF16), wo_ref[...], preferred_element_type=F32)
    x = x_ref[...] + g1_ref[...] * _rms(y, npost_ref[...])
    xo_ref[...] = x
    h = _rms(x, npre_ref[...]) * (1.0 + sc2_ref[...]) + sh2_ref[...]
    h_ref[...] = h.astype(h_ref.dtype)
    if not with_router:
        return

    @pl.when(pl.program_id(0) == 0)
    def _():
        carry_ref[...] = jnp.zeros_like(carry_ref)

    h_hi = h.astype(BF16)
    h_lo = (h - h_hi.astype(F32)).astype(BF16)
    w = wrt_ref[...]
    w_hi = w.astype(BF16)
    w_lo = (w - w_hi.astype(F32)).astype(BF16)
    lg = (jnp.dot(h_hi, w_hi, preferred_element_type=F32) + jnp.dot(h_hi, w_lo, preferred_element_type=F32)
          + jnp.dot(h_lo, w_hi, preferred_element_type=F32)) + brt_ref[...]
    lane = lax.broadcasted_iota(I32, lg.shape, 1).astype(F32)
    lg = jnp.where(lane < N_EXPERTS, lg, -jnp.inf)
    m1 = jnp.max(lg, axis=-1, keepdims=True)
    i1 = jnp.min(jnp.where(lg == m1, lane, float(LANE)), axis=-1, keepdims=True)
    lg2 = jnp.where(lane == i1, -jnp.inf, lg)
    m2 = jnp.max(lg2, axis=-1, keepdims=True)
    i2 = jnp.min(jnp.where(lg2 == m2, lane, float(LANE)), axis=-1, keepdims=True)
    t = jnp.exp(m2 - m1)
    w1 = 1.0 / (1.0 + t)
    w2 = t / (1.0 + t)
    oh = jnp.where((lane == i1) | (lane == i2), 1.0, 0.0)
    row = lax.broadcasted_iota(I32, (tm, tm), 0)
    col = lax.broadcasted_iota(I32, (tm, tm), 1)
    tri = jnp.where(col < row, 1.0, 0.0).astype(BF16)
    pos = jnp.dot(tri, oh.astype(BF16), preferred_element_type=F32) + carry_ref[...]
    r1 = jnp.sum(jnp.where(lane == i1, pos, 0.0), axis=-1, keepdims=True)
    r2 = jnp.sum(jnp.where(lane == i2, pos, 0.0), axis=-1, keepdims=True)
    carry = carry_ref[...] + jnp.sum(oh, axis=0, keepdims=True)
    carry_ref[...] = carry
    cnt_ref[...] = jnp.broadcast_to(carry, cnt_ref.shape)
    meta = jnp.zeros(lg.shape, F32)
    for j, val in enumerate((i1, i2, r1, r2, w1, w2)):
        meta = jnp.where(lane == j, val, meta)
    meta_ref[...] = meta


def _postmix(outs, lses, ret, main, x2, wa, wr, wo, npost, npre, mod, layer, S, router=None, tm=512):
    R, D = x2.shape
    tpb = S // tm
    row = lambda w: pl.BlockSpec((tm, w), lambda i: (i, 0))
    const = lambda a: pl.BlockSpec(a.shape, lambda i: (0, 0))

    def att_spec(d):
        return pl.BlockSpec((None, d, tm // d, ATT_OUT_WIDTH), lambda i: (i // tpb, 0, i % tpb, 0))

    specs_att = [att_spec(d) for _, d in ATT_GROUPS]
    in_specs = (specs_att + specs_att + [
        row(RET_V_WIDTH),
        pl.BlockSpec((tm, D), lambda i: (i, 2)),
        pl.BlockSpec((tm, D), lambda i: (i, 3)),
        row(D), const(wa), const(wr), const(wo), const(npost),
        _mod_spec(layer, 2, tpb), const(npre), _mod_spec(layer, 4, tpb), _mod_spec(layer, 3, tpb)])
    args = list(outs) + list(lses) + [ret, main, main, x2, wa, wr, wo, npost, mod, npre, mod, mod]
    n_slabs = sum(2 * (ATT_OUT_WIDTH // LANE) for _, d in ATT_GROUPS if d > 1)
    scratch = [pltpu.VMEM((n_slabs, tm, LANE), F32)]
    if router is None:
        out_shape = [jax.ShapeDtypeStruct((R, D), F32), jax.ShapeDtypeStruct((R, D), BF16)]
        out_specs = [row(D), row(D)]
        sem = ("arbitrary",)
    else:
        in_specs += [const(router[0]), const(router[1])]
        args += list(router)
        out_shape = [jax.ShapeDtypeStruct((R, D), F32), jax.ShapeDtypeStruct((R, D), F32),
                     jax.ShapeDtypeStruct((R, LANE), F32), jax.ShapeDtypeStruct((8, LANE), F32)]
        out_specs = [row(D), row(D), row(LANE), pl.BlockSpec((8, LANE), lambda i: (0, 0))]
        scratch.append(pltpu.VMEM((1, LANE), F32))
        sem = ("arbitrary",)
    return pl.pallas_call(
        functools.partial(_postmix_kernel, with_router=router is not None),
        out_shape=tuple(out_shape),
        grid=(R // tm,),
        in_specs=in_specs,
        out_specs=tuple(out_specs),
        scratch_shapes=scratch,
        compiler_params=_cparams(sem, 48),
        name="postmix",
    )(*args)


FF_TILE = D_FF // 2


def _swiglu_partial(h, wg_ref, wu_ref, wd_ref):
    act = _silu(jnp.dot(h, wg_ref[...], preferred_element_type=F32)) * jnp.dot(h, wu_ref[...], preferred_element_type=F32)
    return jnp.dot(act.astype(BF16), wd_ref[...], preferred_element_type=F32)


def _ffn_kernel(h_ref, x_ref, wg_ref, wu_ref, wd_ref, npost_ref, g2_ref, o_ref, acc_ref):
    j = pl.program_id(1)

    @pl.when(j == 0)
    def _():
        acc_ref[...] = jnp.zeros_like(acc_ref)

    acc_ref[...] += _swiglu_partial(h_ref[...], wg_ref, wu_ref, wd_ref)

    @pl.when(j == pl.num_programs(1) - 1)
    def _():
        o_ref[...] = x_ref[...] + g2_ref[...] * _rms(acc_ref[...], npost_ref[...])


def _ffn(h2, x2, wg, wu, wd, npost, mod, layer, S, tm=512):
    R, D = x2.shape
    tpb = S // tm
    nf = D_FF // FF_TILE
    return pl.pallas_call(
        _ffn_kernel,
        out_shape=jax.ShapeDtypeStruct((R, D), F32),
        grid=(R // tm, nf),
        in_specs=[
            pl.BlockSpec((tm, D), lambda i, j: (i, 0)),
            pl.BlockSpec((tm, D), lambda i, j: (i, 0)),
            pl.BlockSpec((D, FF_TILE), lambda i, j: (0, j)),
            pl.BlockSpec((D, FF_TILE), lambda i, j: (0, j)),
            pl.BlockSpec((FF_TILE, D), lambda i, j: (j, 0)),
            pl.BlockSpec((1, D), lambda i, j: (0, 0)),
            _mod_spec(layer, 5, tpb),
        ],
        out_specs=pl.BlockSpec((tm, D), lambda i, j: (i, 0)),
        scratch_shapes=[pltpu.VMEM((tm, D), F32)],
        compiler_params=_cparams(("arbitrary", "arbitrary"), 48),
        name="ffn",
    )(h2, x2, wg, wu, wd, npost, mod)


MOE_TM = 512
ROW_UNROLL = 8


def _invperm_kernel(n_ref, dest_ref, src_ref):
    def zero(g, c):
        for u in range(ROW_UNROLL):
            src_ref[g * ROW_UNROLL + u] = 0
        return c

    lax.fori_loop(0, n_ref[0], zero, 0)

    def put(g, c):
        for u in range(ROW_UNROLL):
            a = g * ROW_UNROLL + u
            src_ref[dest_ref[a]] = a
        return c

    lax.fori_loop(0, n_ref[1], put, 0)


def _invperm(dest, P):
    assert P % ROW_UNROLL == 0 and dest.shape[0] % ROW_UNROLL == 0
    trips = jnp.array([P // ROW_UNROLL, dest.shape[0] // ROW_UNROLL], I32)
    return pl.pallas_call(
        _invperm_kernel,
        out_shape=jax.ShapeDtypeStruct((P,), I32),
        in_specs=[pl.BlockSpec(memory_space=pltpu.SMEM), pl.BlockSpec(memory_space=pltpu.SMEM)],
        out_specs=pl.BlockSpec(memory_space=pltpu.SMEM),
        name="invperm",
    )(trips, dest)


def _gather_kernel(src_ref, nu_ref, h_hbm, o_hbm, sem, *, tm, n_tok):
    j = pl.program_id(0)
    nu = nu_ref[0]

    def row_copy(p, row):
        return pltpu.make_async_copy(h_hbm.at[pl.ds(row, 1), :], o_hbm.at[pl.ds(p, 1), :], sem)

    def wait_tile():
        def body(i, c):
            row_copy(0, 0).wait()
            return c

        lax.fori_loop(0, tm, body, 0, unroll=ROW_UNROLL)

    @pl.when(j < nu)
    def _():
        def body(i, c):
            p = j * tm + i
            row_copy(p, src_ref[p] & (n_tok - 1)).start()
            return c

        lax.fori_loop(0, tm, body, 0, unroll=ROW_UNROLL)

    @pl.when((j > 0) & (j - 1 < nu))
    def _():
        wait_tile()

    @pl.when((j == pl.num_programs(0) - 1) & (j < nu))
    def _():
        wait_tile()


def _gather_rows(src, n_used, h2, n_tiles, tm):
    R, D = h2.shape
    assert R & (R - 1) == 0
    return pl.pallas_call(
        functools.partial(_gather_kernel, tm=tm, n_tok=R),
        out_shape=jax.ShapeDtypeStruct((n_tiles * tm, D), F32),
        grid_spec=pltpu.PrefetchScalarGridSpec(
            num_scalar_prefetch=2,
            grid=(n_tiles,),
            in_specs=[pl.BlockSpec(memory_space=pl.ANY)],
            out_specs=pl.BlockSpec(memory_space=pl.ANY),
            scratch_shapes=[pltpu.SemaphoreType.DMA],
        ),
        compiler_params=_cparams(("arbitrary",), 32),
        name="moe_gather",
    )(src, n_used, h2)


def _expert_kernel(te_ref, nu_ref, h_ref, wg_ref, wu_ref, wd_ref, o_ref, acc_ref, hb_ref):
    j = pl.program_id(0)
    f = pl.program_id(1)

    @pl.when(j < nu_ref[0])
    def _():
        @pl.when(f == 0)
        def _():
            acc_ref[...] = jnp.zeros_like(acc_ref)
            hb_ref[...] = h_ref[...].astype(BF16)

        acc_ref[...] += _swiglu_partial(hb_ref[...], wg_ref, wu_ref, wd_ref)

        @pl.when(f == pl.num_programs(1) - 1)
        def _():
            o_ref[...] = acc_ref[...]


def _experts(tile_expert, n_used, sorted_h, wg, wu, wd, tm):
    P, D = sorted_h.shape
    nf = D_FF // FF_TILE
    tile = lambda j, nu: jnp.minimum(j, nu[0] - 1)
    ff = lambda j, f, nu: jnp.where(j < nu[0], f, nf - 1)
    return pl.pallas_call(
        _expert_kernel,
        out_shape=jax.ShapeDtypeStruct((P, D), F32),
        grid_spec=pltpu.PrefetchScalarGridSpec(
            num_scalar_prefetch=2,
            grid=(P // tm, nf),
            in_specs=[
                pl.BlockSpec((tm, D), lambda j, f, te, nu: (tile(j, nu), 0)),
                pl.BlockSpec((None, D, FF_TILE), lambda j, f, te, nu: (te[tile(j, nu)], 0, ff(j, f, nu))),
                pl.BlockSpec((None, D, FF_TILE), lambda j, f, te, nu: (te[tile(j, nu)], 0, ff(j, f, nu))),
                pl.BlockSpec((None, FF_TILE, D), lambda j, f, te, nu: (te[tile(j, nu)], ff(j, f, nu), 0)),
            ],
            out_specs=pl.BlockSpec((tm, D), lambda j, f, te, nu: (tile(j, nu), 0)),
            scratch_shapes=[pltpu.VMEM((tm, D), F32), pltpu.VMEM((tm, D), BF16)],
        ),
        compiler_params=_cparams(("arbitrary", "arbitrary"), 48),
        name="moe_experts",
    )(tile_expert, n_used, sorted_h, wg, wu, wd)


def _combine_kernel(d1_ref, d2_ref, y_hbm, meta_ref, x_ref, npost_ref, g2_ref, o_ref, ya_ref, yb_ref, sem):
    tm = x_ref.shape[0]
    t0 = pl.program_id(0) * tm

    def row_copy(row, dst_ref, r):
        return pltpu.make_async_copy(y_hbm.at[pl.ds(row, 1), :], dst_ref.at[pl.ds(r, 1), :], sem)

    def issue(r, c):
        row_copy(d1_ref[t0 + r], ya_ref, r).start()
        row_copy(d2_ref[t0 + r], yb_ref, r).start()
        return c

    lax.fori_loop(0, tm, issue, 0, unroll=ROW_UNROLL)

    def wait(r, c):
        row_copy(0, ya_ref, 0).wait()
        row_copy(0, yb_ref, 0).wait()
        return c

    lax.fori_loop(0, tm, wait, 0, unroll=ROW_UNROLL)
    meta = meta_ref[...]
    y = meta[:, 4:5] * ya_ref[...] + meta[:, 5:6] * yb_ref[...]
    o_ref[...] = x_ref[...] + g2_ref[...] * _rms(y, npost_ref[...])


def _combine(dest1, dest2, sorted_y, meta, x2, npost, mod, layer, S, tm=512):
    R, D = x2.shape
    tpb = S // tm
    return pl.pallas_call(
        _combine_kernel,
        out_shape=jax.ShapeDtypeStruct((R, D), F32),
        grid_spec=pltpu.PrefetchScalarGridSpec(
            num_scalar_prefetch=2,
            grid=(R // tm,),
            in_specs=[
                pl.BlockSpec(memory_space=pl.ANY),
                pl.BlockSpec((tm, LANE), lambda i, *_: (i, 0)),
                pl.BlockSpec((tm, D), lambda i, *_: (i, 0)),
                pl.BlockSpec((1, D), lambda i, *_: (0, 0)),
                _mod_spec(layer, 5, tpb),
            ],
            out_specs=pl.BlockSpec((tm, D), lambda i, *_: (i, 0)),
            scratch_shapes=[pltpu.VMEM((tm, D), F32), pltpu.VMEM((tm, D), F32), pltpu.SemaphoreType.DMA],
        ),
        compiler_params=_cparams(("arbitrary",), 40),
        name="moe_combine",
    )(dest1, dest2, sorted_y, meta, x2, npost, mod)


def _moe(h2, meta, counts, x2, wg, wu, wd, npost, mod, layer, S):
    R, D = x2.shape
    tm = MOE_TM
    n_tiles = (TOP_K * R + N_EXPERTS * (tm - 1)) // tm + 1
    i1, i2, r1, r2 = (meta[:, j].astype(I32) for j in range(4))
    cnt = counts[0, :N_EXPERTS].astype(I32)
    padded = (cnt + tm - 1) // tm * tm
    end = jnp.cumsum(padded)
    start = end - padded
    eye = jnp.arange(N_EXPERTS, dtype=I32)[None, :]
    dest1 = jnp.sum(jnp.where(i1[:, None] == eye, start[None, :], 0), axis=1) + r1
    dest2 = jnp.sum(jnp.where(i2[:, None] == eye, start[None, :], 0), axis=1) + r2
    tile_row0 = jnp.arange(n_tiles, dtype=I32) * tm
    tile_expert = jnp.minimum(jnp.sum(tile_row0[:, None] >= end[None, :], axis=1), N_EXPERTS - 1).astype(I32)
    n_used = (end[-1:] // tm).astype(I32)
    src = _invperm(jnp.concatenate([dest1, dest2]), n_tiles * tm)
    sorted_h = _gather_rows(src, n_used, h2, n_tiles, tm)
    sorted_y = _experts(tile_expert, n_used, sorted_h, wg, wu, wd, tm)
    return _combine(dest1, dest2, sorted_y, meta, x2, npost, mod, layer, S)


def kernel(x, c, w_ada, b_ada, norm_pre_mix, w_in, w_att_proj, w_ret_proj, w_mix_out, norm_post_mix, norm_pre_ffn, w_ffn_gate, w_ffn_up, w_ffn_down, w_router, b_router, w_exp_gate, w_exp_up, w_exp_down, norm_post_ffn):
    B, S, D = x.shape
    R = B * S
    mod = _ada(c, w_ada, b_ada)
    x2 = x.reshape(R, D)
    row = lambda v: v.reshape(1, D)
    for layer in range(DEPTH):
        main, *atts = _inproj(x2, row(norm_pre_mix[layer]), mod, layer, _prep_w_in(w_in[layer]), B, S)
        outs, lses = zip(*[_attn_group(atts[g], g) for g in range(len(ATT_GROUPS))])
        ret = _retention(main.reshape(B, S, MAIN_COLS), B, S)
        j = layer // 2
        router = None
        if layer % 2 == 1:
            wrt = jnp.zeros((D, LANE), F32).at[:, :N_EXPERTS].set(w_router[j])
            brt = jnp.zeros((1, LANE), F32).at[0, :N_EXPERTS].set(b_router[j])
            router = (wrt, brt)
        res = _postmix(outs, lses, ret, main, x2, w_att_proj[layer].astype(BF16), w_ret_proj[layer].astype(BF16),
                       w_mix_out[layer].astype(BF16), row(norm_post_mix[layer]), row(norm_pre_ffn[layer]),
                       mod, layer, S, router=router)
        if layer % 2 == 0:
            x2, h2 = res
            x2 = _ffn(h2, x2, w_ffn_gate[j].astype(BF16), w_ffn_up[j].astype(BF16), w_ffn_down[j].astype(BF16),
                      row(norm_post_ffn[layer]), mod, layer, S)
        else:
            x2, h2, meta, counts = res
            x2 = _moe(h2, meta, counts, x2, w_exp_gate[j].astype(BF16), w_exp_up[j].astype(BF16),
                      w_exp_down[j].astype(BF16), row(norm_post_ffn[layer]), mod, layer, S)
    return x2.reshape(B, S, D)
```

```python
import functools

import numpy as np
import jax
import jax.numpy as jnp
from jax import lax
from jax.experimental import pallas as pl
from jax.experimental.pallas import tpu as pltpu

F32 = jnp.float32
BF16 = jnp.bfloat16
I32 = jnp.int32

D_MODEL = 1024
DEPTH = 2
ATT_GROUPS = ((128, 1), (512, 4), (2048, 16))
HEADS_PER_GROUP = 4
N_ATT_HEADS = HEADS_PER_GROUP * len(ATT_GROUPS)
ATT_HEAD_DIM = 64
ATT_WIDTH = N_ATT_HEADS * ATT_HEAD_DIM
ATT_OUT_WIDTH = HEADS_PER_GROUP * ATT_HEAD_DIM
BLOCK = 128
ALIBI_MAX = 8.0
NEG_INF = -1e30
RET_HEADS = 8
RET_KEY_DIM = 64
RET_VAL_DIM = 128
RET_QK_WIDTH = RET_HEADS * RET_KEY_DIM
RET_V_WIDTH = RET_HEADS * RET_VAL_DIM
RET_CHUNK = 128
ROPE_BASE = 10000.0
D_FF = 2816
N_EXPERTS = 8
TOP_K = 2
EPS = 1e-6
IN_SIZES = (ATT_WIDTH, ATT_WIDTH, ATT_WIDTH, RET_QK_WIDTH, RET_QK_WIDTH, RET_V_WIDTH, RET_V_WIDTH, D_MODEL, D_MODEL)
IN_COLS = sum(IN_SIZES)

LANE = 128
MAIN_COLS = 4 * D_MODEL + 2 * RET_QK_WIDTH
GROUP_COLS = 3 * ATT_OUT_WIDTH
MXU_N = 256

MiB = 1024 * 1024


def _cparams(sem, vmem_mib):
    return pltpu.CompilerParams(dimension_semantics=sem, vmem_limit_bytes=vmem_mib * MiB)


def _rms(x, gain):
    return x * lax.rsqrt(jnp.mean(x * x, axis=-1, keepdims=True) + EPS) * gain


def _silu(x):
    return x * jax.nn.sigmoid(x)


def _ada_kernel(c_ref, w_ref, b_ref, o_ref):
    ca = _silu(c_ref[...])
    o_ref[0] = jnp.dot(ca, w_ref[0], preferred_element_type=F32, precision=lax.Precision.HIGHEST) + b_ref[0]


def _ada(c, w_ada, b_ada):
    B, D = c.shape
    out = pl.pallas_call(
        _ada_kernel,
        out_shape=jax.ShapeDtypeStruct((DEPTH * 6, B, D), F32),
        grid=(DEPTH, 6),
        in_specs=[
            pl.BlockSpec((B, D), lambda l, k: (0, 0)),
            pl.BlockSpec((1, D, D), lambda l, k: (l, 0, k)),
            pl.BlockSpec((1, 1, D), lambda l, k: (l, 0, k)),
        ],
        out_specs=pl.BlockSpec((1, B, D), lambda l, k: (l * 6 + k, 0, 0)),
        compiler_params=_cparams(("arbitrary", "arbitrary"), 32),
        name="ada",
    )(c, w_ada, b_ada.reshape(DEPTH, 1, 6 * D))
    return out.reshape(DEPTH, 6, B, 1, D)


def _mod_spec(layer, k, tiles_per_batch):
    return pl.BlockSpec((None, None, None, 1, D_MODEL),
                        lambda i, *_: (layer, k, i // tiles_per_batch, 0, 0))


def _inproj_kernel(x_ref, gain_ref, sc_ref, sh_ref, w_ref, main_ref, a0_ref, a1_ref, a2_ref, scr_ref):
    tm = x_ref.shape[0]
    h = _rms(x_ref[...], gain_ref[...]) * (1.0 + sc_ref[...]) + sh_ref[...]
    hb = h.astype(BF16)
    for c in range(MAIN_COLS // MXU_N):
        sl = slice(c * MXU_N, (c + 1) * MXU_N)
        main_ref[:, sl] = jnp.dot(hb, w_ref[:, sl], preferred_element_type=F32).astype(BF16)
    att_refs = (a0_ref, a1_ref, a2_ref)
    slot = 0
    for g, (_, d) in enumerate(ATT_GROUPS):
        for seg in range(3):
            col = MAIN_COLS + g * GROUP_COLS + seg * ATT_OUT_WIDTH
            res = jnp.dot(hb, w_ref[:, col:col + ATT_OUT_WIDTH], preferred_element_type=F32)
            if d == 1:
                att_refs[g][0, :, seg * ATT_OUT_WIDTH:(seg + 1) * ATT_OUT_WIDTH] = res.astype(BF16)
                continue
            for s in range(ATT_OUT_WIDTH // LANE):
                scr_ref[slot + s] = res[:, s * LANE:(s + 1) * LANE]
            for r in range(d):
                for s in range(ATT_OUT_WIDTH // LANE):
                    c0 = seg * ATT_OUT_WIDTH + s * LANE
                    att_refs[g][r, :, c0:c0 + LANE] = scr_ref[slot + s, pl.ds(r, tm // d, stride=d), :].astype(BF16)
            slot += ATT_OUT_WIDTH // LANE


def _inproj(x2, gain, mod, layer, w_in_b, B, S, tm=512):
    R, D = x2.shape
    tpb = S // tm
    n_slabs = sum(3 * (ATT_OUT_WIDTH // LANE) for _, d in ATT_GROUPS if d > 1)

    def att_shape(d):
        return jax.ShapeDtypeStruct((B, d, S // d, GROUP_COLS), BF16)

    def att_spec(d):
        return pl.BlockSpec((None, d, tm // d, GROUP_COLS), lambda i: (i // tpb, 0, i % tpb, 0))

    return pl.pallas_call(
        _inproj_kernel,
        out_shape=(jax.ShapeDtypeStruct((R, MAIN_COLS), BF16),) + tuple(att_shape(d) for _, d in ATT_GROUPS),
        grid=(R // tm,),
        in_specs=[
            pl.BlockSpec((tm, D), lambda i: (i, 0)),
            pl.BlockSpec((1, D), lambda i: (0, 0)),
            _mod_spec(layer, 1, tpb),
            _mod_spec(layer, 0, tpb),
            pl.BlockSpec((D, IN_COLS), lambda i: (0, 0), pipeline_mode=pl.Buffered(1)),
        ],
        out_specs=(pl.BlockSpec((tm, MAIN_COLS), lambda i: (i, 0)),) + tuple(att_spec(d) for _, d in ATT_GROUPS),
        scratch_shapes=[pltpu.VMEM((n_slabs, tm, LANE), F32)],
        compiler_params=_cparams(("arbitrary",), 48),
        name="inproj",
    )(x2, gain, mod, mod, w_in_b)


def _prep_w_in(w):
    b = np.concatenate([[0], np.cumsum(IN_SIZES)])
    seg = lambda s: w[:, b[s]:b[s + 1]]
    cols = [seg(5), seg(6), seg(7), seg(8), seg(3), seg(4)]
    q_a, k_a, v_a = seg(0) * (ATT_HEAD_DIM ** -0.5), seg(1), seg(2)
    for g in range(len(ATT_GROUPS)):
        gs = slice(g * ATT_OUT_WIDTH, (g + 1) * ATT_OUT_WIDTH)
        cols += [q_a[:, gs], k_a[:, gs], v_a[:, gs]]
    return jnp.concatenate(cols, axis=1).astype(BF16)


def _alibi_slopes():
    return [2.0 ** (-ALIBI_MAX * h / N_ATT_HEADS) for h in range(1, N_ATT_HEADS + 1)]


def _attn_kernel(a_ref, bias_ref, o_ref, l_ref, *, n, dilation):
    nb = n // BLOCK
    win = min(2 * BLOCK, n)
    W = ATT_OUT_WIDTH

    def unit(u):
        r = u // nb
        i = u % nb
        q0 = pl.multiple_of(i * BLOCK, BLOCK)
        k0 = pl.multiple_of(jnp.clip((i - 1) * BLOCK, 0, n - win), BLOCK)
        t = jnp.minimum(i, 1)
        q = a_ref[r, pl.ds(q0, BLOCK), 0:W]
        k = a_ref[r, pl.ds(k0, win), W:2 * W]
        v = a_ref[r, pl.ds(k0, win), 2 * W:3 * W]
        for h in range(HEADS_PER_GROUP):
            hs = slice(h * ATT_HEAD_DIM, (h + 1) * ATT_HEAD_DIM)
            s = lax.dot_general(q[:, hs], k[:, hs], (((1,), (1,)), ((), ())), preferred_element_type=F32)
            s = s + bias_ref[t, h]
            m = jnp.max(s, axis=-1, keepdims=True)
            p = jnp.exp(s - m)
            den = jnp.sum(p, axis=-1, keepdims=True)
            o = jnp.dot(p.astype(BF16), v[:, hs], preferred_element_type=F32) / den
            o_ref[r, pl.ds(q0, BLOCK), hs] = o.astype(o_ref.dtype)
            l_ref[r, pl.ds(q0, BLOCK), hs] = jnp.broadcast_to(m + jnp.log(den), (BLOCK, ATT_HEAD_DIM))

    def pair(u2, carry):
        unit(2 * u2)
        unit(2 * u2 + 1)
        return carry

    lax.fori_loop(0, dilation * nb // 2, pair, 0)


def _attn_bias(n, n_w, d, slopes):
    win = min(2 * BLOCK, n)
    r = np.arange(BLOCK)[:, None]
    c = np.arange(win)[None, :]
    tables = []
    for shift in (0, win - BLOCK):
        dist = shift + r - c
        valid = (dist >= 0) & (dist <= n_w)
        tables.append(np.stack([np.where(valid, -s * (d * dist), NEG_INF) for s in slopes]))
    return jnp.asarray(np.stack(tables), F32)


def _attn_group(att, g):
    window, d = ATT_GROUPS[g]
    B, _, n, _ = att.shape
    assert n % BLOCK == 0 and (d * (n // BLOCK)) % 2 == 0
    slopes = _alibi_slopes()[g * HEADS_PER_GROUP:(g + 1) * HEADS_PER_GROUP]
    bias = _attn_bias(n, window // d, d, slopes)
    kern = functools.partial(_attn_kernel, n=n, dilation=d)
    out_spec = pl.BlockSpec((None, d, n, ATT_OUT_WIDTH), lambda b: (b, 0, 0, 0))
    return pl.pallas_call(
        kern,
        out_shape=(jax.ShapeDtypeStruct((B, d, n, ATT_OUT_WIDTH), BF16),
                   jax.ShapeDtypeStruct((B, d, n, ATT_OUT_WIDTH), F32)),
        grid=(B,),
        in_specs=[pl.BlockSpec((None, d, n, GROUP_COLS), lambda b: (b, 0, 0, 0)),
                  pl.BlockSpec(bias.shape, lambda b: (0, 0, 0, 0))],
        out_specs=(out_spec, out_spec),
        compiler_params=_cparams(("arbitrary",), 40),
        name=f"attn{g}",
    )(att, bias)


def _ret_tables(S):
    half = RET_KEY_DIM // 2
    inv_freq = 1.0 / (ROPE_BASE ** (jnp.arange(half, dtype=F32) / half))
    ang = jnp.arange(S).astype(F32)[:, None] * inv_freq[None]
    cos = jnp.cos(ang)
    sin = jnp.sin(ang)
    cos_t = jnp.tile(jnp.concatenate([cos, cos], axis=-1), (1, RET_HEADS))
    sin_t = jnp.tile(jnp.concatenate([-sin, sin], axis=-1), (1, RET_HEADS))
    log_gamma = jnp.log1p(-jnp.exp2(-5.0 - jnp.arange(RET_HEADS, dtype=F32)))
    idx = jnp.arange(RET_CHUNK, dtype=F32)
    rel = idx[:, None] - idx[None, :]
    decay = jnp.where(rel[None] >= 0, jnp.exp(jnp.maximum(rel, 0.0)[None] * log_gamma[:, None, None]), 0.0)
    zeta = jnp.exp((RET_CHUNK - 1 - idx)[None] * log_gamma[:, None])
    xi = jnp.exp((idx + 1)[None] * log_gamma[:, None])
    zeta_t = jnp.repeat(zeta.T, RET_KEY_DIM, axis=1) * (RET_KEY_DIM ** -0.5)
    xi_t = jnp.repeat(xi.T, RET_KEY_DIM, axis=1)
    g_chunk = jnp.exp(RET_CHUNK * log_gamma)
    g_t = jnp.broadcast_to(g_chunk[:, None, None], (RET_HEADS, 1, RET_VAL_DIM))
    return cos_t, sin_t, decay, zeta_t, xi_t, g_t


def _rot_half_swap(t):
    w = t.shape[-1]
    lane = lax.broadcasted_iota(I32, t.shape, 1) % RET_KEY_DIM
    half = RET_KEY_DIM // 2
    return jnp.where(lane < half, pltpu.roll(t, w - half, 1), pltpu.roll(t, half, 1))


RET_ROWS = 1024
RET_PAIR = 2


def _retn_kernel(q_ref, k_ref, v_ref, g_ref, cos_ref, sin_ref, decay_ref, zeta_ref, xi_ref, gch_ref,
                 o_ref, state_ref):
    C = RET_CHUNK

    @pl.when(pl.program_id(1) == 0)
    def _():
        state_ref[...] = jnp.zeros_like(state_ref)

    def prep(c):
        rows = pl.ds(pl.multiple_of(c * C, C), C)
        cos = cos_ref[rows, :]
        sin = sin_ref[rows, :]
        q = q_ref[rows, :].astype(F32)
        k = k_ref[rows, :].astype(F32)
        q = q * cos + _rot_half_swap(q) * sin
        k = k * cos + _rot_half_swap(k) * sin
        qb = q.astype(BF16)
        kb = (k * (RET_KEY_DIM ** -0.5)).astype(BF16)
        q_xi = (q * xi_ref[...]).astype(BF16)
        kz_t = (k * zeta_ref[...]).T
        return rows, qb, kb, q_xi, kz_t

    def group(u, carry):
        chunks = [prep(u * RET_PAIR + i) for i in range(RET_PAIR)]
        for h in range(RET_HEADS):
            ks = slice(h * RET_KEY_DIM, (h + 1) * RET_KEY_DIM)
            vs = slice(h * RET_VAL_DIM, (h + 1) * RET_VAL_DIM)
            st = state_ref[h]
            for rows, qb, kb, q_xi, kz_t in chunks:
                v = v_ref[rows, vs]
                s = lax.dot_general(qb[:, ks], kb[:, ks], (((1,), (1,)), ((), ())), preferred_element_type=F32)
                s = (s * decay_ref[h]).astype(BF16)
                o = jnp.dot(s, v, preferred_element_type=F32)
                o = o + jnp.dot(q_xi[:, ks], st.astype(BF16), preferred_element_type=F32)
                st = st * gch_ref[h] + jnp.dot(kz_t[ks, :].astype(BF16), v, preferred_element_type=F32)
                o = o * lax.rsqrt(jnp.mean(o * o, axis=-1, keepdims=True) + EPS)
                o_ref[rows, vs] = (_silu(g_ref[rows, vs].astype(F32)) * o).astype(o_ref.dtype)
            state_ref[h] = st
        return carry

    lax.fori_loop(0, q_ref.shape[0] // (C * RET_PAIR), group, 0)


def _retention(main3, B, S):
    C = RET_CHUNK
    RS = RET_ROWS
    assert S % RS == 0 and RS % (C * RET_PAIR) == 0
    cos_t, sin_t, decay, zeta_t, xi_t, g_t = _ret_tables(S)
    const2 = lambda b, c: (0, 0)
    const3 = lambda b, c: (0, 0, 0)
    qk0 = 4 * D_MODEL // RET_QK_WIDTH
    out = pl.pallas_call(
        _retn_kernel,
        out_shape=jax.ShapeDtypeStruct((B, S, RET_V_WIDTH), BF16),
        grid=(B, S // RS),
        in_specs=[
            pl.BlockSpec((None, RS, RET_QK_WIDTH), lambda b, c: (b, c, qk0)),
            pl.BlockSpec((None, RS, RET_QK_WIDTH), lambda b, c: (b, c, qk0 + 1)),
            pl.BlockSpec((None, RS, RET_V_WIDTH), lambda b, c: (b, c, 0)),
            pl.BlockSpec((None, RS, RET_V_WIDTH), lambda b, c: (b, c, 1)),
            pl.BlockSpec((RS, RET_QK_WIDTH), lambda b, c: (c, 0)),
            pl.BlockSpec((RS, RET_QK_WIDTH), lambda b, c: (c, 0)),
            pl.BlockSpec((RET_HEADS, C, C), const3),
            pl.BlockSpec((C, RET_QK_WIDTH), const2),
            pl.BlockSpec((C, RET_QK_WIDTH), const2),
            pl.BlockSpec((RET_HEADS, 1, RET_VAL_DIM), const3),
        ],
        out_specs=pl.BlockSpec((None, RS, RET_V_WIDTH), lambda b, c: (b, c, 0)),
        scratch_shapes=[pltpu.VMEM((RET_HEADS, RET_KEY_DIM, RET_VAL_DIM), F32)],
        compiler_params=_cparams(("arbitrary", "arbitrary"), 48),
        name="retn",
    )(main3, main3, main3, main3, cos_t, sin_t, decay, zeta_t, xi_t, g_t)
    return out.reshape(B * S, RET_V_WIDTH)


N_META = 6


def _unstride(ref, scr_ref, slot, d, tm):
    if d == 1:
        return ref[0].astype(F32)
    for r in range(d):
        val = ref[r].astype(F32)
        for s in range(ATT_OUT_WIDTH // LANE):
            scr_ref[slot + s, pl.ds(r, tm // d, stride=d), :] = val[:, s * LANE:(s + 1) * LANE]
    return jnp.concatenate([scr_ref[slot + s] for s in range(ATT_OUT_WIDTH // LANE)], axis=1)


def _postmix_kernel(o0_ref, o1_ref, o2_ref, l0_ref, l1_ref, l2_ref, ret_ref, ga_ref, gr_ref, x_ref,
                    wa_ref, wr_ref, wo_ref, npost_ref, g1_ref, npre_ref, sc2_ref, sh2_ref, *rest, with_router):
    if with_router:
        wrt_ref, brt_ref, xo_ref, h_ref, meta_ref, cnt_ref, scr_ref, carry_ref = rest
    else:
        xo_ref, h_ref, scr_ref = rest
    tm = x_ref.shape[0]
    per = ATT_OUT_WIDTH // LANE
    os_, ls_ = [], []
    slot = 0
    for (_, d), o_ref, l_ref in zip(ATT_GROUPS, (o0_ref, o1_ref, o2_ref), (l0_ref, l1_ref, l2_ref)):
        os_.append(_unstride(o_ref, scr_ref, slot, d, tm))
        ls_.append(_unstride(l_ref, scr_ref, slot + per, d, tm))
        if d > 1:
            slot += 2 * per
    l0, l1, l2 = ls_
    m = jnp.maximum(jnp.maximum(l0, l1), l2)
    e0, e1, e2 = jnp.exp(l0 - m), jnp.exp(l1 - m), jnp.exp(l2 - m)
    att = (e0 * os_[0] + e1 * os_[1] + e2 * os_[2]) / (e0 + e1 + e2)
    a = jnp.dot(att.astype(BF16), wa_ref[...], preferred_element_type=F32)
    r = jnp.dot(ret_ref[...], wr_ref[...], preferred_element_type=F32)
    merged = jax.nn.sigmoid(ga_ref[...].astype(F32)) * a + jax.nn.sigmoid(gr_ref[...].astype(F32)) * r
    y = jnp.dot(merged.astype(BF16), wo_ref[...], preferred_element_type=F32)
    x = x_ref[...] + g1_ref[...] * _rms(y, npost_ref[...])
    xo_ref[...] = x
    h = _rms(x, npre_ref[...]) * (1.0 + sc2_ref[...]) + sh2_ref[...]
    h_ref[...] = h.astype(h_ref.dtype)
    if not with_router:
        return

    @pl.when(pl.program_id(0) == 0)
    def _():
        carry_ref[...] = jnp.zeros_like(carry_ref)

    h_hi = h.astype(BF16)
    h_lo = (h - h_hi.astype(F32)).astype(BF16)
    w = wrt_ref[...]
    w_hi = w.astype(BF16)
    w_lo = (w - w_hi.astype(F32)).astype(BF16)
    lg = (jnp.dot(h_hi, w_hi, preferred_element_type=F32) + jnp.dot(h_hi, w_lo, preferred_element_type=F32)
          + jnp.dot(h_lo, w_hi, preferred_element_type=F32)) + brt_ref[...]
    lane = lax.broadcasted_iota(I32, lg.shape, 1).astype(F32)
    lg = jnp.where(lane < N_EXPERTS, lg, -jnp.inf)
    m1 = jnp.max(lg, axis=-1, keepdims=True)
    i1 = jnp.min(jnp.where(lg == m1, lane, float(LANE)), axis=-1, keepdims=True)
    lg2 = jnp.where(lane == i1, -jnp.inf, lg)
    m2 = jnp.max(lg2, axis=-1, keepdims=True)
    i2 = jnp.min(jnp.where(lg2 == m2, lane, float(LANE)), axis=-1, keepdims=True)
    t = jnp.exp(m2 - m1)
    w1 = 1.0 / (1.0 + t)
    w2 = t / (1.0 + t)
    oh = jnp.where((lane == i1) | (lane == i2), 1.0, 0.0)
    row = lax.broadcasted_iota(I32, (tm, tm), 0)
    col = lax.broadcasted_iota(I32, (tm, tm), 1)
    tri = jnp.where(col < row, 1.0, 0.0).astype(BF16)
    pos = jnp.dot(tri, oh.astype(BF16), preferred_element_type=F32) + carry_ref[...]
    r1 = jnp.sum(jnp.where(lane == i1, pos, 0.0), axis=-1, keepdims=True)
    r2 = jnp.sum(jnp.where(lane == i2, pos, 0.0), axis=-1, keepdims=True)
    carry = carry_ref[...] + jnp.sum(oh, axis=0, keepdims=True)
    carry_ref[...] = carry
    cnt_ref[...] = jnp.broadcast_to(carry, cnt_ref.shape)
    meta = jnp.zeros(lg.shape, F32)
    for j, val in enumerate((i1, i2, r1, r2, w1, w2)):
        meta = jnp.where(lane == j, val, meta)
    meta_ref[...] = meta


def _postmix(outs, lses, ret, main, x2, wa, wr, wo, npost, npre, mod, layer, S, router=None, tm=512):
    R, D = x2.shape
    tpb = S // tm
    row = lambda w: pl.BlockSpec((tm, w), lambda i: (i, 0))
    const = lambda a: pl.BlockSpec(a.shape, lambda i: (0, 0))

    def att_spec(d):
        return pl.BlockSpec((None, d, tm // d, ATT_OUT_WIDTH), lambda i: (i // tpb, 0, i % tpb, 0))

    specs_att = [att_spec(d) for _, d in ATT_GROUPS]
    in_specs = (specs_att + specs_att + [
        row(RET_V_WIDTH),
        pl.BlockSpec((tm, D), lambda i: (i, 2)),
        pl.BlockSpec((tm, D), lambda i: (i, 3)),
        row(D), const(wa), const(wr), const(wo), const(npost),
        _mod_spec(layer, 2, tpb), const(npre), _mod_spec(layer, 4, tpb), _mod_spec(layer, 3, tpb)])
    args = list(outs) + list(lses) + [ret, main, main, x2, wa, wr, wo, npost, mod, npre, mod, mod]
    n_slabs = sum(2 * (ATT_OUT_WIDTH // LANE) for _, d in ATT_GROUPS if d > 1)
    scratch = [pltpu.VMEM((n_slabs, tm, LANE), F32)]
    if router is None:
        out_shape = [jax.ShapeDtypeStruct((R, D), F32), jax.ShapeDtypeStruct((R, D), BF16)]
        out_specs = [row(D), row(D)]
        sem = ("arbitrary",)
    else:
        in_specs += [const(router[0]), const(router[1])]
        args += list(router)
        out_shape = [jax.ShapeDtypeStruct((R, D), F32), jax.ShapeDtypeStruct((R, D), F32),
                     jax.ShapeDtypeStruct((R, LANE), F32), jax.ShapeDtypeStruct((8, LANE), F32)]
        out_specs = [row(D), row(D), row(LANE), pl.BlockSpec((8, LANE), lambda i: (0, 0))]
        scratch.append(pltpu.VMEM((1, LANE), F32))
        sem = ("arbitrary",)
    return pl.pallas_call(
        functools.partial(_postmix_kernel, with_router=router is not None),
        out_shape=tuple(out_shape),
        grid=(R // tm,),
        in_specs=in_specs,
        out_specs=tuple(out_specs),
        scratch_shapes=scratch,
        compiler_params=_cparams(sem, 48),
        name="postmix",
    )(*args)


FF_TILE = D_FF // 2


def _swiglu_partial(h, wg_ref, wu_ref, wd_ref):
    act = _silu(jnp.dot(h, wg_ref[...], preferred_element_type=F32)) * jnp.dot(h, wu_ref[...], preferred_element_type=F32)
    return jnp.dot(act.astype(BF16), wd_ref[...], preferred_element_type=F32)


def _ffn_kernel(h_ref, x_ref, wg_ref, wu_ref, wd_ref, npost_ref, g2_ref, o_ref, acc_ref):
    j = pl.program_id(1)

    @pl.when(j == 0)
    def _():
        acc_ref[...] = jnp.zeros_like(acc_ref)

    acc_ref[...] += _swiglu_partial(h_ref[...], wg_ref, wu_ref, wd_ref)

    @pl.when(j == pl.num_programs(1) - 1)
    def _():
        o_ref[...] = x_ref[...] + g2_ref[...] * _rms(acc_ref[...], npost_ref[...])


def _ffn(h2, x2, wg, wu, wd, npost, mod, layer, S, tm=512):
    R, D = x2.shape
    tpb = S // tm
    nf = D_FF // FF_TILE
    return pl.pallas_call(
        _ffn_kernel,
        out_shape=jax.ShapeDtypeStruct((R, D), F32),
        grid=(R // tm, nf),
        in_specs=[
            pl.BlockSpec((tm, D), lambda i, j: (i, 0)),
            pl.BlockSpec((tm, D), lambda i, j: (i, 0)),
            pl.BlockSpec((D, FF_TILE), lambda i, j: (0, j)),
            pl.BlockSpec((D, FF_TILE), lambda i, j: (0, j)),
            pl.BlockSpec((FF_TILE, D), lambda i, j: (j, 0)),
            pl.BlockSpec((1, D), lambda i, j: (0, 0)),
            _mod_spec(layer, 5, tpb),
        ],
        out_specs=pl.BlockSpec((tm, D), lambda i, j: (i, 0)),
        scratch_shapes=[pltpu.VMEM((tm, D), F32)],
        compiler_params=_cparams(("arbitrary", "arbitrary"), 48),
        name="ffn",
    )(h2, x2, wg, wu, wd, npost, mod)


MOE_TM = 512
ROW_UNROLL = 8


def _invperm_kernel(n_ref, dest_ref, src_ref):
    def zero(g, c):
        for u in range(ROW_UNROLL):
            src_ref[g * ROW_UNROLL + u] = 0
        return c

    lax.fori_loop(0, n_ref[0], zero, 0)

    def put(g, c):
        for u in range(ROW_UNROLL):
            a = g * ROW_UNROLL + u
            src_ref[dest_ref[a]] = a
        return c

    lax.fori_loop(0, n_ref[1], put, 0)


def _invperm(dest, P):
    assert P % ROW_UNROLL == 0 and dest.shape[0] % ROW_UNROLL == 0
    trips = jnp.array([P // ROW_UNROLL, dest.shape[0] // ROW_UNROLL], I32)
    return pl.pallas_call(
        _invperm_kernel,
        out_shape=jax.ShapeDtypeStruct((P,), I32),
        in_specs=[pl.BlockSpec(memory_space=pltpu.SMEM), pl.BlockSpec(memory_space=pltpu.SMEM)],
        out_specs=pl.BlockSpec(memory_space=pltpu.SMEM),
        name="invperm",
    )(trips, dest)


def _expert_kernel(te_ref, nu_ref, src_ref, h_hbm, wg_ref, wu_ref, wd_ref, o_ref, acc_ref, hb_ref, hbuf_ref, sem,
                   *, n_tok):
    j = pl.program_id(0)
    f = pl.program_id(1)
    nu = nu_ref[0]
    tm = o_ref.shape[0]

    def row_copy(row, slot, i):
        return pltpu.make_async_copy(h_hbm.at[pl.ds(row, 1), :], hbuf_ref.at[slot, pl.ds(i, 1), :], sem.at[slot])

    def issue(tile, slot):
        def body(i, c):
            row_copy(src_ref[tile * tm + i] & (n_tok - 1), slot, i).start()
            return c

        lax.fori_loop(0, tm, body, 0, unroll=ROW_UNROLL)

    def wait(slot):
        def body(i, c):
            row_copy(0, slot, 0).wait()
            return c

        lax.fori_loop(0, tm, body, 0, unroll=ROW_UNROLL)

    @pl.when(j < nu)
    def _():
        @pl.when(f == 0)
        def _():
            slot = j % 2

            @pl.when(j == 0)
            def _():
                issue(0, 0)

            wait(slot)

            @pl.when(j + 1 < nu)
            def _():
                issue(j + 1, 1 - slot)

            acc_ref[...] = jnp.zeros_like(acc_ref)
            hb_ref[...] = hbuf_ref[slot].astype(BF16)

        acc_ref[...] += _swiglu_partial(hb_ref[...], wg_ref, wu_ref, wd_ref)

        @pl.when(f == pl.num_programs(1) - 1)
        def _():
            o_ref[...] = acc_ref[...]

    @pl.when((j >= nu) & (f == pl.num_programs(1) - 1))
    def _():
        o_ref[...] = jnp.zeros_like(o_ref)


def _experts(tile_expert, n_used, src, h2, wg, wu, wd, n_tiles, tm):
    R, D = h2.shape
    assert R & (R - 1) == 0
    nf = D_FF // FF_TILE
    tile = lambda j, nu: jnp.minimum(j, nu[0] - 1)
    ff = lambda j, f, nu: jnp.where(j < nu[0], f, nf - 1)
    return pl.pallas_call(
        functools.partial(_expert_kernel, n_tok=R),
        out_shape=jax.ShapeDtypeStruct((n_tiles * tm, D), F32),
        grid_spec=pltpu.PrefetchScalarGridSpec(
            num_scalar_prefetch=3,
            grid=(n_tiles, nf),
            in_specs=[
                pl.BlockSpec(memory_space=pl.ANY),
                pl.BlockSpec((None, D, FF_TILE), lambda j, f, te, nu, src: (te[tile(j, nu)], 0, ff(j, f, nu))),
                pl.BlockSpec((None, D, FF_TILE), lambda j, f, te, nu, src: (te[tile(j, nu)], 0, ff(j, f, nu))),
                pl.BlockSpec((None, FF_TILE, D), lambda j, f, te, nu, src: (te[tile(j, nu)], ff(j, f, nu), 0)),
            ],
            out_specs=pl.BlockSpec((tm, D), lambda j, f, te, nu, src: (j, 0)),
            scratch_shapes=[pltpu.VMEM((tm, D), F32), pltpu.VMEM((tm, D), BF16), pltpu.VMEM((2, tm, D), F32),
                            pltpu.SemaphoreType.DMA((2,))],
        ),
        compiler_params=_cparams(("arbitrary", "arbitrary"), 48),
        name="moe_experts",
    )(tile_expert, n_used, src, h2, wg, wu, wd)


def _combine_kernel(d1_ref, d2_ref, y_hbm, meta_ref, x_ref, npost_ref, g2_ref, o_ref, ya_ref, yb_ref, sem):
    tm = x_ref.shape[0]
    t0 = pl.program_id(0) * tm

    def row_copy(row, dst_ref, r):
        return pltpu.make_async_copy(y_hbm.at[pl.ds(row, 1), :], dst_ref.at[pl.ds(r, 1), :], sem)

    def issue(r, c):
        row_copy(d1_ref[t0 + r], ya_ref, r).start()
        row_copy(d2_ref[t0 + r], yb_ref, r).start()
        return c

    lax.fori_loop(0, tm, issue, 0, unroll=ROW_UNROLL)

    def wait(r, c):
        row_copy(0, ya_ref, 0).wait()
        row_copy(0, yb_ref, 0).wait()
        return c

    lax.fori_loop(0, tm, wait, 0, unroll=ROW_UNROLL)
    meta = meta_ref[...]
    y = meta[:, 4:5] * ya_ref[...] + meta[:, 5:6] * yb_ref[...]
    o_ref[...] = x_ref[...] + g2_ref[...] * _rms(y, npost_ref[...])


def _combine(dest1, dest2, sorted_y, meta, x2, npost, mod, layer, S, tm=512):
    R, D = x2.shape
    tpb = S // tm
    return pl.pallas_call(
        _combine_kernel,
        out_shape=jax.ShapeDtypeStruct((R, D), F32),
        grid_spec=pltpu.PrefetchScalarGridSpec(
            num_scalar_prefetch=2,
            grid=(R // tm,),
            in_specs=[
                pl.BlockSpec(memory_space=pl.ANY),
                pl.BlockSpec((tm, LANE), lambda i, *_: (i, 0)),
                pl.BlockSpec((tm, D), lambda i, *_: (i, 0)),
                pl.BlockSpec((1, D), lambda i, *_: (0, 0)),
                _mod_spec(layer, 5, tpb),
            ],
            out_specs=pl.BlockSpec((tm, D), lambda i, *_: (i, 0)),
            scratch_shapes=[pltpu.VMEM((tm, D), F32), pltpu.VMEM((tm, D), F32), pltpu.SemaphoreType.DMA],
        ),
        compiler_params=_cparams(("arbitrary",), 40),
        name="moe_combine",
    )(dest1, dest2, sorted_y, meta, x2, npost, mod)


def _moe(h2, meta, counts, x2, wg, wu, wd, npost, mod, layer, S):
    R, D = x2.shape
    tm = MOE_TM
    n_tiles = (TOP_K * R + N_EXPERTS * (tm - 1)) // tm + 1
    i1, i2, r1, r2 = (meta[:, j].astype(I32) for j in range(4))
    cnt = counts[0, :N_EXPERTS].astype(I32)
    padded = (cnt + tm - 1) // tm * tm
    end = jnp.cumsum(padded)
    start = end - padded
    eye = jnp.arange(N_EXPERTS, dtype=I32)[None, :]
    dest1 = jnp.sum(jnp.where(i1[:, None] == eye, start[None, :], 0), axis=1) + r1
    dest2 = jnp.sum(jnp.where(i2[:, None] == eye, start[None, :], 0), axis=1) + r2
    tile_row0 = jnp.arange(n_tiles, dtype=I32) * tm
    tile_expert = jnp.minimum(jnp.sum(tile_row0[:, None] >= end[None, :], axis=1), N_EXPERTS - 1).astype(I32)
    n_used = (end[-1:] // tm).astype(I32)
    src = _invperm(jnp.concatenate([dest1, dest2]), n_tiles * tm)
    sorted_y = _experts(tile_expert, n_used, src, h2, wg, wu, wd, n_tiles, tm)
    return _combine(dest1, dest2, sorted_y, meta, x2, npost, mod, layer, S)


def kernel(x, c, w_ada, b_ada, norm_pre_mix, w_in, w_att_proj, w_ret_proj, w_mix_out, norm_post_mix, norm_pre_ffn, w_ffn_gate, w_ffn_up, w_ffn_down, w_router, b_router, w_exp_gate, w_exp_up, w_exp_down, norm_post_ffn):
    B, S, D = x.shape
    R = B * S
    mod = _ada(c, w_ada, b_ada)
    x2 = x.reshape(R, D)
    row = lambda v: v.reshape(1, D)
    for layer in range(DEPTH):
        main, *atts = _inproj(x2, row(norm_pre_mix[layer]), mod, layer, _prep_w_in(w_in[layer]), B, S)
        outs, lses = zip(*[_attn_group(atts[g], g) for g in range(len(ATT_GROUPS))])
        ret = _retention(main.reshape(B, S, MAIN_COLS), B, S)
        j = layer // 2
        router = None
        if layer % 2 == 1:
            wrt = jnp.zeros((D, LANE), F32).at[:, :N_EXPERTS].set(w_router[j])
            brt = jnp.zeros((1, LANE), F32).at[0, :N_EXPERTS].set(b_router[j])
            router = (wrt, brt)
        res = _postmix(outs, lses, ret, main, x2, w_att_proj[layer].astype(BF16), w_ret_proj[layer].astype(BF16),
                       w_mix_out[layer].astype(BF16), row(norm_post_mix[layer]), row(norm_pre_ffn[layer]),
                       mod, layer, S, router=router)
        if layer % 2 == 0:
            x2, h2 = res
            x2 = _ffn(h2, x2, w_ffn_gate[j].astype(BF16), w_ffn_up[j].astype(BF16), w_ffn_down[j].astype(BF16),
                      row(norm_post_ffn[layer]), mod, layer, S)
        else:
            x2, h2, meta, counts = res
            x2 = _moe(h2, meta, counts, x2, w_exp_gate[j].astype(BF16), w_exp_up[j].astype(BF16),
                      w_exp_down[j].astype(BF16), row(norm_post_ffn[layer]), mod, layer, S)
    return x2.reshape(B, S, D)
```

```python
import functools

import numpy as np
import jax
import jax.numpy as jnp
from jax import lax
from jax.experimental import pallas as pl
from jax.experimental.pallas import tpu as pltpu

F32 = jnp.float32
BF16 = jnp.bfloat16
I32 = jnp.int32

D_MODEL = 1024
DEPTH = 2
ATT_GROUPS = ((128, 1), (512, 4), (2048, 16))
HEADS_PER_GROUP = 4
N_ATT_HEADS = HEADS_PER_GROUP * len(ATT_GROUPS)
ATT_HEAD_DIM = 64
ATT_WIDTH = N_ATT_HEADS * ATT_HEAD_DIM
ATT_OUT_WIDTH = HEADS_PER_GROUP * ATT_HEAD_DIM
BLOCK = 128
ALIBI_MAX = 8.0
NEG_INF = -1e30
RET_HEADS = 8
RET_KEY_DIM = 64
RET_VAL_DIM = 128
RET_QK_WIDTH = RET_HEADS * RET_KEY_DIM
RET_V_WIDTH = RET_HEADS * RET_VAL_DIM
RET_CHUNK = 128
ROPE_BASE = 10000.0
D_FF = 2816
N_EXPERTS = 8
TOP_K = 2
EPS = 1e-6
IN_SIZES = (ATT_WIDTH, ATT_WIDTH, ATT_WIDTH, RET_QK_WIDTH, RET_QK_WIDTH, RET_V_WIDTH, RET_V_WIDTH, D_MODEL, D_MODEL)
IN_COLS = sum(IN_SIZES)

LANE = 128
MAIN_COLS = 4 * D_MODEL + 2 * RET_QK_WIDTH
GROUP_SRC_COLS = 3 * ATT_OUT_WIDTH
GROUP_SLOTS = 3 * HEADS_PER_GROUP
GROUP_COLS = GROUP_SLOTS * LANE
ATT_T_WIDTH = HEADS_PER_GROUP * LANE
MXU_N = 256

MiB = 1024 * 1024


def _cparams(sem, vmem_mib):
    return pltpu.CompilerParams(dimension_semantics=sem, vmem_limit_bytes=vmem_mib * MiB)


def _rms(x, gain):
    return x * lax.rsqrt(jnp.mean(x * x, axis=-1, keepdims=True) + EPS) * gain


def _silu(x):
    return x * jax.nn.sigmoid(x)


def _ada_kernel(c_ref, w_ref, b_ref, o_ref):
    ca = _silu(c_ref[...])
    o_ref[0] = jnp.dot(ca, w_ref[0], preferred_element_type=F32, precision=lax.Precision.HIGHEST) + b_ref[0]


def _ada(c, w_ada, b_ada):
    B, D = c.shape
    out = pl.pallas_call(
        _ada_kernel,
        out_shape=jax.ShapeDtypeStruct((DEPTH * 6, B, D), F32),
        grid=(DEPTH, 6),
        in_specs=[
            pl.BlockSpec((B, D), lambda l, k: (0, 0)),
            pl.BlockSpec((1, D, D), lambda l, k: (l, 0, k)),
            pl.BlockSpec((1, 1, D), lambda l, k: (l, 0, k)),
        ],
        out_specs=pl.BlockSpec((1, B, D), lambda l, k: (l * 6 + k, 0, 0)),
        compiler_params=_cparams(("arbitrary", "arbitrary"), 32),
        name="ada",
    )(c, w_ada, b_ada.reshape(DEPTH, 1, 6 * D))
    return out.reshape(DEPTH, 6, B, 1, D)


def _mod_spec(layer, k, tiles_per_batch):
    return pl.BlockSpec((None, None, None, 1, D_MODEL),
                        lambda i, *_: (layer, k, i // tiles_per_batch, 0, 0))


def _inproj_kernel(x_ref, gain_ref, sc_ref, sh_ref, w_ref, main_ref, a0_ref, a1_ref, a2_ref, scr_ref):
    tm = x_ref.shape[0]
    h = _rms(x_ref[...], gain_ref[...]) * (1.0 + sc_ref[...]) + sh_ref[...]
    hb = h.astype(BF16)
    for c in range(MAIN_COLS // MXU_N):
        sl = slice(c * MXU_N, (c + 1) * MXU_N)
        main_ref[:, sl] = jnp.dot(hb, w_ref[:, sl], preferred_element_type=F32).astype(BF16)
    att_refs = (a0_ref, a1_ref, a2_ref)
    heads_per_slab = LANE // ATT_HEAD_DIM

    def store_heads(att_ref, r, slot0, val):
        lane = lax.broadcasted_iota(I32, val.shape, 1)
        for k in range(heads_per_slab):
            head = val if k == 0 else pltpu.roll(val, LANE - k * ATT_HEAD_DIM, 1)
            c0 = (slot0 + k) * LANE
            att_ref[r, :, c0:c0 + LANE] = jnp.where(lane < ATT_HEAD_DIM, head, 0.0).astype(BF16)

    slab = 0
    for g, (_, d) in enumerate(ATT_GROUPS):
        for seg in range(3):
            col = MAIN_COLS + g * GROUP_SRC_COLS + seg * ATT_OUT_WIDTH
            res = jnp.dot(hb, w_ref[:, col:col + ATT_OUT_WIDTH], preferred_element_type=F32)
            for s in range(ATT_OUT_WIDTH // LANE):
                part = res[:, s * LANE:(s + 1) * LANE]
                slot0 = seg * HEADS_PER_GROUP + s * heads_per_slab
                if d == 1:
                    store_heads(att_refs[g], 0, slot0, part)
                    continue
                scr_ref[slab] = part
                for r in range(d):
                    store_heads(att_refs[g], r, slot0, scr_ref[slab, pl.ds(r, tm // d, stride=d), :])
                slab += 1


def _inproj(x2, gain, mod, layer, w_in_b, B, S, tm=512):
    R, D = x2.shape
    tpb = S // tm
    n_slabs = sum(3 * (ATT_OUT_WIDTH // LANE) for _, d in ATT_GROUPS if d > 1)

    def att_shape(d):
        return jax.ShapeDtypeStruct((B, d, S // d, GROUP_COLS), BF16)

    def att_spec(d):
        return pl.BlockSpec((None, d, tm // d, GROUP_COLS), lambda i: (i // tpb, 0, i % tpb, 0))

    return pl.pallas_call(
        _inproj_kernel,
        out_shape=(jax.ShapeDtypeStruct((R, MAIN_COLS), BF16),) + tuple(att_shape(d) for _, d in ATT_GROUPS),
        grid=(R // tm,),
        in_specs=[
            pl.BlockSpec((tm, D), lambda i: (i, 0)),
            pl.BlockSpec((1, D), lambda i: (0, 0)),
            _mod_spec(layer, 1, tpb),
            _mod_spec(layer, 0, tpb),
            pl.BlockSpec((D, IN_COLS), lambda i: (0, 0), pipeline_mode=pl.Buffered(1)),
        ],
        out_specs=(pl.BlockSpec((tm, MAIN_COLS), lambda i: (i, 0)),) + tuple(att_spec(d) for _, d in ATT_GROUPS),
        scratch_shapes=[pltpu.VMEM((n_slabs, tm, LANE), F32)],
        compiler_params=_cparams(("arbitrary",), 56),
        name="inproj",
    )(x2, gain, mod, mod, w_in_b)


def _prep_w_in(w):
    b = np.concatenate([[0], np.cumsum(IN_SIZES)])
    seg = lambda s: w[:, b[s]:b[s + 1]]
    cols = [seg(5), seg(6), seg(7), seg(8), seg(3), seg(4)]
    q_a, k_a, v_a = seg(0) * (ATT_HEAD_DIM ** -0.5), seg(1), seg(2)
    for g in range(len(ATT_GROUPS)):
        gs = slice(g * ATT_OUT_WIDTH, (g + 1) * ATT_OUT_WIDTH)
        cols += [q_a[:, gs], k_a[:, gs], v_a[:, gs]]
    return jnp.concatenate(cols, axis=1).astype(BF16)


def _alibi_slopes():
    return [2.0 ** (-ALIBI_MAX * h / N_ATT_HEADS) for h in range(1, N_ATT_HEADS + 1)]


ATT_UNITS = 4


def _attn_kernel(a_ref, bias_ref, o_ref, *, n, dilation):
    nb = n // BLOCK
    win = min(2 * BLOCK, n)
    ones = jnp.ones((win, LANE), BF16)
    lane = lax.broadcasted_iota(I32, (BLOCK, LANE), 1)

    def head_units(u):
        r = u // nb
        i = u % nb
        q0 = pl.multiple_of(i * BLOCK, BLOCK)
        k0 = pl.multiple_of(jnp.clip((i - 1) * BLOCK, 0, n - win), BLOCK)
        t = jnp.minimum(i, 1)
        return [(r, pl.ds(q0, BLOCK), pl.ds(k0, win), t, h) for h in range(HEADS_PER_GROUP)]

    def pair(u2, carry):
        units = sum((head_units(ATT_UNITS * u2 + j) for j in range(ATT_UNITS)), [])
        scores = []
        for r, qrows, krows, t, h in units:
            qs = slice(h * LANE, (h + 1) * LANE)
            ks = slice((HEADS_PER_GROUP + h) * LANE, (HEADS_PER_GROUP + h + 1) * LANE)
            s = lax.dot_general(a_ref[r, qrows, qs], a_ref[r, krows, ks], (((1,), (1,)), ((), ())),
                                preferred_element_type=F32)
            scores.append(s + bias_ref[t, h])
        probs = []
        for s in scores:
            m = jnp.max(s, axis=-1, keepdims=True)
            probs.append((jnp.exp(s - m).astype(BF16), m))
        for (r, qrows, krows, t, h), (p, m) in zip(units, probs):
            qs = slice(h * LANE, (h + 1) * LANE)
            vs = slice((2 * HEADS_PER_GROUP + h) * LANE, (2 * HEADS_PER_GROUP + h + 1) * LANE)
            pv = jnp.dot(p, jnp.concatenate([a_ref[r, krows, vs], ones], axis=1), preferred_element_type=F32)
            den = pv[:, LANE:]
            o_ref[r, qrows, qs] = jnp.where(lane < ATT_HEAD_DIM, pv[:, :LANE] / den, m + jnp.log(den))
        return carry

    lax.fori_loop(0, dilation * nb // ATT_UNITS, pair, 0)


def _attn_bias(n, n_w, d, slopes):
    win = min(2 * BLOCK, n)
    r = np.arange(BLOCK)[:, None]
    c = np.arange(win)[None, :]
    tables = []
    for shift in (0, win - BLOCK):
        dist = shift + r - c
        valid = (dist >= 0) & (dist <= n_w)
        tables.append(np.stack([np.where(valid, -s * (d * dist), NEG_INF) for s in slopes]))
    return jnp.asarray(np.stack(tables), F32)


def _attn_group(att, g):
    window, d = ATT_GROUPS[g]
    B, _, n, _ = att.shape
    assert n % BLOCK == 0 and (d * (n // BLOCK)) % ATT_UNITS == 0
    slopes = _alibi_slopes()[g * HEADS_PER_GROUP:(g + 1) * HEADS_PER_GROUP]
    bias = _attn_bias(n, window // d, d, slopes)
    kern = functools.partial(_attn_kernel, n=n, dilation=d)
    out_spec = pl.BlockSpec((None, d, n, ATT_T_WIDTH), lambda b: (b, 0, 0, 0))
    return pl.pallas_call(
        kern,
        out_shape=jax.ShapeDtypeStruct((B, d, n, ATT_T_WIDTH), F32),
        grid=(B,),
        in_specs=[pl.BlockSpec((None, d, n, GROUP_COLS), lambda b: (b, 0, 0, 0)),
                  pl.BlockSpec(bias.shape, lambda b: (0, 0, 0, 0))],
        out_specs=out_spec,
        compiler_params=_cparams(("arbitrary",), 48),
        name=f"attn{g}",
    )(att, bias)


def _ret_tables(S):
    half = RET_KEY_DIM // 2
    inv_freq = 1.0 / (ROPE_BASE ** (jnp.arange(half, dtype=F32) / half))
    ang = jnp.arange(S).astype(F32)[:, None] * inv_freq[None]
    cos = jnp.cos(ang)
    sin = jnp.sin(ang)
    cos_t = jnp.tile(jnp.concatenate([cos, cos], axis=-1), (1, RET_HEADS))
    sin_t = jnp.tile(jnp.concatenate([-sin, sin], axis=-1), (1, RET_HEADS))
    log_gamma = jnp.log1p(-jnp.exp2(-5.0 - jnp.arange(RET_HEADS, dtype=F32)))
    idx = jnp.arange(RET_CHUNK, dtype=F32)
    rel = idx[:, None] - idx[None, :]
    decay = jnp.where(rel[None] >= 0, jnp.exp(jnp.maximum(rel, 0.0)[None] * log_gamma[:, None, None]), 0.0)
    zeta = jnp.exp((RET_CHUNK - 1 - idx)[None] * log_gamma[:, None])
    xi = jnp.exp((idx + 1)[None] * log_gamma[:, None])
    zeta_t = jnp.repeat(zeta.T, RET_KEY_DIM, axis=1) * (RET_KEY_DIM ** -0.5)
    xi_t = jnp.repeat(xi.T, RET_KEY_DIM, axis=1)
    g_chunk = jnp.exp(RET_CHUNK * log_gamma)
    g_t = jnp.broadcast_to(g_chunk[:, None, None], (RET_HEADS, 1, RET_VAL_DIM))
    return cos_t, sin_t, decay, zeta_t, xi_t, g_t


def _rot_half_swap(t):
    w = t.shape[-1]
    lane = lax.broadcasted_iota(I32, t.shape, 1) % RET_KEY_DIM
    half = RET_KEY_DIM // 2
    return jnp.where(lane < half, pltpu.roll(t, w - half, 1), pltpu.roll(t, half, 1))


RET_ROWS = 1024
RET_PAIR = 4


def _retn_kernel(q_ref, k_ref, v_ref, g_ref, cos_ref, sin_ref, decay_ref, zeta_ref, xi_ref, gch_ref,
                 o_ref, state_ref):
    C = RET_CHUNK

    @pl.when(pl.program_id(1) == 0)
    def _():
        state_ref[...] = jnp.zeros_like(state_ref)

    def prep(c):
        rows = pl.ds(pl.multiple_of(c * C, C), C)
        cos = cos_ref[rows, :]
        sin = sin_ref[rows, :]
        q = q_ref[rows, :].astype(F32)
        k = k_ref[rows, :].astype(F32)
        q = q * cos + _rot_half_swap(q) * sin
        k = k * cos + _rot_half_swap(k) * sin
        qb = q.astype(BF16)
        kb = (k * (RET_KEY_DIM ** -0.5)).astype(BF16)
        q_xi = (q * xi_ref[...]).astype(BF16)
        kz_t = (k * zeta_ref[...]).T
        return rows, qb, kb, q_xi, kz_t

    def group(u, carry):
        chunks = [prep(u * RET_PAIR + i) for i in range(RET_PAIR)]
        ksl = [slice(h * RET_KEY_DIM, (h + 1) * RET_KEY_DIM) for h in range(RET_HEADS)]
        vsl = [slice(h * RET_VAL_DIM, (h + 1) * RET_VAL_DIM) for h in range(RET_HEADS)]
        units = [(c, h) for h in range(RET_HEADS) for c in range(RET_PAIR)]
        scores, kvs = {}, {}
        for c, h in units:
            rows, qb, kb, q_xi, kz_t = chunks[c]
            v = v_ref[rows, vsl[h]]
            scores[c, h] = lax.dot_general(qb[:, ksl[h]], kb[:, ksl[h]], (((1,), (1,)), ((), ())),
                                           preferred_element_type=F32)
            kvs[c, h] = jnp.dot(kz_t[ksl[h], :].astype(BF16), v, preferred_element_type=F32)
        states = {}
        for h in range(RET_HEADS):
            st = state_ref[h]
            for c in range(RET_PAIR):
                states[c, h] = st.astype(BF16)
                st = st * gch_ref[h] + kvs[c, h]
            state_ref[h] = st
        outs = {}
        for c, h in units:
            rows, qb, kb, q_xi, kz_t = chunks[c]
            s = (scores[c, h] * decay_ref[h]).astype(BF16)
            outs[c, h] = (jnp.dot(s, v_ref[rows, vsl[h]], preferred_element_type=F32)
                          + jnp.dot(q_xi[:, ksl[h]], states[c, h], preferred_element_type=F32))
        for c, h in units:
            rows = chunks[c][0]
            o = outs[c, h]
            o = o * lax.rsqrt(jnp.mean(o * o, axis=-1, keepdims=True) + EPS)
            o_ref[rows, vsl[h]] = (_silu(g_ref[rows, vsl[h]].astype(F32)) * o).astype(o_ref.dtype)
        return carry

    lax.fori_loop(0, q_ref.shape[0] // (C * RET_PAIR), group, 0)


def _retention(main3, B, S):
    C = RET_CHUNK
    RS = RET_ROWS
    assert S % RS == 0 and RS % (C * RET_PAIR) == 0
    cos_t, sin_t, decay, zeta_t, xi_t, g_t = _ret_tables(S)
    const2 = lambda b, c: (0, 0)
    const3 = lambda b, c: (0, 0, 0)
    qk0 = 4 * D_MODEL // RET_QK_WIDTH
    out = pl.pallas_call(
        _retn_kernel,
        out_shape=jax.ShapeDtypeStruct((B, S, RET_V_WIDTH), BF16),
        grid=(B, S // RS),
        in_specs=[
            pl.BlockSpec((None, RS, RET_QK_WIDTH), lambda b, c: (b, c, qk0)),
            pl.BlockSpec((None, RS, RET_QK_WIDTH), lambda b, c: (b, c, qk0 + 1)),
            pl.BlockSpec((None, RS, RET_V_WIDTH), lambda b, c: (b, c, 0)),
            pl.BlockSpec((None, RS, RET_V_WIDTH), lambda b, c: (b, c, 1)),
            pl.BlockSpec((RS, RET_QK_WIDTH), lambda b, c: (c, 0)),
            pl.BlockSpec((RS, RET_QK_WIDTH), lambda b, c: (c, 0)),
            pl.BlockSpec((RET_HEADS, C, C), const3),
            pl.BlockSpec((C, RET_QK_WIDTH), const2),
            pl.BlockSpec((C, RET_QK_WIDTH), const2),
            pl.BlockSpec((RET_HEADS, 1, RET_VAL_DIM), const3),
        ],
        out_specs=pl.BlockSpec((None, RS, RET_V_WIDTH), lambda b, c: (b, c, 0)),
        scratch_shapes=[pltpu.VMEM((RET_HEADS, RET_KEY_DIM, RET_VAL_DIM), F32)],
        compiler_params=_cparams(("arbitrary", "arbitrary"), 48),
        name="retn",
    )(main3, main3, main3, main3, cos_t, sin_t, decay, zeta_t, xi_t, g_t)
    return out.reshape(B * S, RET_V_WIDTH)


N_META = 6


def _unstride(ref, scr_ref, slab0, d, tm):
    if d == 1:
        return [ref[0, :, s * LANE:(s + 1) * LANE] for s in range(HEADS_PER_GROUP)]
    for r in range(d):
        for s in range(HEADS_PER_GROUP):
            scr_ref[slab0 + s, pl.ds(r, tm // d, stride=d), :] = ref[r, :, s * LANE:(s + 1) * LANE]
    return [scr_ref[slab0 + s] for s in range(HEADS_PER_GROUP)]


def _postmix_kernel(o0_ref, o1_ref, o2_ref, ret_ref, ga_ref, gr_ref, x_ref,
                    wa_ref, wr_ref, wo_ref, npost_ref, g1_ref, npre_ref, sc2_ref, sh2_ref, *rest, with_router):
    if with_router:
        wrt_ref, brt_ref, xo_ref, h_ref, meta_ref, cnt_ref, scr_ref, carry_ref = rest
    else:
        xo_ref, h_ref, scr_ref = rest
    tm = x_ref.shape[0]
    groups = []
    slab = 0
    for (_, d), o_ref in zip(ATT_GROUPS, (o0_ref, o1_ref, o2_ref)):
        groups.append(_unstride(o_ref, scr_ref, slab, d, tm))
        if d > 1:
            slab += HEADS_PER_GROUP
    head_lane = lax.broadcasted_iota(I32, (tm, LANE), 1)
    slots = []
    for hd in range(HEADS_PER_GROUP):
        outs = [grp[hd] for grp in groups]
        lses = [pltpu.roll(o, LANE // 2, 1) for o in outs]
        mx = functools.reduce(jnp.maximum, lses)
        es = [jnp.exp(l - mx) for l in lses]
        num = sum(e * o for e, o in zip(es, outs))
        slots.append(jnp.where(head_lane < ATT_HEAD_DIM, num / sum(es), 0.0))
    att = jnp.concatenate(slots, axis=1)
    a = jnp.dot(att.astype(BF16), wa_ref[...], preferred_element_type=F32)
    r = jnp.dot(ret_ref[...], wr_ref[...], preferred_element_type=F32)
    merged = jax.nn.sigmoid(ga_ref[...].astype(F32)) * a + jax.nn.sigmoid(gr_ref[...].astype(F32)) * r
    y = jnp.dot(merged.astype(BF16), wo_ref[...], preferred_element_type=F32)
    x = x_ref[...] + g1_ref[...] * _rms(y, npost_ref[...])
    xo_ref[...] = x
    h = _rms(x, npre_ref[...]) * (1.0 + sc2_ref[...]) + sh2_ref[...]
    h_ref[...] = h.astype(h_ref.dtype)
    if not with_router:
        return

    @pl.when(pl.program_id(0) == 0)
    def _():
        carry_ref[...] = jnp.zeros_like(carry_ref)

    h_hi = h.astype(BF16)
    h_lo = (h - h_hi.astype(F32)).astype(BF16)
    w = wrt_ref[...]
    w_hi = w.astype(BF16)
    w_lo = (w - w_hi.astype(F32)).astype(BF16)
    lg = (jnp.dot(h_hi, w_hi, preferred_element_type=F32) + jnp.dot(h_hi, w_lo, preferred_element_type=F32)
          + jnp.dot(h_lo, w_hi, preferred_element_type=F32)) + brt_ref[...]
    lane = lax.broadcasted_iota(I32, lg.shape, 1).astype(F32)
    lg = jnp.where(lane < N_EXPERTS, lg, -jnp.inf)
    m1 = jnp.max(lg, axis=-1, keepdims=True)
    i1 = jnp.min(jnp.where(lg == m1, lane, float(LANE)), axis=-1, keepdims=True)
    lg2 = jnp.where(lane == i1, -jnp.inf, lg)
    m2 = jnp.max(lg2, axis=-1, keepdims=True)
    i2 = jnp.min(jnp.where(lg2 == m2, lane, float(LANE)), axis=-1, keepdims=True)
    t = jnp.exp(m2 - m1)
    w1 = 1.0 / (1.0 + t)
    w2 = t / (1.0 + t)
    oh = jnp.where((lane == i1) | (lane == i2), 1.0, 0.0)
    row = lax.broadcasted_iota(I32, (tm, tm), 0)
    col = lax.broadcasted_iota(I32, (tm, tm), 1)
    tri = jnp.where(col < row, 1.0, 0.0).astype(BF16)
    pos = jnp.dot(tri, oh.astype(BF16), preferred_element_type=F32) + carry_ref[...]
    r1 = jnp.sum(jnp.where(lane == i1, pos, 0.0), axis=-1, keepdims=True)
    r2 = jnp.sum(jnp.where(lane == i2, pos, 0.0), axis=-1, keepdims=True)
    carry = carry_ref[...] + jnp.sum(oh, axis=0, keepdims=True)
    carry_ref[...] = carry
    cnt_ref[...] = jnp.broadcast_to(carry, cnt_ref.shape)
    meta = jnp.zeros(lg.shape, F32)
    for j, val in enumerate((i1, i2, r1, r2, w1, w2)):
        meta = jnp.where(lane == j, val, meta)
    meta_ref[...] = meta


def _postmix(outs, ret, main, x2, wa, wr, wo, npost, npre, mod, layer, S, router=None, tm=512):
    R, D = x2.shape
    tpb = S // tm
    row = lambda w: pl.BlockSpec((tm, w), lambda i: (i, 0))
    const = lambda a: pl.BlockSpec(a.shape, lambda i: (0, 0))

    def att_spec(d):
        return pl.BlockSpec((None, d, tm // d, ATT_T_WIDTH), lambda i: (i // tpb, 0, i % tpb, 0))

    specs_att = [att_spec(d) for _, d in ATT_GROUPS]
    in_specs = (specs_att + [
        row(RET_V_WIDTH),
        pl.BlockSpec((tm, D), lambda i: (i, 2)),
        pl.BlockSpec((tm, D), lambda i: (i, 3)),
        row(D), const(wa), const(wr), const(wo), const(npost),
        _mod_spec(layer, 2, tpb), const(npre), _mod_spec(layer, 4, tpb), _mod_spec(layer, 3, tpb)])
    args = list(outs) + [ret, main, main, x2, wa, wr, wo, npost, mod, npre, mod, mod]
    n_slabs = sum(HEADS_PER_GROUP for _, d in ATT_GROUPS if d > 1)
    scratch = [pltpu.VMEM((n_slabs, tm, LANE), F32)]
    if router is None:
        out_shape = [jax.ShapeDtypeStruct((R, D), F32), jax.ShapeDtypeStruct((R, D), BF16)]
        out_specs = [row(D), row(D)]
        sem = ("arbitrary",)
    else:
        in_specs += [const(router[0]), const(router[1])]
        args += list(router)
        out_shape = [jax.ShapeDtypeStruct((R, D), F32), jax.ShapeDtypeStruct((R, D), F32),
                     jax.ShapeDtypeStruct((R, LANE), F32), jax.ShapeDtypeStruct((8, LANE), F32)]
        out_specs = [row(D), row(D), row(LANE), pl.BlockSpec((8, LANE), lambda i: (0, 0))]
        scratch.append(pltpu.VMEM((1, LANE), F32))
        sem = ("arbitrary",)
    return pl.pallas_call(
        functools.partial(_postmix_kernel, with_router=router is not None),
        out_shape=tuple(out_shape),
        grid=(R // tm,),
        in_specs=in_specs,
        out_specs=tuple(out_specs),
        scratch_shapes=scratch,
        compiler_params=_cparams(sem, 56),
        name="postmix",
    )(*args)


FF_TILE = D_FF // 2


def _swiglu_partial(h, wg_ref, wu_ref, wd_ref):
    act = _silu(jnp.dot(h, wg_ref[...], preferred_element_type=F32)) * jnp.dot(h, wu_ref[...], preferred_element_type=F32)
    return jnp.dot(act.astype(BF16), wd_ref[...], preferred_element_type=F32)


def _ffn_kernel(h_ref, x_ref, wg_ref, wu_ref, wd_ref, npost_ref, g2_ref, o_ref, acc_ref):
    j = pl.program_id(1)

    @pl.when(j == 0)
    def _():
        acc_ref[...] = jnp.zeros_like(acc_ref)

    acc_ref[...] += _swiglu_partial(h_ref[...], wg_ref, wu_ref, wd_ref)

    @pl.when(j == pl.num_programs(1) - 1)
    def _():
        o_ref[...] = x_ref[...] + g2_ref[...] * _rms(acc_ref[...], npost_ref[...])


def _ffn(h2, x2, wg, wu, wd, npost, mod, layer, S, tm=512):
    R, D = x2.shape
    tpb = S // tm
    nf = D_FF // FF_TILE
    return pl.pallas_call(
        _ffn_kernel,
        out_shape=jax.ShapeDtypeStruct((R, D), F32),
        grid=(R // tm, nf),
        in_specs=[
            pl.BlockSpec((tm, D), lambda i, j: (i, 0)),
            pl.BlockSpec((tm, D), lambda i, j: (i, 0)),
            pl.BlockSpec((D, FF_TILE), lambda i, j: (0, j)),
            pl.BlockSpec((D, FF_TILE), lambda i, j: (0, j)),
            pl.BlockSpec((FF_TILE, D), lambda i, j: (j, 0)),
            pl.BlockSpec((1, D), lambda i, j: (0, 0)),
            _mod_spec(layer, 5, tpb),
        ],
        out_specs=pl.BlockSpec((tm, D), lambda i, j: (i, 0)),
        scratch_shapes=[pltpu.VMEM((tm, D), F32)],
        compiler_params=_cparams(("arbitrary", "arbitrary"), 48),
        name="ffn",
    )(h2, x2, wg, wu, wd, npost, mod)


MOE_TM = 512
ROW_UNROLL = 8


def _invperm_kernel(n_ref, dest_ref, src_ref):
    def zero(g, c):
        for u in range(ROW_UNROLL):
            src_ref[g * ROW_UNROLL + u] = 0
        return c

    lax.fori_loop(0, n_ref[0], zero, 0)

    def put(g, c):
        for u in range(ROW_UNROLL):
            a = g * ROW_UNROLL + u
            src_ref[dest_ref[a]] = a
        return c

    lax.fori_loop(0, n_ref[1], put, 0)


def _invperm(dest, P):
    assert P % ROW_UNROLL == 0 and dest.shape[0] % ROW_UNROLL == 0
    trips = jnp.array([P // ROW_UNROLL, dest.shape[0] // ROW_UNROLL], I32)
    return pl.pallas_call(
        _invperm_kernel,
        out_shape=jax.ShapeDtypeStruct((P,), I32),
        in_specs=[pl.BlockSpec(memory_space=pltpu.SMEM), pl.BlockSpec(memory_space=pltpu.SMEM)],
        out_specs=pl.BlockSpec(memory_space=pltpu.SMEM),
        name="invperm",
    )(trips, dest)


def _expert_kernel(te_ref, nu_ref, src_ref, h_hbm, wg_ref, wu_ref, wd_ref, o_ref, acc_ref, hb_ref, hbuf_ref, sem,
                   *, n_tok):
    j = pl.program_id(0)
    f = pl.program_id(1)
    nu = nu_ref[0]
    tm = o_ref.shape[0]

    def row_copy(row, slot, i):
        return pltpu.make_async_copy(h_hbm.at[pl.ds(row, 1), :], hbuf_ref.at[slot, pl.ds(i, 1), :], sem.at[slot])

    def issue(tile, slot):
        def body(i, c):
            row_copy(src_ref[tile * tm + i] & (n_tok - 1), slot, i).start()
            return c

        lax.fori_loop(0, tm, body, 0, unroll=ROW_UNROLL)

    def wait(slot):
        def body(i, c):
            row_copy(0, slot, 0).wait()
            return c

        lax.fori_loop(0, tm, body, 0, unroll=ROW_UNROLL)

    nf = pl.num_programs(1)
    slot = j % 2
    n_ahead = tm // (D_FF // FF_TILE)

    @pl.when((f == 0) & (j == 0))
    def _():
        issue(0, 0)

    @pl.when((f == 0) & (j <= nu))
    def _():
        wait(slot)

    @pl.when(j < nu)
    def _():
        @pl.when(f == 0)
        def _():
            acc_ref[...] = jnp.zeros_like(acc_ref)
            hb_ref[...] = hbuf_ref[slot].astype(BF16)

        nxt = jnp.minimum(j + 1, pl.num_programs(0) - 1)
        for i in range(n_ahead):
            r = f * n_ahead + i
            row_copy(src_ref[nxt * tm + r] & (n_tok - 1), 1 - slot, r).start()
        acc_ref[...] += _swiglu_partial(hb_ref[...], wg_ref, wu_ref, wd_ref)

        @pl.when(f == nf - 1)
        def _():
            o_ref[...] = acc_ref[...]

    @pl.when((j == pl.num_programs(0) - 1) & (f == nf - 1) & (j < nu))
    def _():
        wait(1 - slot)

    @pl.when((j >= nu) & (f == pl.num_programs(1) - 1))
    def _():
        o_ref[...] = jnp.zeros_like(o_ref)


def _experts(tile_expert, n_used, src, h2, wg, wu, wd, n_tiles, tm):
    R, D = h2.shape
    assert R & (R - 1) == 0
    nf = D_FF // FF_TILE
    tile = lambda j, nu: jnp.minimum(j, nu[0] - 1)
    ff = lambda j, f, nu: jnp.where(j < nu[0], f, nf - 1)
    return pl.pallas_call(
        functools.partial(_expert_kernel, n_tok=R),
        out_shape=jax.ShapeDtypeStruct((n_tiles * tm, D), F32),
        grid_spec=pltpu.PrefetchScalarGridSpec(
            num_scalar_prefetch=3,
            grid=(n_tiles, nf),
            in_specs=[
                pl.BlockSpec(memory_space=pl.ANY),
                pl.BlockSpec((None, D, FF_TILE), lambda j, f, te, nu, src: (te[tile(j, nu)], 0, ff(j, f, nu))),
                pl.BlockSpec((None, D, FF_TILE), lambda j, f, te, nu, src: (te[tile(j, nu)], 0, ff(j, f, nu))),
                pl.BlockSpec((None, FF_TILE, D), lambda j, f, te, nu, src: (te[tile(j, nu)], ff(j, f, nu), 0)),
            ],
            out_specs=pl.BlockSpec((tm, D), lambda j, f, te, nu, src: (j, 0)),
            scratch_shapes=[pltpu.VMEM((tm, D), F32), pltpu.VMEM((tm, D), BF16), pltpu.VMEM((2, tm, D), F32),
                            pltpu.SemaphoreType.DMA((2,))],
        ),
        compiler_params=_cparams(("arbitrary", "arbitrary"), 48),
        name="moe_experts",
    )(tile_expert, n_used, src, h2, wg, wu, wd)


def _combine_kernel(d1_ref, d2_ref, y_hbm, meta_ref, x_ref, npost_ref, g2_ref, o_ref, ya_ref, yb_ref, sem):
    tm = x_ref.shape[0]
    t0 = pl.program_id(0) * tm

    def row_copy(row, dst_ref, r):
        return pltpu.make_async_copy(y_hbm.at[pl.ds(row, 1), :], dst_ref.at[pl.ds(r, 1), :], sem)

    def issue(r, c):
        row_copy(d1_ref[t0 + r], ya_ref, r).start()
        row_copy(d2_ref[t0 + r], yb_ref, r).start()
        return c

    lax.fori_loop(0, tm, issue, 0, unroll=ROW_UNROLL)

    def wait(r, c):
        row_copy(0, ya_ref, 0).wait()
        row_copy(0, yb_ref, 0).wait()
        return c

    lax.fori_loop(0, tm, wait, 0, unroll=ROW_UNROLL)
    meta = meta_ref[...]
    y = meta[:, 4:5] * ya_ref[...] + meta[:, 5:6] * yb_ref[...]
    o_ref[...] = x_ref[...] + g2_ref[...] * _rms(y, npost_ref[...])


def _combine(dest1, dest2, sorted_y, meta, x2, npost, mod, layer, S, tm=512):
    R, D = x2.shape
    tpb = S // tm
    return pl.pallas_call(
        _combine_kernel,
        out_shape=jax.ShapeDtypeStruct((R, D), F32),
        grid_spec=pltpu.PrefetchScalarGridSpec(
            num_scalar_prefetch=2,
            grid=(R // tm,),
            in_specs=[
                pl.BlockSpec(memory_space=pl.ANY),
                pl.BlockSpec((tm, LANE), lambda i, *_: (i, 0)),
                pl.BlockSpec((tm, D), lambda i, *_: (i, 0)),
                pl.BlockSpec((1, D), lambda i, *_: (0, 0)),
                _mod_spec(layer, 5, tpb),
            ],
            out_specs=pl.BlockSpec((tm, D), lambda i, *_: (i, 0)),
            scratch_shapes=[pltpu.VMEM((tm, D), F32), pltpu.VMEM((tm, D), F32), pltpu.SemaphoreType.DMA],
        ),
        compiler_params=_cparams(("arbitrary",), 40),
        name="moe_combine",
    )(dest1, dest2, sorted_y, meta, x2, npost, mod)


def _moe(h2, meta, counts, x2, wg, wu, wd, npost, mod, layer, S):
    R, D = x2.shape
    tm = MOE_TM
    n_tiles = (TOP_K * R + N_EXPERTS * (tm - 1)) // tm + 1
    i1, i2, r1, r2 = (meta[:, j].astype(I32) for j in range(4))
    cnt = counts[0, :N_EXPERTS].astype(I32)
    padded = (cnt + tm - 1) // tm * tm
    end = jnp.cumsum(padded)
    start = end - padded
    eye = jnp.arange(N_EXPERTS, dtype=I32)[None, :]
    dest1 = jnp.sum(jnp.where(i1[:, None] == eye, start[None, :], 0), axis=1) + r1
    dest2 = jnp.sum(jnp.where(i2[:, None] == eye, start[None, :], 0), axis=1) + r2
    tile_row0 = jnp.arange(n_tiles, dtype=I32) * tm
    tile_expert = jnp.minimum(jnp.sum(tile_row0[:, None] >= end[None, :], axis=1), N_EXPERTS - 1).astype(I32)
    n_used = (end[-1:] // tm).astype(I32)
    src = _invperm(jnp.concatenate([dest1, dest2]), n_tiles * tm)
    sorted_y = _experts(tile_expert, n_used, src, h2, wg, wu, wd, n_tiles, tm)
    return _combine(dest1, dest2, sorted_y, meta, x2, npost, mod, layer, S)


def kernel(x, c, w_ada, b_ada, norm_pre_mix, w_in, w_att_proj, w_ret_proj, w_mix_out, norm_post_mix, norm_pre_ffn, w_ffn_gate, w_ffn_up, w_ffn_down, w_router, b_router, w_exp_gate, w_exp_up, w_exp_down, norm_post_ffn):
    B, S, D = x.shape
    R = B * S
    mod = _ada(c, w_ada, b_ada)
    x2 = x.reshape(R, D)
    row = lambda v: v.reshape(1, D)
    for layer in range(DEPTH):
        main, *atts = _inproj(x2, row(norm_pre_mix[layer]), mod, layer, _prep_w_in(w_in[layer]), B, S)
        outs = [_attn_group(atts[g], g) for g in range(len(ATT_GROUPS))]
        ret = _retention(main.reshape(B, S, MAIN_COLS), B, S)
        j = layer // 2
        router = None
        if layer % 2 == 1:
            wrt = jnp.zeros((D, LANE), F32).at[:, :N_EXPERTS].set(w_router[j])
            brt = jnp.zeros((1, LANE), F32).at[0, :N_EXPERTS].set(b_router[j])
            router = (wrt, brt)
        wa = jnp.pad(w_att_proj[layer].reshape(HEADS_PER_GROUP, ATT_HEAD_DIM, D),
                     ((0, 0), (0, LANE - ATT_HEAD_DIM), (0, 0))).reshape(ATT_T_WIDTH, D)
        res = _postmix(outs, ret, main, x2, wa.astype(BF16), w_ret_proj[layer].astype(BF16),
                       w_mix_out[layer].astype(BF16), row(norm_post_mix[layer]), row(norm_pre_ffn[layer]),
                       mod, layer, S, router=router)
        if layer % 2 == 0:
            x2, h2 = res
            x2 = _ffn(h2, x2, w_ffn_gate[j].astype(BF16), w_ffn_up[j].astype(BF16), w_ffn_down[j].astype(BF16),
                      row(norm_post_ffn[layer]), mod, layer, S)
        else:
            x2, h2, meta, counts = res
            x2 = _moe(h2, meta, counts, x2, w_exp_gate[j].astype(BF16), w_exp_up[j].astype(BF16),
                      w_exp_down[j].astype(BF16), row(norm_post_ffn[layer]), mod, layer, S)
    return x2.reshape(B, S, D)
```

```python
import functools

import numpy as np
import jax
import jax.numpy as jnp
from jax import lax
from jax.experimental import pallas as pl
from jax.experimental.pallas import tpu as pltpu

F32 = jnp.float32
BF16 = jnp.bfloat16
I32 = jnp.int32

D_MODEL = 1024
DEPTH = 2
ATT_GROUPS = ((128, 1), (512, 4), (2048, 16))
HEADS_PER_GROUP = 4
N_ATT_HEADS = HEADS_PER_GROUP * len(ATT_GROUPS)
ATT_HEAD_DIM = 64
ATT_WIDTH = N_ATT_HEADS * ATT_HEAD_DIM
ATT_OUT_WIDTH = HEADS_PER_GROUP * ATT_HEAD_DIM
BLOCK = 128
ALIBI_MAX = 8.0
NEG_INF = -1e30
RET_HEADS = 8
RET_KEY_DIM = 64
RET_VAL_DIM = 128
RET_QK_WIDTH = RET_HEADS * RET_KEY_DIM
RET_V_WIDTH = RET_HEADS * RET_VAL_DIM
RET_CHUNK = 128
ROPE_BASE = 10000.0
D_FF = 2816
N_EXPERTS = 8
TOP_K = 2
EPS = 1e-6
IN_SIZES = (ATT_WIDTH, ATT_WIDTH, ATT_WIDTH, RET_QK_WIDTH, RET_QK_WIDTH, RET_V_WIDTH, RET_V_WIDTH, D_MODEL, D_MODEL)
IN_COLS = sum(IN_SIZES)

LANE = 128
MAIN_COLS = 4 * D_MODEL + 2 * RET_QK_WIDTH
GROUP_SRC_COLS = 3 * ATT_OUT_WIDTH
GROUP_SLOTS = 3 * HEADS_PER_GROUP
GROUP_COLS = GROUP_SLOTS * LANE
ATT_T_WIDTH = HEADS_PER_GROUP * LANE
MXU_N = 256

MiB = 1024 * 1024


def _cparams(sem, vmem_mib):
    return pltpu.CompilerParams(dimension_semantics=sem, vmem_limit_bytes=vmem_mib * MiB)


def _rms(x, gain):
    return x * lax.rsqrt(jnp.mean(x * x, axis=-1, keepdims=True) + EPS) * gain


def _silu(x):
    return x * jax.nn.sigmoid(x)


def _ada_kernel(c_ref, w_ref, b_ref, o_ref):
    ca = _silu(c_ref[...])
    o_ref[0] = jnp.dot(ca, w_ref[0], preferred_element_type=F32, precision=lax.Precision.HIGHEST) + b_ref[0]


def _ada(c, w_ada, b_ada):
    B, D = c.shape
    out = pl.pallas_call(
        _ada_kernel,
        out_shape=jax.ShapeDtypeStruct((DEPTH * 6, B, D), F32),
        grid=(DEPTH, 6),
        in_specs=[
            pl.BlockSpec((B, D), lambda l, k: (0, 0)),
            pl.BlockSpec((1, D, D), lambda l, k: (l, 0, k)),
            pl.BlockSpec((1, 1, D), lambda l, k: (l, 0, k)),
        ],
        out_specs=pl.BlockSpec((1, B, D), lambda l, k: (l * 6 + k, 0, 0)),
        compiler_params=_cparams(("arbitrary", "arbitrary"), 32),
        name="ada",
    )(c, w_ada, b_ada.reshape(DEPTH, 1, 6 * D))
    return out.reshape(DEPTH, 6, B, 1, D)


def _mod_spec(layer, k, tiles_per_batch):
    return pl.BlockSpec((None, None, None, 1, D_MODEL),
                        lambda i, *_: (layer, k, i // tiles_per_batch, 0, 0))


def _inproj_kernel(x_ref, gain_ref, sc_ref, sh_ref, w_ref, main_ref, a0_ref, a1_ref, a2_ref, scr_ref):
    tm = x_ref.shape[0]
    h = _rms(x_ref[...], gain_ref[...]) * (1.0 + sc_ref[...]) + sh_ref[...]
    hb = h.astype(BF16)
    for c in range(MAIN_COLS // MXU_N):
        sl = slice(c * MXU_N, (c + 1) * MXU_N)
        main_ref[:, sl] = jnp.dot(hb, w_ref[:, sl], preferred_element_type=F32).astype(BF16)
    att_refs = (a0_ref, a1_ref, a2_ref)
    heads_per_slab = LANE // ATT_HEAD_DIM

    def store_heads(att_ref, r, slot0, val):
        lane = lax.broadcasted_iota(I32, val.shape, 1)
        for k in range(heads_per_slab):
            head = val if k == 0 else pltpu.roll(val, LANE - k * ATT_HEAD_DIM, 1)
            c0 = (slot0 + k) * LANE
            att_ref[r, :, c0:c0 + LANE] = jnp.where(lane < ATT_HEAD_DIM, head, 0.0).astype(BF16)

    slab = 0
    for g, (_, d) in enumerate(ATT_GROUPS):
        for seg in range(3):
            col = MAIN_COLS + g * GROUP_SRC_COLS + seg * ATT_OUT_WIDTH
            res = jnp.dot(hb, w_ref[:, col:col + ATT_OUT_WIDTH], preferred_element_type=F32)
            for s in range(ATT_OUT_WIDTH // LANE):
                part = res[:, s * LANE:(s + 1) * LANE]
                slot0 = seg * HEADS_PER_GROUP + s * heads_per_slab
                if d == 1:
                    store_heads(att_refs[g], 0, slot0, part)
                    continue
                scr_ref[slab] = part
                for r in range(d):
                    store_heads(att_refs[g], r, slot0, scr_ref[slab, pl.ds(r, tm // d, stride=d), :])
                slab += 1


def _inproj(x2, gain, mod, layer, w_in_b, B, S, tm=512):
    R, D = x2.shape
    tpb = S // tm
    n_slabs = sum(3 * (ATT_OUT_WIDTH // LANE) for _, d in ATT_GROUPS if d > 1)

    def att_shape(d):
        return jax.ShapeDtypeStruct((B, d, S // d, GROUP_COLS), BF16)

    def att_spec(d):
        return pl.BlockSpec((None, d, tm // d, GROUP_COLS), lambda i: (i // tpb, 0, i % tpb, 0))

    return pl.pallas_call(
        _inproj_kernel,
        out_shape=(jax.ShapeDtypeStruct((R, MAIN_COLS), BF16),) + tuple(att_shape(d) for _, d in ATT_GROUPS),
        grid=(R // tm,),
        in_specs=[
            pl.BlockSpec((tm, D), lambda i: (i, 0)),
            pl.BlockSpec((1, D), lambda i: (0, 0)),
            _mod_spec(layer, 1, tpb),
            _mod_spec(layer, 0, tpb),
            pl.BlockSpec((D, IN_COLS), lambda i: (0, 0), pipeline_mode=pl.Buffered(1)),
        ],
        out_specs=(pl.BlockSpec((tm, MAIN_COLS), lambda i: (i, 0)),) + tuple(att_spec(d) for _, d in ATT_GROUPS),
        scratch_shapes=[pltpu.VMEM((n_slabs, tm, LANE), F32)],
        compiler_params=_cparams(("arbitrary",), 56),
        name="inproj",
    )(x2, gain, mod, mod, w_in_b)


def _prep_w_in(w):
    b = np.concatenate([[0], np.cumsum(IN_SIZES)])
    seg = lambda s: w[:, b[s]:b[s + 1]]
    cols = [seg(5), seg(6), seg(7), seg(8), seg(3), seg(4)]
    q_a, k_a, v_a = seg(0) * (ATT_HEAD_DIM ** -0.5), seg(1), seg(2)
    for g in range(len(ATT_GROUPS)):
        gs = slice(g * ATT_OUT_WIDTH, (g + 1) * ATT_OUT_WIDTH)
        cols += [q_a[:, gs], k_a[:, gs], v_a[:, gs]]
    return jnp.concatenate(cols, axis=1).astype(BF16)


def _alibi_slopes():
    return [2.0 ** (-ALIBI_MAX * h / N_ATT_HEADS) for h in range(1, N_ATT_HEADS + 1)]


ATT_UNITS = 4


def _attn_kernel(a_ref, bias_ref, o_ref, *, n, dilation):
    nb = n // BLOCK
    win = min(2 * BLOCK, n)
    ones = jnp.ones((win, LANE), BF16)
    lane = lax.broadcasted_iota(I32, (BLOCK, LANE), 1)

    def head_units(u):
        r = u // nb
        i = u % nb
        q0 = pl.multiple_of(i * BLOCK, BLOCK)
        k0 = pl.multiple_of(jnp.clip((i - 1) * BLOCK, 0, n - win), BLOCK)
        t = jnp.minimum(i, 1)
        return [(r, pl.ds(q0, BLOCK), pl.ds(k0, win), t, h) for h in range(HEADS_PER_GROUP)]

    def pair(u2, carry):
        units = sum((head_units(ATT_UNITS * u2 + j) for j in range(ATT_UNITS)), [])
        scores = []
        for r, qrows, krows, t, h in units:
            qs = slice(h * LANE, (h + 1) * LANE)
            ks = slice((HEADS_PER_GROUP + h) * LANE, (HEADS_PER_GROUP + h + 1) * LANE)
            s = lax.dot_general(a_ref[r, qrows, qs], a_ref[r, krows, ks], (((1,), (1,)), ((), ())),
                                preferred_element_type=F32)
            scores.append(s + bias_ref[t, h])
        probs = []
        for s in scores:
            m = jnp.max(s, axis=-1, keepdims=True)
            probs.append((jnp.exp(s - m).astype(BF16), m))
        for (r, qrows, krows, t, h), (p, m) in zip(units, probs):
            qs = slice(h * LANE, (h + 1) * LANE)
            vs = slice((2 * HEADS_PER_GROUP + h) * LANE, (2 * HEADS_PER_GROUP + h + 1) * LANE)
            pv = jnp.dot(p, jnp.concatenate([a_ref[r, krows, vs], ones], axis=1), preferred_element_type=F32)
            den = pv[:, LANE:]
            o_ref[r, qrows, qs] = jnp.where(lane < ATT_HEAD_DIM, pv[:, :LANE] / den, m + jnp.log(den))
        return carry

    lax.fori_loop(0, dilation * nb // ATT_UNITS, pair, 0)


def _attn_bias(n, n_w, d, slopes):
    win = min(2 * BLOCK, n)
    r = np.arange(BLOCK)[:, None]
    c = np.arange(win)[None, :]
    tables = []
    for shift in (0, win - BLOCK):
        dist = shift + r - c
        valid = (dist >= 0) & (dist <= n_w)
        tables.append(np.stack([np.where(valid, -s * (d * dist), NEG_INF) for s in slopes]))
    return jnp.asarray(np.stack(tables), F32)


def _attn_group(att, g):
    window, d = ATT_GROUPS[g]
    B, _, n, _ = att.shape
    assert n % BLOCK == 0 and (d * (n // BLOCK)) % ATT_UNITS == 0
    slopes = _alibi_slopes()[g * HEADS_PER_GROUP:(g + 1) * HEADS_PER_GROUP]
    bias = _attn_bias(n, window // d, d, slopes)
    kern = functools.partial(_attn_kernel, n=n, dilation=d)
    out_spec = pl.BlockSpec((None, d, n, ATT_T_WIDTH), lambda b: (b, 0, 0, 0))
    return pl.pallas_call(
        kern,
        out_shape=jax.ShapeDtypeStruct((B, d, n, ATT_T_WIDTH), F32),
        grid=(B,),
        in_specs=[pl.BlockSpec((None, d, n, GROUP_COLS), lambda b: (b, 0, 0, 0)),
                  pl.BlockSpec(bias.shape, lambda b: (0, 0, 0, 0))],
        out_specs=out_spec,
        compiler_params=_cparams(("arbitrary",), 48),
        name=f"attn{g}",
    )(att, bias)


def _ret_tables(S):
    half = RET_KEY_DIM // 2
    inv_freq = 1.0 / (ROPE_BASE ** (jnp.arange(half, dtype=F32) / half))
    ang = jnp.arange(S).astype(F32)[:, None] * inv_freq[None]
    cos = jnp.cos(ang)
    sin = jnp.sin(ang)
    cos_t = jnp.tile(jnp.concatenate([cos, cos], axis=-1), (1, RET_HEADS))
    sin_t = jnp.tile(jnp.concatenate([-sin, sin], axis=-1), (1, RET_HEADS))
    log_gamma = jnp.log1p(-jnp.exp2(-5.0 - jnp.arange(RET_HEADS, dtype=F32)))
    idx = jnp.arange(RET_CHUNK, dtype=F32)
    rel = idx[:, None] - idx[None, :]
    decay = jnp.where(rel[None] >= 0, jnp.exp(jnp.maximum(rel, 0.0)[None] * log_gamma[:, None, None]), 0.0)
    zeta = jnp.exp((RET_CHUNK - 1 - idx)[None] * log_gamma[:, None])
    xi = jnp.exp((idx + 1)[None] * log_gamma[:, None])
    zeta_t = jnp.repeat(zeta.T, RET_KEY_DIM, axis=1) * (RET_KEY_DIM ** -0.5)
    xi_t = jnp.repeat(xi.T, RET_KEY_DIM, axis=1)
    g_chunk = jnp.exp(RET_CHUNK * log_gamma)
    g_t = jnp.broadcast_to(g_chunk[:, None, None], (RET_HEADS, 1, RET_VAL_DIM))
    return cos_t, sin_t, decay, zeta_t, xi_t, g_t


def _rot_half_swap(t):
    w = t.shape[-1]
    lane = lax.broadcasted_iota(I32, t.shape, 1) % RET_KEY_DIM
    half = RET_KEY_DIM // 2
    return jnp.where(lane < half, pltpu.roll(t, w - half, 1), pltpu.roll(t, half, 1))


RET_ROWS = 1024
RET_PAIR = 4


def _retn_kernel(q_ref, k_ref, v_ref, g_ref, cos_ref, sin_ref, decay_ref, zeta_ref, xi_ref, gch_ref,
                 o_ref, state_ref):
    C = RET_CHUNK

    @pl.when(pl.program_id(1) == 0)
    def _():
        state_ref[...] = jnp.zeros_like(state_ref)

    def prep(c):
        rows = pl.ds(pl.multiple_of(c * C, C), C)
        cos = cos_ref[rows, :]
        sin = sin_ref[rows, :]
        q = q_ref[rows, :].astype(F32)
        k = k_ref[rows, :].astype(F32)
        q = q * cos + _rot_half_swap(q) * sin
        k = k * cos + _rot_half_swap(k) * sin
        qb = q.astype(BF16)
        kb = (k * (RET_KEY_DIM ** -0.5)).astype(BF16)
        q_xi = (q * xi_ref[...]).astype(BF16)
        kz_t = (k * zeta_ref[...]).T
        return rows, qb, kb, q_xi, kz_t

    def group(u, carry):
        chunks = [prep(u * RET_PAIR + i) for i in range(RET_PAIR)]
        ksl = [slice(h * RET_KEY_DIM, (h + 1) * RET_KEY_DIM) for h in range(RET_HEADS)]
        vsl = [slice(h * RET_VAL_DIM, (h + 1) * RET_VAL_DIM) for h in range(RET_HEADS)]
        units = [(c, h) for h in range(RET_HEADS) for c in range(RET_PAIR)]
        scores, kvs = {}, {}
        for c, h in units:
            rows, qb, kb, q_xi, kz_t = chunks[c]
            v = v_ref[rows, vsl[h]]
            scores[c, h] = lax.dot_general(qb[:, ksl[h]], kb[:, ksl[h]], (((1,), (1,)), ((), ())),
                                           preferred_element_type=F32)
            kvs[c, h] = jnp.dot(kz_t[ksl[h], :].astype(BF16), v, preferred_element_type=F32)
        states = {}
        for h in range(RET_HEADS):
            st = state_ref[h]
            for c in range(RET_PAIR):
                states[c, h] = st.astype(BF16)
                st = st * gch_ref[h] + kvs[c, h]
            state_ref[h] = st
        outs = {}
        for c, h in units:
            rows, qb, kb, q_xi, kz_t = chunks[c]
            s = (scores[c, h] * decay_ref[h]).astype(BF16)
            outs[c, h] = (jnp.dot(s, v_ref[rows, vsl[h]], preferred_element_type=F32)
                          + jnp.dot(q_xi[:, ksl[h]], states[c, h], preferred_element_type=F32))
        for c, h in units:
            rows = chunks[c][0]
            o = outs[c, h]
            o = o * lax.rsqrt(jnp.mean(o * o, axis=-1, keepdims=True) + EPS)
            o_ref[rows, vsl[h]] = (_silu(g_ref[rows, vsl[h]].astype(F32)) * o).astype(o_ref.dtype)
        return carry

    lax.fori_loop(0, q_ref.shape[0] // (C * RET_PAIR), group, 0)


def _retention(main3, B, S):
    C = RET_CHUNK
    RS = RET_ROWS
    assert S % RS == 0 and RS % (C * RET_PAIR) == 0
    cos_t, sin_t, decay, zeta_t, xi_t, g_t = _ret_tables(S)
    const2 = lambda b, c: (0, 0)
    const3 = lambda b, c: (0, 0, 0)
    qk0 = 4 * D_MODEL // RET_QK_WIDTH
    out = pl.pallas_call(
        _retn_kernel,
        out_shape=jax.ShapeDtypeStruct((B, S, RET_V_WIDTH), BF16),
        grid=(B, S // RS),
        in_specs=[
            pl.BlockSpec((None, RS, RET_QK_WIDTH), lambda b, c: (b, c, qk0)),
            pl.BlockSpec((None, RS, RET_QK_WIDTH), lambda b, c: (b, c, qk0 + 1)),
            pl.BlockSpec((None, RS, RET_V_WIDTH), lambda b, c: (b, c, 0)),
            pl.BlockSpec((None, RS, RET_V_WIDTH), lambda b, c: (b, c, 1)),
            pl.BlockSpec((RS, RET_QK_WIDTH), lambda b, c: (c, 0)),
            pl.BlockSpec((RS, RET_QK_WIDTH), lambda b, c: (c, 0)),
            pl.BlockSpec((RET_HEADS, C, C), const3),
            pl.BlockSpec((C, RET_QK_WIDTH), const2),
            pl.BlockSpec((C, RET_QK_WIDTH), const2),
            pl.BlockSpec((RET_HEADS, 1, RET_VAL_DIM), const3),
        ],
        out_specs=pl.BlockSpec((None, RS, RET_V_WIDTH), lambda b, c: (b, c, 0)),
        scratch_shapes=[pltpu.VMEM((RET_HEADS, RET_KEY_DIM, RET_VAL_DIM), F32)],
        compiler_params=_cparams(("arbitrary", "arbitrary"), 48),
        name="retn",
    )(main3, main3, main3, main3, cos_t, sin_t, decay, zeta_t, xi_t, g_t)
    return out.reshape(B * S, RET_V_WIDTH)


N_META = 6


def _unstride(ref, scr_ref, slab0, d, tm):
    if d == 1:
        return [ref[0, :, s * LANE:(s + 1) * LANE] for s in range(HEADS_PER_GROUP)]
    for r in range(d):
        for s in range(HEADS_PER_GROUP):
            scr_ref[slab0 + s, pl.ds(r, tm // d, stride=d), :] = ref[r, :, s * LANE:(s + 1) * LANE]
    return [scr_ref[slab0 + s] for s in range(HEADS_PER_GROUP)]


def _postmix_kernel(o0_ref, o1_ref, o2_ref, ret_ref, ga_ref, gr_ref, x_ref,
                    wa_ref, wr_ref, wo_ref, npost_ref, g1_ref, npre_ref, sc2_ref, sh2_ref, *rest, with_router):
    if with_router:
        wrt_ref, brt_ref, tri_ref, xo_ref, h_ref, meta_ref, cnt_ref, scr_ref, carry_ref = rest
    else:
        xo_ref, h_ref, scr_ref = rest
    tm = x_ref.shape[0]
    groups = []
    slab = 0
    for (_, d), o_ref in zip(ATT_GROUPS, (o0_ref, o1_ref, o2_ref)):
        groups.append(_unstride(o_ref, scr_ref, slab, d, tm))
        if d > 1:
            slab += HEADS_PER_GROUP
    head_lane = lax.broadcasted_iota(I32, (tm, LANE), 1)
    slots = []
    for hd in range(HEADS_PER_GROUP):
        outs = [grp[hd] for grp in groups]
        lses = [pltpu.roll(o, LANE // 2, 1) for o in outs]
        mx = functools.reduce(jnp.maximum, lses)
        es = [jnp.exp(l - mx) for l in lses]
        num = sum(e * o for e, o in zip(es, outs))
        slots.append(jnp.where(head_lane < ATT_HEAD_DIM, num / sum(es), 0.0))
    att = jnp.concatenate(slots, axis=1)
    a = jnp.dot(att.astype(BF16), wa_ref[...], preferred_element_type=F32)
    r = jnp.dot(ret_ref[...], wr_ref[...], preferred_element_type=F32)
    merged = jax.nn.sigmoid(ga_ref[...].astype(F32)) * a + jax.nn.sigmoid(gr_ref[...].astype(F32)) * r
    y = jnp.dot(merged.astype(BF16), wo_ref[...], preferred_element_type=F32)
    x = x_ref[...] + g1_ref[...] * _rms(y, npost_ref[...])
    xo_ref[...] = x
    h = _rms(x, npre_ref[...]) * (1.0 + sc2_ref[...]) + sh2_ref[...]
    if not with_router:
        h_ref[...] = h.astype(h_ref.dtype)
        return
    _to_token_tiles(h_ref, h)

    @pl.when(pl.program_id(0) == 0)
    def _():
        carry_ref[...] = jnp.zeros_like(carry_ref)

    h_hi = h.astype(BF16)
    h_lo = (h - h_hi.astype(F32)).astype(BF16)
    w = wrt_ref[...]
    w_hi = w.astype(BF16)
    w_lo = (w - w_hi.astype(F32)).astype(BF16)
    lg = (jnp.dot(h_hi, w_hi, preferred_element_type=F32) + jnp.dot(h_hi, w_lo, preferred_element_type=F32)
          + jnp.dot(h_lo, w_hi, preferred_element_type=F32)) + brt_ref[...]
    lane = lax.broadcasted_iota(I32, lg.shape, 1).astype(F32)
    lg = jnp.where(lane < N_EXPERTS, lg, -jnp.inf)
    m1 = jnp.max(lg, axis=-1, keepdims=True)
    i1 = jnp.min(jnp.where(lg == m1, lane, float(LANE)), axis=-1, keepdims=True)
    lg2 = jnp.where(lane == i1, -jnp.inf, lg)
    m2 = jnp.max(lg2, axis=-1, keepdims=True)
    i2 = jnp.min(jnp.where(lg2 == m2, lane, float(LANE)), axis=-1, keepdims=True)
    t = jnp.exp(m2 - m1)
    w1 = 1.0 / (1.0 + t)
    w2 = t / (1.0 + t)
    oh = jnp.where((lane == i1) | (lane == i2), 1.0, 0.0)
    pos = jnp.dot(tri_ref[...], oh.astype(BF16), preferred_element_type=F32) + carry_ref[...]
    r1 = jnp.sum(jnp.where(lane == i1, pos, 0.0), axis=-1, keepdims=True)
    r2 = jnp.sum(jnp.where(lane == i2, pos, 0.0), axis=-1, keepdims=True)
    carry = carry_ref[...] + jnp.sum(oh, axis=0, keepdims=True)
    carry_ref[...] = carry
    cnt_ref[...] = jnp.broadcast_to(carry, cnt_ref.shape)
    meta = jnp.zeros(lg.shape, F32)
    for j, val in enumerate((i1, i2, r1, r2, w1, w2)):
        meta = jnp.where(lane == j, val, meta)
    meta_ref[...] = meta


def _postmix(outs, ret, main, x2, wa, wr, wo, npost, npre, mod, layer, S, router=None, tm=512):
    R, D = x2.shape
    tpb = S // tm
    row = lambda w: pl.BlockSpec((tm, w), lambda i: (i, 0))
    const = lambda a: pl.BlockSpec(a.shape, lambda i: (0, 0))

    def att_spec(d):
        return pl.BlockSpec((None, d, tm // d, ATT_T_WIDTH), lambda i: (i // tpb, 0, i % tpb, 0))

    specs_att = [att_spec(d) for _, d in ATT_GROUPS]
    in_specs = (specs_att + [
        row(RET_V_WIDTH),
        pl.BlockSpec((tm, D), lambda i: (i, 2)),
        pl.BlockSpec((tm, D), lambda i: (i, 3)),
        row(D), const(wa), const(wr), const(wo), const(npost),
        _mod_spec(layer, 2, tpb), const(npre), _mod_spec(layer, 4, tpb), _mod_spec(layer, 3, tpb)])
    args = list(outs) + [ret, main, main, x2, wa, wr, wo, npost, mod, npre, mod, mod]
    n_slabs = sum(HEADS_PER_GROUP for _, d in ATT_GROUPS if d > 1)
    scratch = [pltpu.VMEM((n_slabs, tm, LANE), F32)]
    if router is None:
        out_shape = [jax.ShapeDtypeStruct((R, D), F32), jax.ShapeDtypeStruct((R, D), BF16)]
        out_specs = [row(D), row(D)]
        sem = ("arbitrary",)
    else:
        tri = jnp.asarray(np.tril(np.ones((tm, tm), np.float32), -1), BF16)
        router = tuple(router) + (tri,)
        in_specs += [const(a) for a in router]
        args += list(router)
        out_shape = [jax.ShapeDtypeStruct((R, D), F32), jax.ShapeDtypeStruct((R * TOK_ROWS, LANE), F32),
                     jax.ShapeDtypeStruct((R, LANE), F32), jax.ShapeDtypeStruct((8, LANE), F32)]
        out_specs = [row(D), pl.BlockSpec((tm * TOK_ROWS, LANE), lambda i: (i, 0)), row(LANE),
                     pl.BlockSpec((8, LANE), lambda i: (0, 0))]
        scratch.append(pltpu.VMEM((1, LANE), F32))
        sem = ("arbitrary",)
    return pl.pallas_call(
        functools.partial(_postmix_kernel, with_router=router is not None),
        out_shape=tuple(out_shape),
        grid=(R // tm,),
        in_specs=in_specs,
        out_specs=tuple(out_specs),
        scratch_shapes=scratch,
        compiler_params=_cparams(sem, 56),
        name="postmix",
    )(*args)


FF_TILE = D_FF // 2


def _swiglu_partial(h, wg_ref, wu_ref, wd_ref):
    act = _silu(jnp.dot(h, wg_ref[...], preferred_element_type=F32)) * jnp.dot(h, wu_ref[...], preferred_element_type=F32)
    return jnp.dot(act.astype(BF16), wd_ref[...], preferred_element_type=F32)


def _ffn_kernel(h_ref, x_ref, wg_ref, wu_ref, wd_ref, npost_ref, g2_ref, o_ref, acc_ref):
    j = pl.program_id(1)

    @pl.when(j == 0)
    def _():
        acc_ref[...] = jnp.zeros_like(acc_ref)

    acc_ref[...] += _swiglu_partial(h_ref[...], wg_ref, wu_ref, wd_ref)

    @pl.when(j == pl.num_programs(1) - 1)
    def _():
        o_ref[...] = x_ref[...] + g2_ref[...] * _rms(acc_ref[...], npost_ref[...])


def _ffn(h2, x2, wg, wu, wd, npost, mod, layer, S, tm=512):
    R, D = x2.shape
    tpb = S // tm
    nf = D_FF // FF_TILE
    return pl.pallas_call(
        _ffn_kernel,
        out_shape=jax.ShapeDtypeStruct((R, D), F32),
        grid=(R // tm, nf),
        in_specs=[
            pl.BlockSpec((tm, D), lambda i, j: (i, 0)),
            pl.BlockSpec((tm, D), lambda i, j: (i, 0)),
            pl.BlockSpec((D, FF_TILE), lambda i, j: (0, j)),
            pl.BlockSpec((D, FF_TILE), lambda i, j: (0, j)),
            pl.BlockSpec((FF_TILE, D), lambda i, j: (j, 0)),
            pl.BlockSpec((1, D), lambda i, j: (0, 0)),
            _mod_spec(layer, 5, tpb),
        ],
        out_specs=pl.BlockSpec((tm, D), lambda i, j: (i, 0)),
        scratch_shapes=[pltpu.VMEM((tm, D), F32)],
        compiler_params=_cparams(("arbitrary", "arbitrary"), 48),
        name="ffn",
    )(h2, x2, wg, wu, wd, npost, mod)


MOE_TM = 512
ROW_UNROLL = 8


def _invperm_kernel(n_ref, dest_ref, src_ref):
    def zero(g, c):
        for u in range(ROW_UNROLL):
            src_ref[g * ROW_UNROLL + u] = 0
        return c

    lax.fori_loop(0, n_ref[0], zero, 0)

    def put(g, c):
        for u in range(ROW_UNROLL):
            a = g * ROW_UNROLL + u
            src_ref[dest_ref[a]] = a
        return c

    lax.fori_loop(0, n_ref[1], put, 0)


def _invperm(dest, P):
    assert P % ROW_UNROLL == 0 and dest.shape[0] % ROW_UNROLL == 0
    trips = jnp.array([P // ROW_UNROLL, dest.shape[0] // ROW_UNROLL], I32)
    return pl.pallas_call(
        _invperm_kernel,
        out_shape=jax.ShapeDtypeStruct((P,), I32),
        in_specs=[pl.BlockSpec(memory_space=pltpu.SMEM), pl.BlockSpec(memory_space=pltpu.SMEM)],
        out_specs=pl.BlockSpec(memory_space=pltpu.SMEM),
        name="invperm",
    )(trips, dest)


TOK_ROWS = D_MODEL // LANE


def _to_token_tiles(ref, val):
    tm = val.shape[0]
    for s in range(TOK_ROWS):
        ref[pl.ds(s, tm, stride=TOK_ROWS), :] = val[:, s * LANE:(s + 1) * LANE]


def _token_tile_slabs(read, tm):
    return [read(pl.ds(s, tm, stride=TOK_ROWS)) for s in range(TOK_ROWS)]


def _expert_kernel(te_ref, nu_ref, src_ref, h_hbm, wg_ref, wu_ref, wd_ref, o_ref, acc_ref, hb_ref, hbuf_ref, sem,
                   *, n_tok):
    j = pl.program_id(0)
    f = pl.program_id(1)
    nu = nu_ref[0]
    tm = acc_ref.shape[0]

    def row_copy(row, slot, i):
        src = h_hbm.at[pl.ds(pl.multiple_of(row * TOK_ROWS, TOK_ROWS), TOK_ROWS), :]
        dst = hbuf_ref.at[slot, pl.ds(pl.multiple_of(i * TOK_ROWS, TOK_ROWS), TOK_ROWS), :]
        return pltpu.make_async_copy(src, dst, sem.at[slot])

    def issue(tile, slot):
        def body(i, c):
            row_copy(src_ref[tile * tm + i] & (n_tok - 1), slot, i).start()
            return c

        lax.fori_loop(0, tm, body, 0, unroll=ROW_UNROLL)

    def wait(slot):
        def body(i, c):
            row_copy(0, slot, 0).wait()
            return c

        lax.fori_loop(0, tm, body, 0, unroll=ROW_UNROLL)

    nf = pl.num_programs(1)
    slot = j % 2
    n_ahead = tm // (D_FF // FF_TILE)

    @pl.when((f == 0) & (j == 0))
    def _():
        issue(0, 0)

    @pl.when((f == 0) & (j <= nu))
    def _():
        wait(slot)

    @pl.when(j < nu)
    def _():
        @pl.when(f == 0)
        def _():
            acc_ref[...] = jnp.zeros_like(acc_ref)
            for s, slab in enumerate(_token_tile_slabs(lambda rows: hbuf_ref[slot, rows, :], tm)):
                hb_ref[:, s * LANE:(s + 1) * LANE] = slab.astype(BF16)

        nxt = jnp.minimum(j + 1, pl.num_programs(0) - 1)
        for i in range(n_ahead):
            r = f * n_ahead + i
            row_copy(src_ref[nxt * tm + r] & (n_tok - 1), 1 - slot, r).start()
        acc_ref[...] += _swiglu_partial(hb_ref[...], wg_ref, wu_ref, wd_ref)

        @pl.when(f == nf - 1)
        def _():
            _to_token_tiles(o_ref, acc_ref[...])

    @pl.when((j == pl.num_programs(0) - 1) & (f == nf - 1) & (j < nu))
    def _():
        wait(1 - slot)

    @pl.when((j >= nu) & (f == pl.num_programs(1) - 1))
    def _():
        o_ref[...] = jnp.zeros_like(o_ref)


def _experts(tile_expert, n_used, src, h2t, wg, wu, wd, n_tiles, tm):
    R, D = h2t.shape[0] // TOK_ROWS, D_MODEL
    assert R & (R - 1) == 0
    nf = D_FF // FF_TILE
    tile = lambda j, nu: jnp.minimum(j, nu[0] - 1)
    ff = lambda j, f, nu: jnp.where(j < nu[0], f, nf - 1)
    return pl.pallas_call(
        functools.partial(_expert_kernel, n_tok=R),
        out_shape=jax.ShapeDtypeStruct((n_tiles * tm * TOK_ROWS, LANE), F32),
        grid_spec=pltpu.PrefetchScalarGridSpec(
            num_scalar_prefetch=3,
            grid=(n_tiles, nf),
            in_specs=[
                pl.BlockSpec(memory_space=pl.ANY),
                pl.BlockSpec((None, D, FF_TILE), lambda j, f, te, nu, src: (te[tile(j, nu)], 0, ff(j, f, nu))),
                pl.BlockSpec((None, D, FF_TILE), lambda j, f, te, nu, src: (te[tile(j, nu)], 0, ff(j, f, nu))),
                pl.BlockSpec((None, FF_TILE, D), lambda j, f, te, nu, src: (te[tile(j, nu)], ff(j, f, nu), 0)),
            ],
            out_specs=pl.BlockSpec((tm * TOK_ROWS, LANE), lambda j, f, te, nu, src: (j, 0)),
            scratch_shapes=[pltpu.VMEM((tm, D), F32), pltpu.VMEM((tm, D), BF16),
                            pltpu.VMEM((2, tm * TOK_ROWS, LANE), F32), pltpu.SemaphoreType.DMA((2,))],
        ),
        compiler_params=_cparams(("arbitrary", "arbitrary"), 48),
        name="moe_experts",
    )(tile_expert, n_used, src, h2t, wg, wu, wd)


def _combine_kernel(d1_ref, d2_ref, y_hbm, meta_ref, x_ref, npost_ref, g2_ref, o_ref, ya_ref, yb_ref, sem):
    tm = x_ref.shape[0]
    t0 = pl.program_id(0) * tm

    def row_copy(row, dst_ref, r):
        src = y_hbm.at[pl.ds(pl.multiple_of(row * TOK_ROWS, TOK_ROWS), TOK_ROWS), :]
        dst = dst_ref.at[pl.ds(pl.multiple_of(r * TOK_ROWS, TOK_ROWS), TOK_ROWS), :]
        return pltpu.make_async_copy(src, dst, sem)

    def issue(r, c):
        row_copy(d1_ref[t0 + r], ya_ref, r).start()
        row_copy(d2_ref[t0 + r], yb_ref, r).start()
        return c

    lax.fori_loop(0, tm, issue, 0, unroll=ROW_UNROLL)

    def wait(r, c):
        row_copy(0, ya_ref, 0).wait()
        row_copy(0, yb_ref, 0).wait()
        return c

    lax.fori_loop(0, tm, wait, 0, unroll=ROW_UNROLL)
    meta = meta_ref[...]
    w1, w2 = meta[:, 4:5], meta[:, 5:6]
    slabs_a = _token_tile_slabs(lambda rows: ya_ref[rows, :], tm)
    slabs_b = _token_tile_slabs(lambda rows: yb_ref[rows, :], tm)
    y = jnp.concatenate([w1 * a + w2 * b for a, b in zip(slabs_a, slabs_b)], axis=1)
    o_ref[...] = x_ref[...] + g2_ref[...] * _rms(y, npost_ref[...])


def _combine(dest1, dest2, sorted_y, meta, x2, npost, mod, layer, S, tm=512):
    R, D = x2.shape
    tpb = S // tm
    return pl.pallas_call(
        _combine_kernel,
        out_shape=jax.ShapeDtypeStruct((R, D), F32),
        grid_spec=pltpu.PrefetchScalarGridSpec(
            num_scalar_prefetch=2,
            grid=(R // tm,),
            in_specs=[
                pl.BlockSpec(memory_space=pl.ANY),
                pl.BlockSpec((tm, LANE), lambda i, *_: (i, 0)),
                pl.BlockSpec((tm, D), lambda i, *_: (i, 0)),
                pl.BlockSpec((1, D), lambda i, *_: (0, 0)),
                _mod_spec(layer, 5, tpb),
            ],
            out_specs=pl.BlockSpec((tm, D), lambda i, *_: (i, 0)),
            scratch_shapes=[pltpu.VMEM((tm * TOK_ROWS, LANE), F32), pltpu.VMEM((tm * TOK_ROWS, LANE), F32),
                            pltpu.SemaphoreType.DMA],
        ),
        compiler_params=_cparams(("arbitrary",), 40),
        name="moe_combine",
    )(dest1, dest2, sorted_y, meta, x2, npost, mod)


def _moe(h2, meta, counts, x2, wg, wu, wd, npost, mod, layer, S):
    R, D = x2.shape
    tm = MOE_TM
    n_tiles = (TOP_K * R + N_EXPERTS * (tm - 1)) // tm + 1
    i1, i2, r1, r2 = (meta[:, j].astype(I32) for j in range(4))
    cnt = counts[0, :N_EXPERTS].astype(I32)
    padded = (cnt + tm - 1) // tm * tm
    end = jnp.cumsum(padded)
    start = end - padded
    eye = jnp.arange(N_EXPERTS, dtype=I32)[None, :]
    dest1 = jnp.sum(jnp.where(i1[:, None] == eye, start[None, :], 0), axis=1) + r1
    dest2 = jnp.sum(jnp.where(i2[:, None] == eye, start[None, :], 0), axis=1) + r2
    tile_row0 = jnp.arange(n_tiles, dtype=I32) * tm
    tile_expert = jnp.minimum(jnp.sum(tile_row0[:, None] >= end[None, :], axis=1), N_EXPERTS - 1).astype(I32)
    n_used = (end[-1:] // tm).astype(I32)
    src = _invperm(jnp.concatenate([dest1, dest2]), n_tiles * tm)
    sorted_y = _experts(tile_expert, n_used, src, h2, wg, wu, wd, n_tiles, tm)
    return _combine(dest1, dest2, sorted_y, meta, x2, npost, mod, layer, S)


def kernel(x, c, w_ada, b_ada, norm_pre_mix, w_in, w_att_proj, w_ret_proj, w_mix_out, norm_post_mix, norm_pre_ffn, w_ffn_gate, w_ffn_up, w_ffn_down, w_router, b_router, w_exp_gate, w_exp_up, w_exp_down, norm_post_ffn):
    B, S, D = x.shape
    R = B * S
    mod = _ada(c, w_ada, b_ada)
    x2 = x.reshape(R, D)
    row = lambda v: v.reshape(1, D)
    for layer in range(DEPTH):
        main, *atts = _inproj(x2, row(norm_pre_mix[layer]), mod, layer, _prep_w_in(w_in[layer]), B, S)
        outs = [_attn_group(atts[g], g) for g in range(len(ATT_GROUPS))]
        ret = _retention(main.reshape(B, S, MAIN_COLS), B, S)
        j = layer // 2
        router = None
        if layer % 2 == 1:
            wrt = jnp.zeros((D, LANE), F32).at[:, :N_EXPERTS].set(w_router[j])
            brt = jnp.zeros((1, LANE), F32).at[0, :N_EXPERTS].set(b_router[j])
            router = (wrt, brt)
        wa = jnp.pad(w_att_proj[layer].reshape(HEADS_PER_GROUP, ATT_HEAD_DIM, D),
                     ((0, 0), (0, LANE - ATT_HEAD_DIM), (0, 0))).reshape(ATT_T_WIDTH, D)
        res = _postmix(outs, ret, main, x2, wa.astype(BF16), w_ret_proj[layer].astype(BF16),
                       w_mix_out[layer].astype(BF16), row(norm_post_mix[layer]), row(norm_pre_ffn[layer]),
                       mod, layer, S, router=router)
        if layer % 2 == 0:
            x2, h2 = res
            x2 = _ffn(h2, x2, w_ffn_gate[j].astype(BF16), w_ffn_up[j].astype(BF16), w_ffn_down[j].astype(BF16),
                      row(norm_post_ffn[layer]), mod, layer, S)
        else:
            x2, h2, meta, counts = res
            x2 = _moe(h2, meta, counts, x2, w_exp_gate[j].astype(BF16), w_exp_up[j].astype(BF16),
                      w_exp_down[j].astype(BF16), row(norm_post_ffn[layer]), mod, layer, S)
    return x2.reshape(B, S, D)
```

```python
import functools

import numpy as np
import jax
import jax.numpy as jnp
from jax import lax
from jax.experimental import pallas as pl
from jax.experimental.pallas import tpu as pltpu

F32 = jnp.float32
BF16 = jnp.bfloat16
I32 = jnp.int32

D_MODEL = 1024
DEPTH = 2
ATT_GROUPS = ((128, 1), (512, 4), (2048, 16))
HEADS_PER_GROUP = 4
N_ATT_HEADS = HEADS_PER_GROUP * len(ATT_GROUPS)
ATT_HEAD_DIM = 64
ATT_WIDTH = N_ATT_HEADS * ATT_HEAD_DIM
ATT_OUT_WIDTH = HEADS_PER_GROUP * ATT_HEAD_DIM
BLOCK = 128
ALIBI_MAX = 8.0
NEG_INF = -1e30
RET_HEADS = 8
RET_KEY_DIM = 64
RET_VAL_DIM = 128
RET_QK_WIDTH = RET_HEADS * RET_KEY_DIM
RET_V_WIDTH = RET_HEADS * RET_VAL_DIM
RET_CHUNK = 128
ROPE_BASE = 10000.0
D_FF = 2816
N_EXPERTS = 8
TOP_K = 2
EPS = 1e-6
IN_SIZES = (ATT_WIDTH, ATT_WIDTH, ATT_WIDTH, RET_QK_WIDTH, RET_QK_WIDTH, RET_V_WIDTH, RET_V_WIDTH, D_MODEL, D_MODEL)
IN_COLS = sum(IN_SIZES)

LANE = 128
MAIN_COLS = 4 * D_MODEL + 2 * RET_QK_WIDTH
GROUP_SRC_COLS = 3 * ATT_OUT_WIDTH
GROUP_SLOTS = 3 * HEADS_PER_GROUP
GROUP_COLS = GROUP_SLOTS * LANE
ATT_T_WIDTH = HEADS_PER_GROUP * LANE
MXU_N = 256

MiB = 1024 * 1024


def _cparams(sem, vmem_mib):
    return pltpu.CompilerParams(dimension_semantics=sem, vmem_limit_bytes=vmem_mib * MiB)


def _rms(x, gain):
    return x * lax.rsqrt(jnp.mean(x * x, axis=-1, keepdims=True) + EPS) * gain


def _silu(x):
    return x * jax.nn.sigmoid(x)


def _ada_kernel(c_ref, w_ref, b_ref, o_ref):
    ca = _silu(c_ref[...])
    o_ref[0] = jnp.dot(ca, w_ref[0], preferred_element_type=F32, precision=lax.Precision.HIGHEST) + b_ref[0]


def _ada(c, w_ada, b_ada):
    B, D = c.shape
    out = pl.pallas_call(
        _ada_kernel,
        out_shape=jax.ShapeDtypeStruct((DEPTH * 6, B, D), F32),
        grid=(DEPTH, 6),
        in_specs=[
            pl.BlockSpec((B, D), lambda l, k: (0, 0)),
            pl.BlockSpec((1, D, D), lambda l, k: (l, 0, k)),
            pl.BlockSpec((1, 1, D), lambda l, k: (l, 0, k)),
        ],
        out_specs=pl.BlockSpec((1, B, D), lambda l, k: (l * 6 + k, 0, 0)),
        compiler_params=_cparams(("arbitrary", "arbitrary"), 32),
        name="ada",
    )(c, w_ada, b_ada.reshape(DEPTH, 1, 6 * D))
    return out.reshape(DEPTH, 6, B, 1, D)


def _mod_spec(layer, k, tiles_per_batch):
    return pl.BlockSpec((None, None, None, 1, D_MODEL),
                        lambda i, *_: (layer, k, i // tiles_per_batch, 0, 0))


def _inproj_kernel(x_ref, gain_ref, sc_ref, sh_ref, w_ref, main_ref, a0_ref, a1_ref, a2_ref, scr_ref):
    tm = x_ref.shape[0]
    h = _rms(x_ref[...], gain_ref[...]) * (1.0 + sc_ref[...]) + sh_ref[...]
    hb = h.astype(BF16)
    for c in range(MAIN_COLS // MXU_N):
        sl = slice(c * MXU_N, (c + 1) * MXU_N)
        main_ref[:, sl] = jnp.dot(hb, w_ref[:, sl], preferred_element_type=F32).astype(BF16)
    att_refs = (a0_ref, a1_ref, a2_ref)
    heads_per_slab = LANE // ATT_HEAD_DIM

    def store_heads(att_ref, r, slot0, val):
        lane = lax.broadcasted_iota(I32, val.shape, 1)
        for k in range(heads_per_slab):
            head = val if k == 0 else pltpu.roll(val, LANE - k * ATT_HEAD_DIM, 1)
            c0 = (slot0 + k) * LANE
            att_ref[r, :, c0:c0 + LANE] = jnp.where(lane < ATT_HEAD_DIM, head, 0.0).astype(BF16)

    slab = 0
    for g, (_, d) in enumerate(ATT_GROUPS):
        for seg in range(3):
            col = MAIN_COLS + g * GROUP_SRC_COLS + seg * ATT_OUT_WIDTH
            res = jnp.dot(hb, w_ref[:, col:col + ATT_OUT_WIDTH], preferred_element_type=F32)
            for s in range(ATT_OUT_WIDTH // LANE):
                part = res[:, s * LANE:(s + 1) * LANE]
                slot0 = seg * HEADS_PER_GROUP + s * heads_per_slab
                if d == 1:
                    store_heads(att_refs[g], 0, slot0, part)
                    continue
                scr_ref[slab] = part
                for r in range(d):
                    store_heads(att_refs[g], r, slot0, scr_ref[slab, pl.ds(r, tm // d, stride=d), :])
                slab += 1


def _inproj(x2, gain, mod, layer, w_in_b, B, S, tm=512):
    R, D = x2.shape
    tpb = S // tm
    n_slabs = sum(3 * (ATT_OUT_WIDTH // LANE) for _, d in ATT_GROUPS if d > 1)

    def att_shape(d):
        return jax.ShapeDtypeStruct((B, d, S // d, GROUP_COLS), BF16)

    def att_spec(d):
        return pl.BlockSpec((None, d, tm // d, GROUP_COLS), lambda i: (i // tpb, 0, i % tpb, 0))

    return pl.pallas_call(
        _inproj_kernel,
        out_shape=(jax.ShapeDtypeStruct((R, MAIN_COLS), BF16),) + tuple(att_shape(d) for _, d in ATT_GROUPS),
        grid=(R // tm,),
        in_specs=[
            pl.BlockSpec((tm, D), lambda i: (i, 0)),
            pl.BlockSpec((1, D), lambda i: (0, 0)),
            _mod_spec(layer, 1, tpb),
            _mod_spec(layer, 0, tpb),
            pl.BlockSpec((D, IN_COLS), lambda i: (0, 0), pipeline_mode=pl.Buffered(1)),
        ],
        out_specs=(pl.BlockSpec((tm, MAIN_COLS), lambda i: (i, 0)),) + tuple(att_spec(d) for _, d in ATT_GROUPS),
        scratch_shapes=[pltpu.VMEM((n_slabs, tm, LANE), F32)],
        compiler_params=_cparams(("arbitrary",), 56),
        name="inproj",
    )(x2, gain, mod, mod, w_in_b)


def _prep_w_in(w):
    b = np.concatenate([[0], np.cumsum(IN_SIZES)])
    seg = lambda s: w[:, b[s]:b[s + 1]]
    cols = [seg(5), seg(6), seg(7), seg(8), seg(3), seg(4)]
    q_a, k_a, v_a = seg(0) * (ATT_HEAD_DIM ** -0.5), seg(1), seg(2)
    for g in range(len(ATT_GROUPS)):
        gs = slice(g * ATT_OUT_WIDTH, (g + 1) * ATT_OUT_WIDTH)
        cols += [q_a[:, gs], k_a[:, gs], v_a[:, gs]]
    return jnp.concatenate(cols, axis=1).astype(BF16)


def _alibi_slopes():
    return [2.0 ** (-ALIBI_MAX * h / N_ATT_HEADS) for h in range(1, N_ATT_HEADS + 1)]


ATT_UNITS = 4


def _attn_kernel(a_ref, bias_ref, o_ref, *, n, dilation):
    nb = n // BLOCK
    win = min(2 * BLOCK, n)
    ones = jnp.ones((win, LANE), BF16)
    lane = lax.broadcasted_iota(I32, (BLOCK, LANE), 1)

    def head_units(u):
        r = u // nb
        i = u % nb
        q0 = pl.multiple_of(i * BLOCK, BLOCK)
        k0 = pl.multiple_of(jnp.clip((i - 1) * BLOCK, 0, n - win), BLOCK)
        t = jnp.minimum(i, 1)
        return [(r, pl.ds(q0, BLOCK), pl.ds(k0, win), t, h) for h in range(HEADS_PER_GROUP)]

    def pair(u2, carry):
        units = sum((head_units(ATT_UNITS * u2 + j) for j in range(ATT_UNITS)), [])
        scores = []
        for r, qrows, krows, t, h in units:
            qs = slice(h * LANE, (h + 1) * LANE)
            ks = slice((HEADS_PER_GROUP + h) * LANE, (HEADS_PER_GROUP + h + 1) * LANE)
            s = lax.dot_general(a_ref[r, qrows, qs], a_ref[r, krows, ks], (((1,), (1,)), ((), ())),
                                preferred_element_type=F32)
            scores.append(s + bias_ref[t, h])
        probs = []
        for s in scores:
            m = jnp.max(s, axis=-1, keepdims=True)
            probs.append((jnp.exp(s - m).astype(BF16), m))
        for (r, qrows, krows, t, h), (p, m) in zip(units, probs):
            qs = slice(h * LANE, (h + 1) * LANE)
            vs = slice((2 * HEADS_PER_GROUP + h) * LANE, (2 * HEADS_PER_GROUP + h + 1) * LANE)
            pv = jnp.dot(p, jnp.concatenate([a_ref[r, krows, vs], ones], axis=1), preferred_element_type=F32)
            den = pv[:, LANE:]
            o_ref[r, qrows, qs] = jnp.where(lane < ATT_HEAD_DIM, pv[:, :LANE] / den, m + jnp.log(den))
        return carry

    lax.fori_loop(0, dilation * nb // ATT_UNITS, pair, 0)


def _attn_bias(n, n_w, d, slopes):
    win = min(2 * BLOCK, n)
    r = np.arange(BLOCK)[:, None]
    c = np.arange(win)[None, :]
    tables = []
    for shift in (0, win - BLOCK):
        dist = shift + r - c
        valid = (dist >= 0) & (dist <= n_w)
        tables.append(np.stack([np.where(valid, -s * (d * dist), NEG_INF) for s in slopes]))
    return jnp.asarray(np.stack(tables), F32)


def _attn_group(att, g):
    window, d = ATT_GROUPS[g]
    B, _, n, _ = att.shape
    assert n % BLOCK == 0 and (d * (n // BLOCK)) % ATT_UNITS == 0
    slopes = _alibi_slopes()[g * HEADS_PER_GROUP:(g + 1) * HEADS_PER_GROUP]
    bias = _attn_bias(n, window // d, d, slopes)
    kern = functools.partial(_attn_kernel, n=n, dilation=d)
    out_spec = pl.BlockSpec((None, d, n, ATT_T_WIDTH), lambda b: (b, 0, 0, 0))
    return pl.pallas_call(
        kern,
        out_shape=jax.ShapeDtypeStruct((B, d, n, ATT_T_WIDTH), F32),
        grid=(B,),
        in_specs=[pl.BlockSpec((None, d, n, GROUP_COLS), lambda b: (b, 0, 0, 0)),
                  pl.BlockSpec(bias.shape, lambda b: (0, 0, 0, 0))],
        out_specs=out_spec,
        compiler_params=_cparams(("arbitrary",), 48),
        name=f"attn{g}",
    )(att, bias)


def _ret_tables(S):
    half = RET_KEY_DIM // 2
    inv_freq = 1.0 / (ROPE_BASE ** (jnp.arange(half, dtype=F32) / half))
    ang = jnp.arange(S).astype(F32)[:, None] * inv_freq[None]
    cos = jnp.cos(ang)
    sin = jnp.sin(ang)
    cos_t = jnp.tile(jnp.concatenate([cos, cos], axis=-1), (1, RET_HEADS))
    sin_t = jnp.tile(jnp.concatenate([-sin, sin], axis=-1), (1, RET_HEADS))
    log_gamma = jnp.log1p(-jnp.exp2(-5.0 - jnp.arange(RET_HEADS, dtype=F32)))
    idx = jnp.arange(RET_CHUNK, dtype=F32)
    rel = idx[:, None] - idx[None, :]
    decay = jnp.where(rel[None] >= 0, jnp.exp(jnp.maximum(rel, 0.0)[None] * log_gamma[:, None, None]), 0.0)
    zeta = jnp.exp((RET_CHUNK - 1 - idx)[None] * log_gamma[:, None])
    xi = jnp.exp((idx + 1)[None] * log_gamma[:, None])
    zeta_t = jnp.repeat(zeta.T, RET_KEY_DIM, axis=1) * (RET_KEY_DIM ** -0.5)
    xi_t = jnp.repeat(xi.T, RET_KEY_DIM, axis=1)
    g_chunk = jnp.exp(RET_CHUNK * log_gamma)
    g_t = jnp.broadcast_to(g_chunk[:, None, None], (RET_HEADS, 1, RET_VAL_DIM))
    return cos_t, sin_t, decay, zeta_t, xi_t, g_t


def _rot_half_swap(t):
    w = t.shape[-1]
    lane = lax.broadcasted_iota(I32, t.shape, 1) % RET_KEY_DIM
    half = RET_KEY_DIM // 2
    return jnp.where(lane < half, pltpu.roll(t, w - half, 1), pltpu.roll(t, half, 1))


RET_ROWS = 1024
RET_PAIR = 4


def _retn_kernel(q_ref, k_ref, v_ref, g_ref, cos_ref, sin_ref, decay_ref, zeta_ref, xi_ref, gch_ref,
                 o_ref, state_ref):
    C = RET_CHUNK

    @pl.when(pl.program_id(1) == 0)
    def _():
        state_ref[...] = jnp.zeros_like(state_ref)

    def prep(c):
        rows = pl.ds(pl.multiple_of(c * C, C), C)
        cos = cos_ref[rows, :]
        sin = sin_ref[rows, :]
        q = q_ref[rows, :].astype(F32)
        k = k_ref[rows, :].astype(F32)
        q = q * cos + _rot_half_swap(q) * sin
        k = k * cos + _rot_half_swap(k) * sin
        qb = q.astype(BF16)
        kb = (k * (RET_KEY_DIM ** -0.5)).astype(BF16)
        q_xi = (q * xi_ref[...]).astype(BF16)
        kz_t = (k * zeta_ref[...]).T
        return rows, qb, kb, q_xi, kz_t

    def group(u, carry):
        chunks = [prep(u * RET_PAIR + i) for i in range(RET_PAIR)]
        ksl = [slice(h * RET_KEY_DIM, (h + 1) * RET_KEY_DIM) for h in range(RET_HEADS)]
        vsl = [slice(h * RET_VAL_DIM, (h + 1) * RET_VAL_DIM) for h in range(RET_HEADS)]
        units = [(c, h) for h in range(RET_HEADS) for c in range(RET_PAIR)]
        scores, kvs = {}, {}
        for c, h in units:
            rows, qb, kb, q_xi, kz_t = chunks[c]
            v = v_ref[rows, vsl[h]]
            scores[c, h] = lax.dot_general(qb[:, ksl[h]], kb[:, ksl[h]], (((1,), (1,)), ((), ())),
                                           preferred_element_type=F32)
            kvs[c, h] = jnp.dot(kz_t[ksl[h], :].astype(BF16), v, preferred_element_type=F32)
        states = {}
        for h in range(RET_HEADS):
            st = state_ref[h]
            for c in range(RET_PAIR):
                states[c, h] = st.astype(BF16)
                st = st * gch_ref[h] + kvs[c, h]
            state_ref[h] = st
        outs = {}
        for c, h in units:
            rows, qb, kb, q_xi, kz_t = chunks[c]
            s = (scores[c, h] * decay_ref[h]).astype(BF16)
            outs[c, h] = (jnp.dot(s, v_ref[rows, vsl[h]], preferred_element_type=F32)
                          + jnp.dot(q_xi[:, ksl[h]], states[c, h], preferred_element_type=F32))
        for c, h in units:
            rows = chunks[c][0]
            o = outs[c, h]
            o = o * lax.rsqrt(jnp.mean(o * o, axis=-1, keepdims=True) + EPS)
            o_ref[rows, vsl[h]] = (_silu(g_ref[rows, vsl[h]].astype(F32)) * o).astype(o_ref.dtype)
        return carry

    lax.fori_loop(0, q_ref.shape[0] // (C * RET_PAIR), group, 0)


def _retention(main3, B, S):
    C = RET_CHUNK
    RS = RET_ROWS
    assert S % RS == 0 and RS % (C * RET_PAIR) == 0
    cos_t, sin_t, decay, zeta_t, xi_t, g_t = _ret_tables(S)
    const2 = lambda b, c: (0, 0)
    const3 = lambda b, c: (0, 0, 0)
    qk0 = 4 * D_MODEL // RET_QK_WIDTH
    out = pl.pallas_call(
        _retn_kernel,
        out_shape=jax.ShapeDtypeStruct((B, S, RET_V_WIDTH), BF16),
        grid=(B, S // RS),
        in_specs=[
            pl.BlockSpec((None, RS, RET_QK_WIDTH), lambda b, c: (b, c, qk0)),
            pl.BlockSpec((None, RS, RET_QK_WIDTH), lambda b, c: (b, c, qk0 + 1)),
            pl.BlockSpec((None, RS, RET_V_WIDTH), lambda b, c: (b, c, 0)),
            pl.BlockSpec((None, RS, RET_V_WIDTH), lambda b, c: (b, c, 1)),
            pl.BlockSpec((RS, RET_QK_WIDTH), lambda b, c: (c, 0)),
            pl.BlockSpec((RS, RET_QK_WIDTH), lambda b, c: (c, 0)),
            pl.BlockSpec((RET_HEADS, C, C), const3),
            pl.BlockSpec((C, RET_QK_WIDTH), const2),
            pl.BlockSpec((C, RET_QK_WIDTH), const2),
            pl.BlockSpec((RET_HEADS, 1, RET_VAL_DIM), const3),
        ],
        out_specs=pl.BlockSpec((None, RS, RET_V_WIDTH), lambda b, c: (b, c, 0)),
        scratch_shapes=[pltpu.VMEM((RET_HEADS, RET_KEY_DIM, RET_VAL_DIM), F32)],
        compiler_params=_cparams(("arbitrary", "arbitrary"), 48),
        name="retn",
    )(main3, main3, main3, main3, cos_t, sin_t, decay, zeta_t, xi_t, g_t)
    return out.reshape(B * S, RET_V_WIDTH)


N_META = 6


def _unstride(ref, scr_ref, slab0, d, tm):
    if d == 1:
        return [ref[0, :, s * LANE:(s + 1) * LANE] for s in range(HEADS_PER_GROUP)]
    for r in range(d):
        for s in range(HEADS_PER_GROUP):
            scr_ref[slab0 + s, pl.ds(r, tm // d, stride=d), :] = ref[r, :, s * LANE:(s + 1) * LANE]
    return [scr_ref[slab0 + s] for s in range(HEADS_PER_GROUP)]


def _postmix_kernel(o0_ref, o1_ref, o2_ref, ret_ref, ga_ref, gr_ref, x_ref,
                    wa_ref, wr_ref, wo_ref, npost_ref, g1_ref, npre_ref, sc2_ref, sh2_ref, *rest, with_router):
    if with_router:
        wrt_ref, brt_ref, tri_ref, xo_ref, h_ref, meta_ref, cnt_ref, scr_ref, carry_ref = rest
    else:
        xo_ref, h_ref, scr_ref = rest
    tm = x_ref.shape[0]
    groups = []
    slab = 0
    for (_, d), o_ref in zip(ATT_GROUPS, (o0_ref, o1_ref, o2_ref)):
        groups.append(_unstride(o_ref, scr_ref, slab, d, tm))
        if d > 1:
            slab += HEADS_PER_GROUP
    head_lane = lax.broadcasted_iota(I32, (tm, LANE), 1)
    slots = []
    for hd in range(HEADS_PER_GROUP):
        outs = [grp[hd] for grp in groups]
        lses = [pltpu.roll(o, LANE // 2, 1) for o in outs]
        mx = functools.reduce(jnp.maximum, lses)
        es = [jnp.exp(l - mx) for l in lses]
        num = sum(e * o for e, o in zip(es, outs))
        slots.append(jnp.where(head_lane < ATT_HEAD_DIM, num / sum(es), 0.0))
    att = jnp.concatenate(slots, axis=1)
    a = jnp.dot(att.astype(BF16), wa_ref[...], preferred_element_type=F32)
    r = jnp.dot(ret_ref[...], wr_ref[...], preferred_element_type=F32)
    merged = jax.nn.sigmoid(ga_ref[...].astype(F32)) * a + jax.nn.sigmoid(gr_ref[...].astype(F32)) * r
    y = jnp.dot(merged.astype(BF16), wo_ref[...], preferred_element_type=F32)
    x = x_ref[...] + g1_ref[...] * _rms(y, npost_ref[...])
    xo_ref[...] = x
    h = _rms(x, npre_ref[...]) * (1.0 + sc2_ref[...]) + sh2_ref[...]
    if not with_router:
        h_ref[...] = h.astype(h_ref.dtype)
        return
    _to_token_tiles(h_ref, h)

    @pl.when(pl.program_id(0) == 0)
    def _():
        carry_ref[...] = jnp.zeros_like(carry_ref)

    h_hi = h.astype(BF16)
    h_lo = (h - h_hi.astype(F32)).astype(BF16)
    w = wrt_ref[...]
    w_hi = w.astype(BF16)
    w_lo = (w - w_hi.astype(F32)).astype(BF16)
    lg = (jnp.dot(h_hi, w_hi, preferred_element_type=F32) + jnp.dot(h_hi, w_lo, preferred_element_type=F32)
          + jnp.dot(h_lo, w_hi, preferred_element_type=F32)) + brt_ref[...]
    lane = lax.broadcasted_iota(I32, lg.shape, 1).astype(F32)
    lg = jnp.where(lane < N_EXPERTS, lg, -jnp.inf)
    m1 = jnp.max(lg, axis=-1, keepdims=True)
    i1 = jnp.min(jnp.where(lg == m1, lane, float(LANE)), axis=-1, keepdims=True)
    lg2 = jnp.where(lane == i1, -jnp.inf, lg)
    m2 = jnp.max(lg2, axis=-1, keepdims=True)
    i2 = jnp.min(jnp.where(lg2 == m2, lane, float(LANE)), axis=-1, keepdims=True)
    t = jnp.exp(m2 - m1)
    w1 = 1.0 / (1.0 + t)
    w2 = t / (1.0 + t)
    oh = jnp.where((lane == i1) | (lane == i2), 1.0, 0.0)
    pos = jnp.dot(tri_ref[...], oh.astype(BF16), preferred_element_type=F32) + carry_ref[...]
    r1 = jnp.sum(jnp.where(lane == i1, pos, 0.0), axis=-1, keepdims=True)
    r2 = jnp.sum(jnp.where(lane == i2, pos, 0.0), axis=-1, keepdims=True)
    carry = carry_ref[...] + jnp.sum(oh, axis=0, keepdims=True)
    carry_ref[...] = carry
    cnt_ref[...] = jnp.broadcast_to(carry, cnt_ref.shape)
    meta = jnp.zeros(lg.shape, F32)
    for j, val in enumerate((i1, i2, r1, r2, w1, w2)):
        meta = jnp.where(lane == j, val, meta)
    meta_ref[...] = meta


def _postmix(outs, ret, main, x2, wa, wr, wo, npost, npre, mod, layer, S, router=None, tm=512):
    R, D = x2.shape
    tpb = S // tm
    row = lambda w: pl.BlockSpec((tm, w), lambda i: (i, 0))
    const = lambda a: pl.BlockSpec(a.shape, lambda i: (0, 0))

    def att_spec(d):
        return pl.BlockSpec((None, d, tm // d, ATT_T_WIDTH), lambda i: (i // tpb, 0, i % tpb, 0))

    specs_att = [att_spec(d) for _, d in ATT_GROUPS]
    in_specs = (specs_att + [
        row(RET_V_WIDTH),
        pl.BlockSpec((tm, D), lambda i: (i, 2)),
        pl.BlockSpec((tm, D), lambda i: (i, 3)),
        row(D), const(wa), const(wr), const(wo), const(npost),
        _mod_spec(layer, 2, tpb), const(npre), _mod_spec(layer, 4, tpb), _mod_spec(layer, 3, tpb)])
    args = list(outs) + [ret, main, main, x2, wa, wr, wo, npost, mod, npre, mod, mod]
    n_slabs = sum(HEADS_PER_GROUP for _, d in ATT_GROUPS if d > 1)
    scratch = [pltpu.VMEM((n_slabs, tm, LANE), F32)]
    if router is None:
        out_shape = [jax.ShapeDtypeStruct((R, D), F32), jax.ShapeDtypeStruct((R, D), BF16)]
        out_specs = [row(D), row(D)]
        sem = ("arbitrary",)
    else:
        tri = jnp.asarray(np.tril(np.ones((tm, tm), np.float32), -1), BF16)
        router = tuple(router) + (tri,)
        in_specs += [const(a) for a in router]
        args += list(router)
        out_shape = [jax.ShapeDtypeStruct((R, D), F32), jax.ShapeDtypeStruct((R * TOK_ROWS, LANE), F32),
                     jax.ShapeDtypeStruct((R, LANE), F32), jax.ShapeDtypeStruct((8, LANE), F32)]
        out_specs = [row(D), pl.BlockSpec((tm * TOK_ROWS, LANE), lambda i: (i, 0)), row(LANE),
                     pl.BlockSpec((8, LANE), lambda i: (0, 0))]
        scratch.append(pltpu.VMEM((1, LANE), F32))
        sem = ("arbitrary",)
    return pl.pallas_call(
        functools.partial(_postmix_kernel, with_router=router is not None),
        out_shape=tuple(out_shape),
        grid=(R // tm,),
        in_specs=in_specs,
        out_specs=tuple(out_specs),
        scratch_shapes=scratch,
        compiler_params=_cparams(sem, 56),
        name="postmix",
    )(*args)


FF_TILE = D_FF // 2


def _swiglu_partial(h, wg_ref, wu_ref, wd_ref):
    act = _silu(jnp.dot(h, wg_ref[...], preferred_element_type=F32)) * jnp.dot(h, wu_ref[...], preferred_element_type=F32)
    return jnp.dot(act.astype(BF16), wd_ref[...], preferred_element_type=F32)


def _ffn_kernel(h_ref, x_ref, wg_ref, wu_ref, wd_ref, npost_ref, g2_ref, o_ref, acc_ref):
    j = pl.program_id(1)

    @pl.when(j == 0)
    def _():
        acc_ref[...] = jnp.zeros_like(acc_ref)

    acc_ref[...] += _swiglu_partial(h_ref[...], wg_ref, wu_ref, wd_ref)

    @pl.when(j == pl.num_programs(1) - 1)
    def _():
        o_ref[...] = x_ref[...] + g2_ref[...] * _rms(acc_ref[...], npost_ref[...])


def _ffn(h2, x2, wg, wu, wd, npost, mod, layer, S, tm=512):
    R, D = x2.shape
    tpb = S // tm
    nf = D_FF // FF_TILE
    return pl.pallas_call(
        _ffn_kernel,
        out_shape=jax.ShapeDtypeStruct((R, D), F32),
        grid=(R // tm, nf),
        in_specs=[
            pl.BlockSpec((tm, D), lambda i, j: (i, 0)),
            pl.BlockSpec((tm, D), lambda i, j: (i, 0)),
            pl.BlockSpec((D, FF_TILE), lambda i, j: (0, j)),
            pl.BlockSpec((D, FF_TILE), lambda i, j: (0, j)),
            pl.BlockSpec((FF_TILE, D), lambda i, j: (j, 0)),
            pl.BlockSpec((1, D), lambda i, j: (0, 0)),
            _mod_spec(layer, 5, tpb),
        ],
        out_specs=pl.BlockSpec((tm, D), lambda i, j: (i, 0)),
        scratch_shapes=[pltpu.VMEM((tm, D), F32)],
        compiler_params=_cparams(("arbitrary", "arbitrary"), 48),
        name="ffn",
    )(h2, x2, wg, wu, wd, npost, mod)


MOE_TM = 512
ROW_UNROLL = 8
DMA_QUEUES = 2


def _invperm_kernel(n_ref, dest_ref, src_ref, zeros_ref, sem):
    zeros_ref[...] = jnp.zeros_like(zeros_ref)
    fill = pltpu.make_async_copy(zeros_ref, src_ref, sem)
    fill.start()
    fill.wait()

    def put(g, c):
        for u in range(ROW_UNROLL):
            a = g * ROW_UNROLL + u
            src_ref[dest_ref[a]] = a
        return c

    lax.fori_loop(0, n_ref[0], put, 0)


def _invperm(dest, P):
    assert dest.shape[0] % ROW_UNROLL == 0
    trips = jnp.array([dest.shape[0] // ROW_UNROLL], I32)
    return pl.pallas_call(
        _invperm_kernel,
        out_shape=jax.ShapeDtypeStruct((P,), I32),
        in_specs=[pl.BlockSpec(memory_space=pltpu.SMEM), pl.BlockSpec(memory_space=pltpu.SMEM)],
        out_specs=pl.BlockSpec(memory_space=pltpu.SMEM),
        scratch_shapes=[pltpu.VMEM((P,), I32), pltpu.SemaphoreType.DMA],
        name="invperm",
    )(trips, dest)


TOK_ROWS = D_MODEL // LANE


def _to_token_tiles(ref, val):
    tm = val.shape[0]
    for s in range(TOK_ROWS):
        ref[pl.ds(s, tm, stride=TOK_ROWS), :] = val[:, s * LANE:(s + 1) * LANE]


def _token_tile_slabs(read, tm):
    return [read(pl.ds(s, tm, stride=TOK_ROWS)) for s in range(TOK_ROWS)]


def _expert_kernel(te_ref, nu_ref, src_ref, h_hbm, wg_ref, wu_ref, wd_ref, o_ref, acc_ref, hb_ref, hbuf_ref, sem,
                   *, n_tok):
    j = pl.program_id(0)
    f = pl.program_id(1)
    nu = nu_ref[0]
    tm = acc_ref.shape[0]

    def row_copy(row, slot, i):
        src = h_hbm.at[pl.ds(pl.multiple_of(row * TOK_ROWS, TOK_ROWS), TOK_ROWS), :]
        dst = hbuf_ref.at[slot, pl.ds(pl.multiple_of(i * TOK_ROWS, TOK_ROWS), TOK_ROWS), :]
        return pltpu.make_async_copy(src, dst, sem.at[slot])

    def issue(tile, slot):
        def body(g, c):
            for k in range(DMA_QUEUES):
                i = g * DMA_QUEUES + k
                row_copy(src_ref[tile * tm + i] & (n_tok - 1), slot, i).start(priority=k)
            return c

        lax.fori_loop(0, tm // DMA_QUEUES, body, 0, unroll=ROW_UNROLL // DMA_QUEUES)

    def wait(slot):
        def body(i, c):
            row_copy(0, slot, 0).wait()
            return c

        lax.fori_loop(0, tm, body, 0, unroll=ROW_UNROLL)

    nf = pl.num_programs(1)
    slot = j % 2

    @pl.when((f == 0) & (j == 0))
    def _():
        issue(0, 0)

    @pl.when((f == 0) & (j <= nu))
    def _():
        wait(slot)

    @pl.when(j < nu)
    def _():
        @pl.when(f == 0)
        def _():
            acc_ref[...] = jnp.zeros_like(acc_ref)
            for s, slab in enumerate(_token_tile_slabs(lambda rows: hbuf_ref[slot, rows, :], tm)):
                hb_ref[:, s * LANE:(s + 1) * LANE] = slab.astype(BF16)
            issue(jnp.minimum(j + 1, pl.num_programs(0) - 1), 1 - slot)

        acc_ref[...] += _swiglu_partial(hb_ref[...], wg_ref, wu_ref, wd_ref)

        @pl.when(f == nf - 1)
        def _():
            _to_token_tiles(o_ref, acc_ref[...])

    @pl.when((j == pl.num_programs(0) - 1) & (f == nf - 1) & (j < nu))
    def _():
        wait(1 - slot)

    @pl.when((j >= nu) & (f == pl.num_programs(1) - 1))
    def _():
        o_ref[...] = jnp.zeros_like(o_ref)


def _experts(tile_expert, n_used, src, h2t, wg, wu, wd, n_tiles, tm):
    R, D = h2t.shape[0] // TOK_ROWS, D_MODEL
    assert R & (R - 1) == 0
    nf = D_FF // FF_TILE
    tile = lambda j, nu: jnp.minimum(j, nu[0] - 1)
    ff = lambda j, f, nu: jnp.where(j < nu[0], f, nf - 1)
    return pl.pallas_call(
        functools.partial(_expert_kernel, n_tok=R),
        out_shape=jax.ShapeDtypeStruct((n_tiles * tm * TOK_ROWS, LANE), F32),
        grid_spec=pltpu.PrefetchScalarGridSpec(
            num_scalar_prefetch=3,
            grid=(n_tiles, nf),
            in_specs=[
                pl.BlockSpec(memory_space=pl.ANY),
                pl.BlockSpec((None, D, FF_TILE), lambda j, f, te, nu, src: (te[tile(j, nu)], 0, ff(j, f, nu))),
                pl.BlockSpec((None, D, FF_TILE), lambda j, f, te, nu, src: (te[tile(j, nu)], 0, ff(j, f, nu))),
                pl.BlockSpec((None, FF_TILE, D), lambda j, f, te, nu, src: (te[tile(j, nu)], ff(j, f, nu), 0)),
            ],
            out_specs=pl.BlockSpec((tm * TOK_ROWS, LANE), lambda j, f, te, nu, src: (j, 0)),
            scratch_shapes=[pltpu.VMEM((tm, D), F32), pltpu.VMEM((tm, D), BF16),
                            pltpu.VMEM((2, tm * TOK_ROWS, LANE), F32), pltpu.SemaphoreType.DMA((2,))],
        ),
        compiler_params=_cparams(("arbitrary", "arbitrary"), 48),
        name="moe_experts",
    )(tile_expert, n_used, src, h2t, wg, wu, wd)


def _combine_kernel(d1_ref, d2_ref, y_hbm, meta_ref, x_ref, npost_ref, g2_ref, o_ref, ya_ref, yb_ref, sem):
    tm = x_ref.shape[0]
    step = pl.program_id(0)
    slot = step % 2

    def row_copy(row, dst_ref, s, r):
        src = y_hbm.at[pl.ds(pl.multiple_of(row * TOK_ROWS, TOK_ROWS), TOK_ROWS), :]
        dst = dst_ref.at[s, pl.ds(pl.multiple_of(r * TOK_ROWS, TOK_ROWS), TOK_ROWS), :]
        return pltpu.make_async_copy(src, dst, sem.at[s])

    def issue(tile, s):
        def body(r, c):
            t = tile * tm + r
            row_copy(d1_ref[t], ya_ref, s, r).start(priority=0)
            row_copy(d2_ref[t], yb_ref, s, r).start(priority=1)
            return c

        lax.fori_loop(0, tm, body, 0, unroll=ROW_UNROLL)

    @pl.when(step == 0)
    def _():
        issue(0, 0)

    @pl.when(step + 1 < pl.num_programs(0))
    def _():
        issue(step + 1, 1 - slot)

    def wait(r, c):
        row_copy(0, ya_ref, slot, 0).wait()
        row_copy(0, yb_ref, slot, 0).wait()
        return c

    lax.fori_loop(0, tm, wait, 0, unroll=ROW_UNROLL)
    meta = meta_ref[...]
    w1, w2 = meta[:, 4:5], meta[:, 5:6]
    slabs_a = _token_tile_slabs(lambda rows: ya_ref[slot, rows, :], tm)
    slabs_b = _token_tile_slabs(lambda rows: yb_ref[slot, rows, :], tm)
    y = jnp.concatenate([w1 * a + w2 * b for a, b in zip(slabs_a, slabs_b)], axis=1)
    o_ref[...] = x_ref[...] + g2_ref[...] * _rms(y, npost_ref[...])


def _combine(dest1, dest2, sorted_y, meta, x2, npost, mod, layer, S, tm=512):
    R, D = x2.shape
    tpb = S // tm
    return pl.pallas_call(
        _combine_kernel,
        out_shape=jax.ShapeDtypeStruct((R, D), F32),
        grid_spec=pltpu.PrefetchScalarGridSpec(
            num_scalar_prefetch=2,
            grid=(R // tm,),
            in_specs=[
                pl.BlockSpec(memory_space=pl.ANY),
                pl.BlockSpec((tm, LANE), lambda i, *_: (i, 0)),
                pl.BlockSpec((tm, D), lambda i, *_: (i, 0)),
                pl.BlockSpec((1, D), lambda i, *_: (0, 0)),
                _mod_spec(layer, 5, tpb),
            ],
            out_specs=pl.BlockSpec((tm, D), lambda i, *_: (i, 0)),
            scratch_shapes=[pltpu.VMEM((2, tm * TOK_ROWS, LANE), F32), pltpu.VMEM((2, tm * TOK_ROWS, LANE), F32),
                            pltpu.SemaphoreType.DMA((2,))],
        ),
        compiler_params=_cparams(("arbitrary",), 40),
        name="moe_combine",
    )(dest1, dest2, sorted_y, meta, x2, npost, mod)


def _moe(h2, meta, counts, x2, wg, wu, wd, npost, mod, layer, S):
    R, D = x2.shape
    tm = MOE_TM
    n_tiles = (TOP_K * R + N_EXPERTS * (tm - 1)) // tm + 1
    i1, i2, r1, r2 = (meta[:, j].astype(I32) for j in range(4))
    cnt = counts[0, :N_EXPERTS].astype(I32)
    padded = (cnt + tm - 1) // tm * tm
    end = jnp.cumsum(padded)
    start = end - padded
    eye = jnp.arange(N_EXPERTS, dtype=I32)[None, :]
    dest1 = jnp.sum(jnp.where(i1[:, None] == eye, start[None, :], 0), axis=1) + r1
    dest2 = jnp.sum(jnp.where(i2[:, None] == eye, start[None, :], 0), axis=1) + r2
    tile_row0 = jnp.arange(n_tiles, dtype=I32) * tm
    tile_expert = jnp.minimum(jnp.sum(tile_row0[:, None] >= end[None, :], axis=1), N_EXPERTS - 1).astype(I32)
    n_used = (end[-1:] // tm).astype(I32)
    src = _invperm(jnp.concatenate([dest1, dest2]), n_tiles * tm)
    sorted_y = _experts(tile_expert, n_used, src, h2, wg, wu, wd, n_tiles, tm)
    return _combine(dest1, dest2, sorted_y, meta, x2, npost, mod, layer, S)


def kernel(x, c, w_ada, b_ada, norm_pre_mix, w_in, w_att_proj, w_ret_proj, w_mix_out, norm_post_mix, norm_pre_ffn, w_ffn_gate, w_ffn_up, w_ffn_down, w_router, b_router, w_exp_gate, w_exp_up, w_exp_down, norm_post_ffn):
    B, S, D = x.shape
    R = B * S
    mod = _ada(c, w_ada, b_ada)
    x2 = x.reshape(R, D)
    row = lambda v: v.reshape(1, D)
    for layer in range(DEPTH):
        main, *atts = _inproj(x2, row(norm_pre_mix[layer]), mod, layer, _prep_w_in(w_in[layer]), B, S)
        outs = [_attn_group(atts[g], g) for g in range(len(ATT_GROUPS))]
        ret = _retention(main.reshape(B, S, MAIN_COLS), B, S)
        j = layer // 2
        router = None
        if layer % 2 == 1:
            wrt = jnp.zeros((D, LANE), F32).at[:, :N_EXPERTS].set(w_router[j])
            brt = jnp.zeros((1, LANE), F32).at[0, :N_EXPERTS].set(b_router[j])
            router = (wrt, brt)
        wa = jnp.pad(w_att_proj[layer].reshape(HEADS_PER_GROUP, ATT_HEAD_DIM, D),
                     ((0, 0), (0, LANE - ATT_HEAD_DIM), (0, 0))).reshape(ATT_T_WIDTH, D)
        res = _postmix(outs, ret, main, x2, wa.astype(BF16), w_ret_proj[layer].astype(BF16),
                       w_mix_out[layer].astype(BF16), row(norm_post_mix[layer]), row(norm_pre_ffn[layer]),
                       mod, layer, S, router=router)
        if layer % 2 == 0:
            x2, h2 = res
            x2 = _ffn(h2, x2, w_ffn_gate[j].astype(BF16), w_ffn_up[j].astype(BF16), w_ffn_down[j].astype(BF16),
                      row(norm_post_ffn[layer]), mod, layer, S)
        else:
            x2, h2, meta, counts = res
            x2 = _moe(h2, meta, counts, x2, w_exp_gate[j].astype(BF16), w_exp_up[j].astype(BF16),
                      w_exp_down[j].astype(BF16), row(norm_post_ffn[layer]), mod, layer, S)
    return x2.reshape(B, S, D)
```

```python
import functools

import numpy as np
import jax
import jax.numpy as jnp
from jax import lax
from jax.experimental import pallas as pl
from jax.experimental.pallas import tpu as pltpu

F32 = jnp.float32
BF16 = jnp.bfloat16
I32 = jnp.int32

D_MODEL = 1024
DEPTH = 2
ATT_GROUPS = ((128, 1), (512, 4), (2048, 16))
HEADS_PER_GROUP = 4
N_ATT_HEADS = HEADS_PER_GROUP * len(ATT_GROUPS)
ATT_HEAD_DIM = 64
ATT_WIDTH = N_ATT_HEADS * ATT_HEAD_DIM
ATT_OUT_WIDTH = HEADS_PER_GROUP * ATT_HEAD_DIM
BLOCK = 128
ALIBI_MAX = 8.0
NEG_INF = -1e30
RET_HEADS = 8
RET_KEY_DIM = 64
RET_VAL_DIM = 128
RET_QK_WIDTH = RET_HEADS * RET_KEY_DIM
RET_V_WIDTH = RET_HEADS * RET_VAL_DIM
RET_CHUNK = 128
ROPE_BASE = 10000.0
D_FF = 2816
N_EXPERTS = 8
TOP_K = 2
EPS = 1e-6
IN_SIZES = (ATT_WIDTH, ATT_WIDTH, ATT_WIDTH, RET_QK_WIDTH, RET_QK_WIDTH, RET_V_WIDTH, RET_V_WIDTH, D_MODEL, D_MODEL)
IN_COLS = sum(IN_SIZES)

LANE = 128
MAIN_COLS = 4 * D_MODEL + 2 * RET_QK_WIDTH
GROUP_SRC_COLS = 3 * ATT_OUT_WIDTH
GROUP_SLOTS = 3 * HEADS_PER_GROUP
GROUP_COLS = GROUP_SLOTS * LANE
ATT_T_WIDTH = HEADS_PER_GROUP * LANE
MXU_N = 256

MiB = 1024 * 1024


def _cparams(sem, vmem_mib):
    return pltpu.CompilerParams(dimension_semantics=sem, vmem_limit_bytes=vmem_mib * MiB)


def _rms(x, gain):
    return x * lax.rsqrt(jnp.mean(x * x, axis=-1, keepdims=True) + EPS) * gain


def _silu(x):
    return x * jax.nn.sigmoid(x)


def _ada_kernel(c_ref, w_ref, b_ref, o_ref):
    ca = _silu(c_ref[...])
    o_ref[0] = jnp.dot(ca, w_ref[0], preferred_element_type=F32, precision=lax.Precision.HIGHEST) + b_ref[0]


def _ada(c, w_ada, b_ada):
    B, D = c.shape
    out = pl.pallas_call(
        _ada_kernel,
        out_shape=jax.ShapeDtypeStruct((DEPTH * 6, B, D), F32),
        grid=(DEPTH, 6),
        in_specs=[
            pl.BlockSpec((B, D), lambda l, k: (0, 0)),
            pl.BlockSpec((1, D, D), lambda l, k: (l, 0, k)),
            pl.BlockSpec((1, 1, D), lambda l, k: (l, 0, k)),
        ],
        out_specs=pl.BlockSpec((1, B, D), lambda l, k: (l * 6 + k, 0, 0)),
        compiler_params=_cparams(("arbitrary", "arbitrary"), 32),
        name="ada",
    )(c, w_ada, b_ada.reshape(DEPTH, 1, 6 * D))
    return out.reshape(DEPTH, 6, B, 1, D)


def _mod_spec(layer, k, tiles_per_batch):
    return pl.BlockSpec((None, None, None, 1, D_MODEL),
                        lambda i, *_: (layer, k, i // tiles_per_batch, 0, 0))


def _inproj_kernel(x_ref, gain_ref, sc_ref, sh_ref, w_ref, main_ref, a0_ref, a1_ref, a2_ref, scr_ref):
    tm = x_ref.shape[0]
    h = _rms(x_ref[...], gain_ref[...]) * (1.0 + sc_ref[...]) + sh_ref[...]
    hb = h.astype(BF16)
    for c in range(MAIN_COLS // MXU_N):
        sl = slice(c * MXU_N, (c + 1) * MXU_N)
        main_ref[:, sl] = jnp.dot(hb, w_ref[:, sl], preferred_element_type=F32).astype(BF16)
    att_refs = (a0_ref, a1_ref, a2_ref)
    heads_per_slab = LANE // ATT_HEAD_DIM

    def store_heads(att_ref, r, slot0, val):
        lane = lax.broadcasted_iota(I32, val.shape, 1)
        for k in range(heads_per_slab):
            head = val if k == 0 else pltpu.roll(val, LANE - k * ATT_HEAD_DIM, 1)
            c0 = (slot0 + k) * LANE
            att_ref[r, :, c0:c0 + LANE] = jnp.where(lane < ATT_HEAD_DIM, head, 0.0).astype(BF16)

    slab = 0
    for g, (_, d) in enumerate(ATT_GROUPS):
        for seg in range(3):
            col = MAIN_COLS + g * GROUP_SRC_COLS + seg * ATT_OUT_WIDTH
            res = jnp.dot(hb, w_ref[:, col:col + ATT_OUT_WIDTH], preferred_element_type=F32)
            for s in range(ATT_OUT_WIDTH // LANE):
                part = res[:, s * LANE:(s + 1) * LANE]
                slot0 = seg * HEADS_PER_GROUP + s * heads_per_slab
                if d == 1:
                    store_heads(att_refs[g], 0, slot0, part)
                    continue
                scr_ref[slab] = part
                for r in range(d):
                    store_heads(att_refs[g], r, slot0, scr_ref[slab, pl.ds(r, tm // d, stride=d), :])
                slab += 1


def _inproj(x2, gain, mod, layer, w_in_b, B, S, tm=512):
    R, D = x2.shape
    tpb = S // tm
    n_slabs = sum(3 * (ATT_OUT_WIDTH // LANE) for _, d in ATT_GROUPS if d > 1)

    def att_shape(d):
        return jax.ShapeDtypeStruct((B, d, S // d, GROUP_COLS), BF16)

    def att_spec(d):
        return pl.BlockSpec((None, d, tm // d, GROUP_COLS), lambda i: (i // tpb, 0, i % tpb, 0))

    return pl.pallas_call(
        _inproj_kernel,
        out_shape=(jax.ShapeDtypeStruct((R, MAIN_COLS), BF16),) + tuple(att_shape(d) for _, d in ATT_GROUPS),
        grid=(R // tm,),
        in_specs=[
            pl.BlockSpec((tm, D), lambda i: (i, 0)),
            pl.BlockSpec((1, D), lambda i: (0, 0)),
            _mod_spec(layer, 1, tpb),
            _mod_spec(layer, 0, tpb),
            pl.BlockSpec((D, IN_COLS), lambda i: (0, 0), pipeline_mode=pl.Buffered(1)),
        ],
        out_specs=(pl.BlockSpec((tm, MAIN_COLS), lambda i: (i, 0)),) + tuple(att_spec(d) for _, d in ATT_GROUPS),
        scratch_shapes=[pltpu.VMEM((n_slabs, tm, LANE), F32)],
        compiler_params=_cparams(("arbitrary",), 56),
        name="inproj",
    )(x2, gain, mod, mod, w_in_b)


def _prep_w_in(w):
    b = np.concatenate([[0], np.cumsum(IN_SIZES)])
    seg = lambda s: w[:, b[s]:b[s + 1]]
    cols = [seg(5), seg(6), seg(7), seg(8), seg(3), seg(4)]
    q_a, k_a, v_a = seg(0) * (ATT_HEAD_DIM ** -0.5), seg(1), seg(2)
    for g in range(len(ATT_GROUPS)):
        gs = slice(g * ATT_OUT_WIDTH, (g + 1) * ATT_OUT_WIDTH)
        cols += [q_a[:, gs], k_a[:, gs], v_a[:, gs]]
    return jnp.concatenate(cols, axis=1).astype(BF16)


def _alibi_slopes():
    return [2.0 ** (-ALIBI_MAX * h / N_ATT_HEADS) for h in range(1, N_ATT_HEADS + 1)]


ATT_UNITS = 4


def _attn_kernel(a_ref, bias_ref, o_ref, *, n, dilation):
    nb = n // BLOCK
    win = min(2 * BLOCK, n)
    ones = jnp.ones((win, LANE), BF16)
    lane = lax.broadcasted_iota(I32, (BLOCK, LANE), 1)

    def head_units(u):
        r = u // nb
        i = u % nb
        q0 = pl.multiple_of(i * BLOCK, BLOCK)
        k0 = pl.multiple_of(jnp.clip((i - 1) * BLOCK, 0, n - win), BLOCK)
        t = jnp.minimum(i, 1)
        return [(r, pl.ds(q0, BLOCK), pl.ds(k0, win), t, h) for h in range(HEADS_PER_GROUP)]

    def pair(u2, carry):
        units = sum((head_units(ATT_UNITS * u2 + j) for j in range(ATT_UNITS)), [])
        scores = []
        for r, qrows, krows, t, h in units:
            qs = slice(h * LANE, (h + 1) * LANE)
            ks = slice((HEADS_PER_GROUP + h) * LANE, (HEADS_PER_GROUP + h + 1) * LANE)
            s = lax.dot_general(a_ref[r, qrows, qs], a_ref[r, krows, ks], (((1,), (1,)), ((), ())),
                                preferred_element_type=F32)
            scores.append(s + bias_ref[t, h])
        probs = []
        for s in scores:
            m = jnp.max(s, axis=-1, keepdims=True)
            probs.append((jnp.exp(s - m).astype(BF16), m))
        for (r, qrows, krows, t, h), (p, m) in zip(units, probs):
            qs = slice(h * LANE, (h + 1) * LANE)
            vs = slice((2 * HEADS_PER_GROUP + h) * LANE, (2 * HEADS_PER_GROUP + h + 1) * LANE)
            pv = jnp.dot(p, jnp.concatenate([a_ref[r, krows, vs], ones], axis=1), preferred_element_type=F32)
            den = pv[:, LANE:]
            o_ref[r, qrows, qs] = jnp.where(lane < ATT_HEAD_DIM, pv[:, :LANE] / den, m + jnp.log(den))
        return carry

    lax.fori_loop(0, dilation * nb // ATT_UNITS, pair, 0)


def _attn_bias(n, n_w, d, slopes):
    win = min(2 * BLOCK, n)
    r = np.arange(BLOCK)[:, None]
    c = np.arange(win)[None, :]
    tables = []
    for shift in (0, win - BLOCK):
        dist = shift + r - c
        valid = (dist >= 0) & (dist <= n_w)
        tables.append(np.stack([np.where(valid, -s * (d * dist), NEG_INF) for s in slopes]))
    return jnp.asarray(np.stack(tables), F32)


def _attn_group(att, g):
    window, d = ATT_GROUPS[g]
    B, _, n, _ = att.shape
    assert n % BLOCK == 0 and (d * (n // BLOCK)) % ATT_UNITS == 0
    slopes = _alibi_slopes()[g * HEADS_PER_GROUP:(g + 1) * HEADS_PER_GROUP]
    bias = _attn_bias(n, window // d, d, slopes)
    kern = functools.partial(_attn_kernel, n=n, dilation=d)
    out_spec = pl.BlockSpec((None, d, n, ATT_T_WIDTH), lambda b: (b, 0, 0, 0))
    return pl.pallas_call(
        kern,
        out_shape=jax.ShapeDtypeStruct((B, d, n, ATT_T_WIDTH), F32),
        grid=(B,),
        in_specs=[pl.BlockSpec((None, d, n, GROUP_COLS), lambda b: (b, 0, 0, 0)),
                  pl.BlockSpec(bias.shape, lambda b: (0, 0, 0, 0))],
        out_specs=out_spec,
        compiler_params=_cparams(("arbitrary",), 48),
        name=f"attn{g}",
    )(att, bias)


def _ret_tables(S):
    half = RET_KEY_DIM // 2
    inv_freq = 1.0 / (ROPE_BASE ** (jnp.arange(half, dtype=F32) / half))
    ang = jnp.arange(S).astype(F32)[:, None] * inv_freq[None]
    cos = jnp.cos(ang)
    sin = jnp.sin(ang)
    cos_t = jnp.tile(jnp.concatenate([cos, cos], axis=-1), (1, RET_HEADS))
    sin_t = jnp.tile(jnp.concatenate([-sin, sin], axis=-1), (1, RET_HEADS))
    log_gamma = jnp.log1p(-jnp.exp2(-5.0 - jnp.arange(RET_HEADS, dtype=F32)))
    idx = jnp.arange(RET_CHUNK, dtype=F32)
    rel = idx[:, None] - idx[None, :]
    decay = jnp.where(rel[None] >= 0, jnp.exp(jnp.maximum(rel, 0.0)[None] * log_gamma[:, None, None]), 0.0)
    zeta = jnp.exp((RET_CHUNK - 1 - idx)[None] * log_gamma[:, None])
    xi = jnp.exp((idx + 1)[None] * log_gamma[:, None])
    zeta_t = jnp.repeat(zeta.T, RET_KEY_DIM, axis=1) * (RET_KEY_DIM ** -0.5)
    xi_t = jnp.repeat(xi.T, RET_KEY_DIM, axis=1)
    g_chunk = jnp.exp(RET_CHUNK * log_gamma)
    g_t = jnp.broadcast_to(g_chunk[:, None, None], (RET_HEADS, 1, RET_VAL_DIM))
    return cos_t, sin_t, decay, zeta_t, xi_t, g_t


def _rot_half_swap(t):
    w = t.shape[-1]
    lane = lax.broadcasted_iota(I32, t.shape, 1) % RET_KEY_DIM
    half = RET_KEY_DIM // 2
    return jnp.where(lane < half, pltpu.roll(t, w - half, 1), pltpu.roll(t, half, 1))


RET_ROWS = 1024
RET_PAIR = 4


def _retn_kernel(q_ref, k_ref, v_ref, g_ref, cos_ref, sin_ref, decay_ref, zeta_ref, xi_ref, gch_ref,
                 o_ref, state_ref):
    C = RET_CHUNK

    @pl.when(pl.program_id(1) == 0)
    def _():
        state_ref[...] = jnp.zeros_like(state_ref)

    def prep(c):
        rows = pl.ds(pl.multiple_of(c * C, C), C)
        cos = cos_ref[rows, :]
        sin = sin_ref[rows, :]
        q = q_ref[rows, :].astype(F32)
        k = k_ref[rows, :].astype(F32)
        q = q * cos + _rot_half_swap(q) * sin
        k = k * cos + _rot_half_swap(k) * sin
        qb = q.astype(BF16)
        kb = (k * (RET_KEY_DIM ** -0.5)).astype(BF16)
        q_xi = (q * xi_ref[...]).astype(BF16)
        kz_t = (k * zeta_ref[...]).T
        return rows, qb, kb, q_xi, kz_t

    def group(u, carry):
        chunks = [prep(u * RET_PAIR + i) for i in range(RET_PAIR)]
        ksl = [slice(h * RET_KEY_DIM, (h + 1) * RET_KEY_DIM) for h in range(RET_HEADS)]
        vsl = [slice(h * RET_VAL_DIM, (h + 1) * RET_VAL_DIM) for h in range(RET_HEADS)]
        units = [(c, h) for h in range(RET_HEADS) for c in range(RET_PAIR)]
        scores, kvs = {}, {}
        for c, h in units:
            rows, qb, kb, q_xi, kz_t = chunks[c]
            v = v_ref[rows, vsl[h]]
            scores[c, h] = lax.dot_general(qb[:, ksl[h]], kb[:, ksl[h]], (((1,), (1,)), ((), ())),
                                           preferred_element_type=F32)
            kvs[c, h] = jnp.dot(kz_t[ksl[h], :].astype(BF16), v, preferred_element_type=F32)
        states = {}
        for h in range(RET_HEADS):
            st = state_ref[h]
            for c in range(RET_PAIR):
                states[c, h] = st.astype(BF16)
                st = st * gch_ref[h] + kvs[c, h]
            state_ref[h] = st
        outs = {}
        for c, h in units:
            rows, qb, kb, q_xi, kz_t = chunks[c]
            s = (scores[c, h] * decay_ref[h]).astype(BF16)
            outs[c, h] = (jnp.dot(s, v_ref[rows, vsl[h]], preferred_element_type=F32)
                          + jnp.dot(q_xi[:, ksl[h]], states[c, h], preferred_element_type=F32))
        for c, h in units:
            rows = chunks[c][0]
            o = outs[c, h]
            o = o * lax.rsqrt(jnp.mean(o * o, axis=-1, keepdims=True) + EPS)
            o_ref[rows, vsl[h]] = (_silu(g_ref[rows, vsl[h]].astype(F32)) * o).astype(o_ref.dtype)
        return carry

    lax.fori_loop(0, q_ref.shape[0] // (C * RET_PAIR), group, 0)


def _retention(main3, B, S):
    C = RET_CHUNK
    RS = RET_ROWS
    assert S % RS == 0 and RS % (C * RET_PAIR) == 0
    cos_t, sin_t, decay, zeta_t, xi_t, g_t = _ret_tables(S)
    const2 = lambda b, c: (0, 0)
    const3 = lambda b, c: (0, 0, 0)
    qk0 = 4 * D_MODEL // RET_QK_WIDTH
    out = pl.pallas_call(
        _retn_kernel,
        out_shape=jax.ShapeDtypeStruct((B, S, RET_V_WIDTH), BF16),
        grid=(B, S // RS),
        in_specs=[
            pl.BlockSpec((None, RS, RET_QK_WIDTH), lambda b, c: (b, c, qk0)),
            pl.BlockSpec((None, RS, RET_QK_WIDTH), lambda b, c: (b, c, qk0 + 1)),
            pl.BlockSpec((None, RS, RET_V_WIDTH), lambda b, c: (b, c, 0)),
            pl.BlockSpec((None, RS, RET_V_WIDTH), lambda b, c: (b, c, 1)),
            pl.BlockSpec((RS, RET_QK_WIDTH), lambda b, c: (c, 0)),
            pl.BlockSpec((RS, RET_QK_WIDTH), lambda b, c: (c, 0)),
            pl.BlockSpec((RET_HEADS, C, C), const3),
            pl.BlockSpec((C, RET_QK_WIDTH), const2),
            pl.BlockSpec((C, RET_QK_WIDTH), const2),
            pl.BlockSpec((RET_HEADS, 1, RET_VAL_DIM), const3),
        ],
        out_specs=pl.BlockSpec((None, RS, RET_V_WIDTH), lambda b, c: (b, c, 0)),
        scratch_shapes=[pltpu.VMEM((RET_HEADS, RET_KEY_DIM, RET_VAL_DIM), F32)],
        compiler_params=_cparams(("arbitrary", "arbitrary"), 48),
        name="retn",
    )(main3, main3, main3, main3, cos_t, sin_t, decay, zeta_t, xi_t, g_t)
    return out.reshape(B * S, RET_V_WIDTH)


N_META = 6


def _unstride(ref, scr_ref, slab0, d, tm):
    if d == 1:
        return [ref[0, :, s * LANE:(s + 1) * LANE] for s in range(HEADS_PER_GROUP)]
    for r in range(d):
        for s in range(HEADS_PER_GROUP):
            scr_ref[slab0 + s, pl.ds(r, tm // d, stride=d), :] = ref[r, :, s * LANE:(s + 1) * LANE]
    return [scr_ref[slab0 + s] for s in range(HEADS_PER_GROUP)]


def _postmix_kernel(o0_ref, o1_ref, o2_ref, ret_ref, ga_ref, gr_ref, x_ref,
                    wa_ref, wr_ref, wo_ref, npost_ref, g1_ref, npre_ref, sc2_ref, sh2_ref, *rest, with_router):
    if with_router:
        wrt_ref, brt_ref, tri_ref, xo_ref, h_ref, meta_ref, cnt_ref, scr_ref, carry_ref = rest
    else:
        xo_ref, h_ref, scr_ref = rest
    tm = x_ref.shape[0]
    groups = []
    slab = 0
    for (_, d), o_ref in zip(ATT_GROUPS, (o0_ref, o1_ref, o2_ref)):
        groups.append(_unstride(o_ref, scr_ref, slab, d, tm))
        if d > 1:
            slab += HEADS_PER_GROUP
    head_lane = lax.broadcasted_iota(I32, (tm, LANE), 1)
    slots = []
    for hd in range(HEADS_PER_GROUP):
        outs = [grp[hd] for grp in groups]
        lses = [pltpu.roll(o, LANE // 2, 1) for o in outs]
        mx = functools.reduce(jnp.maximum, lses)
        es = [jnp.exp(l - mx) for l in lses]
        num = sum(e * o for e, o in zip(es, outs))
        slots.append(jnp.where(head_lane < ATT_HEAD_DIM, num / sum(es), 0.0))
    att = jnp.concatenate(slots, axis=1)
    a = jnp.dot(att.astype(BF16), wa_ref[...], preferred_element_type=F32)
    r = jnp.dot(ret_ref[...], wr_ref[...], preferred_element_type=F32)
    merged = jax.nn.sigmoid(ga_ref[...].astype(F32)) * a + jax.nn.sigmoid(gr_ref[...].astype(F32)) * r
    y = jnp.dot(merged.astype(BF16), wo_ref[...], preferred_element_type=F32)
    x = x_ref[...] + g1_ref[...] * _rms(y, npost_ref[...])
    xo_ref[...] = x
    h = _rms(x, npre_ref[...]) * (1.0 + sc2_ref[...]) + sh2_ref[...]
    if not with_router:
        h_ref[...] = h.astype(h_ref.dtype)
        return
    _to_token_tiles(h_ref, h)

    @pl.when(pl.program_id(0) == 0)
    def _():
        carry_ref[...] = jnp.zeros_like(carry_ref)

    h_hi = h.astype(BF16)
    h_lo = (h - h_hi.astype(F32)).astype(BF16)
    w = wrt_ref[...]
    w_hi = w.astype(BF16)
    w_lo = (w - w_hi.astype(F32)).astype(BF16)
    lg = (jnp.dot(h_hi, w_hi, preferred_element_type=F32) + jnp.dot(h_hi, w_lo, preferred_element_type=F32)
          + jnp.dot(h_lo, w_hi, preferred_element_type=F32)) + brt_ref[...]
    lane = lax.broadcasted_iota(I32, lg.shape, 1).astype(F32)
    lg = jnp.where(lane < N_EXPERTS, lg, -jnp.inf)
    m1 = jnp.max(lg, axis=-1, keepdims=True)
    i1 = jnp.min(jnp.where(lg == m1, lane, float(LANE)), axis=-1, keepdims=True)
    lg2 = jnp.where(lane == i1, -jnp.inf, lg)
    m2 = jnp.max(lg2, axis=-1, keepdims=True)
    i2 = jnp.min(jnp.where(lg2 == m2, lane, float(LANE)), axis=-1, keepdims=True)
    t = jnp.exp(m2 - m1)
    w1 = 1.0 / (1.0 + t)
    w2 = t / (1.0 + t)
    oh = jnp.where((lane == i1) | (lane == i2), 1.0, 0.0)
    pos = jnp.dot(tri_ref[...], oh.astype(BF16), preferred_element_type=F32) + carry_ref[...]
    r1 = jnp.sum(jnp.where(lane == i1, pos, 0.0), axis=-1, keepdims=True)
    r2 = jnp.sum(jnp.where(lane == i2, pos, 0.0), axis=-1, keepdims=True)
    carry = carry_ref[...] + jnp.sum(oh, axis=0, keepdims=True)
    carry_ref[...] = carry
    cnt_ref[...] = jnp.broadcast_to(carry, cnt_ref.shape)
    meta = jnp.zeros(lg.shape, F32)
    for j, val in enumerate((i1, i2, r1, r2, w1, w2)):
        meta = jnp.where(lane == j, val, meta)
    meta_ref[...] = meta


def _postmix(outs, ret, main, x2, wa, wr, wo, npost, npre, mod, layer, S, router=None, tm=512):
    R, D = x2.shape
    tpb = S // tm
    row = lambda w: pl.BlockSpec((tm, w), lambda i: (i, 0))
    const = lambda a: pl.BlockSpec(a.shape, lambda i: (0, 0))

    def att_spec(d):
        return pl.BlockSpec((None, d, tm // d, ATT_T_WIDTH), lambda i: (i // tpb, 0, i % tpb, 0))

    specs_att = [att_spec(d) for _, d in ATT_GROUPS]
    in_specs = (specs_att + [
        row(RET_V_WIDTH),
        pl.BlockSpec((tm, D), lambda i: (i, 2)),
        pl.BlockSpec((tm, D), lambda i: (i, 3)),
        row(D), const(wa), const(wr), const(wo), const(npost),
        _mod_spec(layer, 2, tpb), const(npre), _mod_spec(layer, 4, tpb), _mod_spec(layer, 3, tpb)])
    args = list(outs) + [ret, main, main, x2, wa, wr, wo, npost, mod, npre, mod, mod]
    n_slabs = sum(HEADS_PER_GROUP for _, d in ATT_GROUPS if d > 1)
    scratch = [pltpu.VMEM((n_slabs, tm, LANE), F32)]
    if router is None:
        out_shape = [jax.ShapeDtypeStruct((R, D), F32), jax.ShapeDtypeStruct((R, D), BF16)]
        out_specs = [row(D), row(D)]
        sem = ("arbitrary",)
    else:
        tri = jnp.asarray(np.tril(np.ones((tm, tm), np.float32), -1), BF16)
        router = tuple(router) + (tri,)
        in_specs += [const(a) for a in router]
        args += list(router)
        out_shape = [jax.ShapeDtypeStruct((R, D), F32), jax.ShapeDtypeStruct((R * TOK_ROWS, LANE), F32),
                     jax.ShapeDtypeStruct((R, LANE), F32), jax.ShapeDtypeStruct((8, LANE), F32)]
        out_specs = [row(D), pl.BlockSpec((tm * TOK_ROWS, LANE), lambda i: (i, 0)), row(LANE),
                     pl.BlockSpec((8, LANE), lambda i: (0, 0))]
        scratch.append(pltpu.VMEM((1, LANE), F32))
        sem = ("arbitrary",)
    return pl.pallas_call(
        functools.partial(_postmix_kernel, with_router=router is not None),
        out_shape=tuple(out_shape),
        grid=(R // tm,),
        in_specs=in_specs,
        out_specs=tuple(out_specs),
        scratch_shapes=scratch,
        compiler_params=_cparams(sem, 56),
        name="postmix",
    )(*args)


def _swiglu(hb, wg_ref, wu_ref, wd_ref):
    y = None
    for c in range(D_FF // MXU_N):
        cols = slice(c * MXU_N, (c + 1) * MXU_N)
        act = (_silu(jnp.dot(hb, wg_ref[:, cols], preferred_element_type=F32))
               * jnp.dot(hb, wu_ref[:, cols], preferred_element_type=F32))
        part = jnp.dot(act.astype(BF16), wd_ref[cols, :], preferred_element_type=F32)
        y = part if y is None else y + part
    return y


def _ffn_kernel(h_ref, x_ref, wg_ref, wu_ref, wd_ref, npost_ref, g2_ref, o_ref):
    y = _swiglu(h_ref[...], wg_ref, wu_ref, wd_ref)
    o_ref[...] = x_ref[...] + g2_ref[...] * _rms(y, npost_ref[...])


def _ffn(h2, x2, wg, wu, wd, npost, mod, layer, S, tm=512):
    R, D = x2.shape
    tpb = S // tm
    resident = lambda a: pl.BlockSpec(a.shape, lambda i: (0, 0), pipeline_mode=pl.Buffered(1))
    return pl.pallas_call(
        _ffn_kernel,
        out_shape=jax.ShapeDtypeStruct((R, D), F32),
        grid=(R // tm,),
        in_specs=[
            pl.BlockSpec((tm, D), lambda i: (i, 0)),
            pl.BlockSpec((tm, D), lambda i: (i, 0)),
            resident(wg), resident(wu), resident(wd),
            pl.BlockSpec((1, D), lambda i: (0, 0)),
            _mod_spec(layer, 5, tpb),
        ],
        out_specs=pl.BlockSpec((tm, D), lambda i: (i, 0)),
        compiler_params=_cparams(("arbitrary",), 48),
        name="ffn",
    )(h2, x2, wg, wu, wd, npost, mod)


MOE_TM = 512
ROW_UNROLL = 8
DMA_QUEUES = 2


def _invperm_kernel(n_ref, dest_ref, src_ref, zeros_ref, sem):
    zeros_ref[...] = jnp.zeros_like(zeros_ref)
    fill = pltpu.make_async_copy(zeros_ref, src_ref, sem)
    fill.start()
    fill.wait()

    def put(g, c):
        for u in range(ROW_UNROLL):
            a = g * ROW_UNROLL + u
            src_ref[dest_ref[a]] = a
        return c

    lax.fori_loop(0, n_ref[0], put, 0)


def _invperm(dest, P):
    assert dest.shape[0] % ROW_UNROLL == 0
    trips = jnp.array([dest.shape[0] // ROW_UNROLL], I32)
    return pl.pallas_call(
        _invperm_kernel,
        out_shape=jax.ShapeDtypeStruct((P,), I32),
        in_specs=[pl.BlockSpec(memory_space=pltpu.SMEM), pl.BlockSpec(memory_space=pltpu.SMEM)],
        out_specs=pl.BlockSpec(memory_space=pltpu.SMEM),
        scratch_shapes=[pltpu.VMEM((P,), I32), pltpu.SemaphoreType.DMA],
        name="invperm",
    )(trips, dest)


TOK_ROWS = D_MODEL // LANE


def _to_token_tiles(ref, val):
    tm = val.shape[0]
    for s in range(TOK_ROWS):
        ref[pl.ds(s, tm, stride=TOK_ROWS), :] = val[:, s * LANE:(s + 1) * LANE]


def _token_tile_slabs(read, tm):
    return [read(pl.ds(s, tm, stride=TOK_ROWS)) for s in range(TOK_ROWS)]


def _expert_kernel(te_ref, nu_ref, src_ref, h_hbm, wg_ref, wu_ref, wd_ref, o_ref, hb_ref, hbuf_ref, sem, *, n_tok):
    j = pl.program_id(0)
    nu = nu_ref[0]
    tm = hb_ref.shape[0]

    def row_copy(row, slot, i):
        src = h_hbm.at[pl.ds(pl.multiple_of(row * TOK_ROWS, TOK_ROWS), TOK_ROWS), :]
        dst = hbuf_ref.at[slot, pl.ds(pl.multiple_of(i * TOK_ROWS, TOK_ROWS), TOK_ROWS), :]
        return pltpu.make_async_copy(src, dst, sem.at[slot])

    def issue(tile, slot):
        def body(g, c):
            for k in range(DMA_QUEUES):
                i = g * DMA_QUEUES + k
                row_copy(src_ref[tile * tm + i] & (n_tok - 1), slot, i).start(priority=k)
            return c

        lax.fori_loop(0, tm // DMA_QUEUES, body, 0, unroll=ROW_UNROLL // DMA_QUEUES)

    def wait(slot):
        def body(i, c):
            row_copy(0, slot, 0).wait()
            return c

        lax.fori_loop(0, tm, body, 0, unroll=ROW_UNROLL)

    slot = j % 2

    @pl.when(j == 0)
    def _():
        issue(0, 0)

    @pl.when(j <= nu)
    def _():
        wait(slot)

    @pl.when(j < nu)
    def _():
        for s, slab in enumerate(_token_tile_slabs(lambda rows: hbuf_ref[slot, rows, :], tm)):
            hb_ref[:, s * LANE:(s + 1) * LANE] = slab.astype(BF16)
        issue(jnp.minimum(j + 1, pl.num_programs(0) - 1), 1 - slot)
        _to_token_tiles(o_ref, _swiglu(hb_ref[...], wg_ref, wu_ref, wd_ref))

    @pl.when((j == pl.num_programs(0) - 1) & (j < nu))
    def _():
        wait(1 - slot)

    @pl.when(j >= nu)
    def _():
        o_ref[...] = jnp.zeros_like(o_ref)


def _experts(tile_expert, n_used, src, h2t, wg, wu, wd, n_tiles, tm):
    R, D = h2t.shape[0] // TOK_ROWS, D_MODEL
    assert R & (R - 1) == 0
    expert = lambda j, te, nu: te[jnp.minimum(j, nu[0] - 1)]
    return pl.pallas_call(
        functools.partial(_expert_kernel, n_tok=R),
        out_shape=jax.ShapeDtypeStruct((n_tiles * tm * TOK_ROWS, LANE), F32),
        grid_spec=pltpu.PrefetchScalarGridSpec(
            num_scalar_prefetch=3,
            grid=(n_tiles,),
            in_specs=[
                pl.BlockSpec(memory_space=pl.ANY),
                pl.BlockSpec((None, D, D_FF), lambda j, te, nu, src: (expert(j, te, nu), 0, 0)),
                pl.BlockSpec((None, D, D_FF), lambda j, te, nu, src: (expert(j, te, nu), 0, 0)),
                pl.BlockSpec((None, D_FF, D), lambda j, te, nu, src: (expert(j, te, nu), 0, 0)),
            ],
            out_specs=pl.BlockSpec((tm * TOK_ROWS, LANE), lambda j, te, nu, src: (j, 0)),
            scratch_shapes=[pltpu.VMEM((tm, D), BF16), pltpu.VMEM((2, tm * TOK_ROWS, LANE), F32),
                            pltpu.SemaphoreType.DMA((2,))],
        ),
        compiler_params=_cparams(("arbitrary",), 56),
        name="moe_experts",
    )(tile_expert, n_used, src, h2t, wg, wu, wd)


def _combine_kernel(d1_ref, d2_ref, y_hbm, meta_ref, x_ref, npost_ref, g2_ref, o_ref, ya_ref, yb_ref, sem):
    tm = x_ref.shape[0]
    step = pl.program_id(0)
    slot = step % 2

    def row_copy(row, dst_ref, s, r):
        src = y_hbm.at[pl.ds(pl.multiple_of(row * TOK_ROWS, TOK_ROWS), TOK_ROWS), :]
        dst = dst_ref.at[s, pl.ds(pl.multiple_of(r * TOK_ROWS, TOK_ROWS), TOK_ROWS), :]
        return pltpu.make_async_copy(src, dst, sem.at[s])

    def issue(tile, s):
        def body(r, c):
            t = tile * tm + r
            row_copy(d1_ref[t], ya_ref, s, r).start(priority=0)
            row_copy(d2_ref[t], yb_ref, s, r).start(priority=1)
            return c

        lax.fori_loop(0, tm, body, 0, unroll=ROW_UNROLL)

    @pl.when(step == 0)
    def _():
        issue(0, 0)

    @pl.when(step + 1 < pl.num_programs(0))
    def _():
        issue(step + 1, 1 - slot)

    def wait(r, c):
        row_copy(0, ya_ref, slot, 0).wait()
        row_copy(0, yb_ref, slot, 0).wait()
        return c

    lax.fori_loop(0, tm, wait, 0, unroll=ROW_UNROLL)
    meta = meta_ref[...]
    w1, w2 = meta[:, 4:5], meta[:, 5:6]
    slabs_a = _token_tile_slabs(lambda rows: ya_ref[slot, rows, :], tm)
    slabs_b = _token_tile_slabs(lambda rows: yb_ref[slot, rows, :], tm)
    y = jnp.concatenate([w1 * a + w2 * b for a, b in zip(slabs_a, slabs_b)], axis=1)
    o_ref[...] = x_ref[...] + g2_ref[...] * _rms(y, npost_ref[...])


def _combine(dest1, dest2, sorted_y, meta, x2, npost, mod, layer, S, tm=512):
    R, D = x2.shape
    tpb = S // tm
    return pl.pallas_call(
        _combine_kernel,
        out_shape=jax.ShapeDtypeStruct((R, D), F32),
        grid_spec=pltpu.PrefetchScalarGridSpec(
            num_scalar_prefetch=2,
            grid=(R // tm,),
            in_specs=[
                pl.BlockSpec(memory_space=pl.ANY),
                pl.BlockSpec((tm, LANE), lambda i, *_: (i, 0)),
                pl.BlockSpec((tm, D), lambda i, *_: (i, 0)),
                pl.BlockSpec((1, D), lambda i, *_: (0, 0)),
                _mod_spec(layer, 5, tpb),
            ],
            out_specs=pl.BlockSpec((tm, D), lambda i, *_: (i, 0)),
            scratch_shapes=[pltpu.VMEM((2, tm * TOK_ROWS, LANE), F32), pltpu.VMEM((2, tm * TOK_ROWS, LANE), F32),
                            pltpu.SemaphoreType.DMA((2,))],
        ),
        compiler_params=_cparams(("arbitrary",), 40),
        name="moe_combine",
    )(dest1, dest2, sorted_y, meta, x2, npost, mod)


def _moe(h2, meta, counts, x2, wg, wu, wd, npost, mod, layer, S):
    R, D = x2.shape
    tm = MOE_TM
    n_tiles = (TOP_K * R + N_EXPERTS * (tm - 1)) // tm + 1
    i1, i2, r1, r2 = (meta[:, j].astype(I32) for j in range(4))
    cnt = counts[0, :N_EXPERTS].astype(I32)
    padded = (cnt + tm - 1) // tm * tm
    end = jnp.cumsum(padded)
    start = end - padded
    eye = jnp.arange(N_EXPERTS, dtype=I32)[None, :]
    dest1 = jnp.sum(jnp.where(i1[:, None] == eye, start[None, :], 0), axis=1) + r1
    dest2 = jnp.sum(jnp.where(i2[:, None] == eye, start[None, :], 0), axis=1) + r2
    tile_row0 = jnp.arange(n_tiles, dtype=I32) * tm
    tile_expert = jnp.minimum(jnp.sum(tile_row0[:, None] >= end[None, :], axis=1), N_EXPERTS - 1).astype(I32)
    n_used = (end[-1:] // tm).astype(I32)
    src = _invperm(jnp.concatenate([dest1, dest2]), n_tiles * tm)
    sorted_y = _experts(tile_expert, n_used, src, h2, wg, wu, wd, n_tiles, tm)
    return _combine(dest1, dest2, sorted_y, meta, x2, npost, mod, layer, S)


def kernel(x, c, w_ada, b_ada, norm_pre_mix, w_in, w_att_proj, w_ret_proj, w_mix_out, norm_post_mix, norm_pre_ffn, w_ffn_gate, w_ffn_up, w_ffn_down, w_router, b_router, w_exp_gate, w_exp_up, w_exp_down, norm_post_ffn):
    B, S, D = x.shape
    R = B * S
    mod = _ada(c, w_ada, b_ada)
    x2 = x.reshape(R, D)
    row = lambda v: v.reshape(1, D)
    for layer in range(DEPTH):
        main, *atts = _inproj(x2, row(norm_pre_mix[layer]), mod, layer, _prep_w_in(w_in[layer]), B, S)
        outs = [_attn_group(atts[g], g) for g in range(len(ATT_GROUPS))]
        ret = _retention(main.reshape(B, S, MAIN_COLS), B, S)
        j = layer // 2
        router = None
        if layer % 2 == 1:
            wrt = jnp.zeros((D, LANE), F32).at[:, :N_EXPERTS].set(w_router[j])
            brt = jnp.zeros((1, LANE), F32).at[0, :N_EXPERTS].set(b_router[j])
            router = (wrt, brt)
        wa = jnp.pad(w_att_proj[layer].reshape(HEADS_PER_GROUP, ATT_HEAD_DIM, D),
                     ((0, 0), (0, LANE - ATT_HEAD_DIM), (0, 0))).reshape(ATT_T_WIDTH, D)
        res = _postmix(outs, ret, main, x2, wa.astype(BF16), w_ret_proj[layer].astype(BF16),
                       w_mix_out[layer].astype(BF16), row(norm_post_mix[layer]), row(norm_pre_ffn[layer]),
                       mod, layer, S, router=router)
        if layer % 2 == 0:
            x2, h2 = res
            x2 = _ffn(h2, x2, w_ffn_gate[j].astype(BF16), w_ffn_up[j].astype(BF16), w_ffn_down[j].astype(BF16),
                      row(norm_post_ffn[layer]), mod, layer, S)
        else:
            x2, h2, meta, counts = res
            x2 = _moe(h2, meta, counts, x2, w_exp_gate[j].astype(BF16), w_exp_up[j].astype(BF16),
                      w_exp_down[j].astype(BF16), row(norm_post_ffn[layer]), mod, layer, S)
    return x2.reshape(B, S, D)
```

```python
import functools

import numpy as np
import jax
import jax.numpy as jnp
from jax import lax
from jax.experimental import pallas as pl
from jax.experimental.pallas import tpu as pltpu

F32 = jnp.float32
BF16 = jnp.bfloat16
I32 = jnp.int32

D_MODEL = 1024
DEPTH = 2
ATT_GROUPS = ((128, 1), (512, 4), (2048, 16))
HEADS_PER_GROUP = 4
N_ATT_HEADS = HEADS_PER_GROUP * len(ATT_GROUPS)
ATT_HEAD_DIM = 64
ATT_WIDTH = N_ATT_HEADS * ATT_HEAD_DIM
ATT_OUT_WIDTH = HEADS_PER_GROUP * ATT_HEAD_DIM
BLOCK = 128
ALIBI_MAX = 8.0
NEG_INF = -1e30
RET_HEADS = 8
RET_KEY_DIM = 64
RET_VAL_DIM = 128
RET_QK_WIDTH = RET_HEADS * RET_KEY_DIM
RET_V_WIDTH = RET_HEADS * RET_VAL_DIM
RET_CHUNK = 128
ROPE_BASE = 10000.0
D_FF = 2816
N_EXPERTS = 8
TOP_K = 2
EPS = 1e-6
IN_SIZES = (ATT_WIDTH, ATT_WIDTH, ATT_WIDTH, RET_QK_WIDTH, RET_QK_WIDTH, RET_V_WIDTH, RET_V_WIDTH, D_MODEL, D_MODEL)
IN_COLS = sum(IN_SIZES)

LANE = 128
MAIN_COLS = 4 * D_MODEL + 2 * RET_QK_WIDTH
GROUP_SRC_COLS = 3 * ATT_OUT_WIDTH
GROUP_SLOTS = 3 * HEADS_PER_GROUP
GROUP_COLS = GROUP_SLOTS * LANE
ATT_T_WIDTH = HEADS_PER_GROUP * LANE
MXU_N = 256

MiB = 1024 * 1024


def _cparams(sem, vmem_mib):
    return pltpu.CompilerParams(dimension_semantics=sem, vmem_limit_bytes=vmem_mib * MiB)


def _rms(x, gain):
    return x * lax.rsqrt(jnp.mean(x * x, axis=-1, keepdims=True) + EPS) * gain


def _silu(x):
    return x * jax.nn.sigmoid(x)


def _ada_kernel(c_ref, w_ref, b_ref, o_ref):
    ca = _silu(c_ref[...])
    o_ref[0] = jnp.dot(ca, w_ref[0], preferred_element_type=F32, precision=lax.Precision.HIGHEST) + b_ref[0]


def _ada(c, w_ada, b_ada):
    B, D = c.shape
    out = pl.pallas_call(
        _ada_kernel,
        out_shape=jax.ShapeDtypeStruct((DEPTH * 6, B, D), F32),
        grid=(DEPTH, 6),
        in_specs=[
            pl.BlockSpec((B, D), lambda l, k: (0, 0)),
            pl.BlockSpec((1, D, D), lambda l, k: (l, 0, k)),
            pl.BlockSpec((1, 1, D), lambda l, k: (l, 0, k)),
        ],
        out_specs=pl.BlockSpec((1, B, D), lambda l, k: (l * 6 + k, 0, 0)),
        compiler_params=_cparams(("arbitrary", "arbitrary"), 32),
        name="ada",
    )(c, w_ada, b_ada.reshape(DEPTH, 1, 6 * D))
    return out.reshape(DEPTH, 6, B, 1, D)


def _mod_spec(layer, k, tiles_per_batch):
    return pl.BlockSpec((None, None, None, 1, D_MODEL),
                        lambda i, *_: (layer, k, i // tiles_per_batch, 0, 0))


def _inproj_kernel(x_ref, gain_ref, sc_ref, sh_ref, w_ref, main_ref, a0_ref, a1_ref, a2_ref, scr_ref):
    tm = x_ref.shape[0]
    h = _rms(x_ref[...], gain_ref[...]) * (1.0 + sc_ref[...]) + sh_ref[...]
    hb = h.astype(BF16)
    for c in range(MAIN_COLS // MXU_N):
        sl = slice(c * MXU_N, (c + 1) * MXU_N)
        main_ref[:, sl] = jnp.dot(hb, w_ref[:, sl], preferred_element_type=F32).astype(BF16)
    att_refs = (a0_ref, a1_ref, a2_ref)
    heads_per_slab = LANE // ATT_HEAD_DIM

    def store_heads(att_ref, r, slot0, val):
        lane = lax.broadcasted_iota(I32, val.shape, 1)
        for k in range(heads_per_slab):
            head = val if k == 0 else pltpu.roll(val, LANE - k * ATT_HEAD_DIM, 1)
            c0 = (slot0 + k) * LANE
            att_ref[r, :, c0:c0 + LANE] = jnp.where(lane < ATT_HEAD_DIM, head, 0.0).astype(BF16)

    slab = 0
    for g, (_, d) in enumerate(ATT_GROUPS):
        for seg in range(3):
            col = MAIN_COLS + g * GROUP_SRC_COLS + seg * ATT_OUT_WIDTH
            res = jnp.dot(hb, w_ref[:, col:col + ATT_OUT_WIDTH], preferred_element_type=F32)
            for s in range(ATT_OUT_WIDTH // LANE):
                part = res[:, s * LANE:(s + 1) * LANE]
                slot0 = seg * HEADS_PER_GROUP + s * heads_per_slab
                if d == 1:
                    store_heads(att_refs[g], 0, slot0, part)
                    continue
                scr_ref[slab] = part
                for r in range(d):
                    store_heads(att_refs[g], r, slot0, scr_ref[slab, pl.ds(r, tm // d, stride=d), :])
                slab += 1


def _inproj(x2, gain, mod, layer, w_in_b, B, S, tm=512):
    R, D = x2.shape
    tpb = S // tm
    n_slabs = sum(3 * (ATT_OUT_WIDTH // LANE) for _, d in ATT_GROUPS if d > 1)

    def att_shape(d):
        return jax.ShapeDtypeStruct((B, d, S // d, GROUP_COLS), BF16)

    def att_spec(d):
        return pl.BlockSpec((None, d, tm // d, GROUP_COLS), lambda i: (i // tpb, 0, i % tpb, 0))

    return pl.pallas_call(
        _inproj_kernel,
        out_shape=(jax.ShapeDtypeStruct((R, MAIN_COLS), BF16),) + tuple(att_shape(d) for _, d in ATT_GROUPS),
        grid=(R // tm,),
        in_specs=[
            pl.BlockSpec((tm, D), lambda i: (i, 0)),
            pl.BlockSpec((1, D), lambda i: (0, 0)),
            _mod_spec(layer, 1, tpb),
            _mod_spec(layer, 0, tpb),
            pl.BlockSpec((D, IN_COLS), lambda i: (0, 0), pipeline_mode=pl.Buffered(1)),
        ],
        out_specs=(pl.BlockSpec((tm, MAIN_COLS), lambda i: (i, 0)),) + tuple(att_spec(d) for _, d in ATT_GROUPS),
        scratch_shapes=[pltpu.VMEM((n_slabs, tm, LANE), F32)],
        compiler_params=_cparams(("arbitrary",), 56),
        name="inproj",
    )(x2, gain, mod, mod, w_in_b)


def _prep_w_in(w):
    b = np.concatenate([[0], np.cumsum(IN_SIZES)])
    w = w.astype(BF16)
    seg = lambda s: w[:, b[s]:b[s + 1]]
    cols = [seg(5), seg(6), seg(7), seg(8), seg(3), seg(4)]
    q_a, k_a, v_a = seg(0) * (ATT_HEAD_DIM ** -0.5), seg(1), seg(2)
    for g in range(len(ATT_GROUPS)):
        gs = slice(g * ATT_OUT_WIDTH, (g + 1) * ATT_OUT_WIDTH)
        cols += [q_a[:, gs], k_a[:, gs], v_a[:, gs]]
    return jnp.concatenate(cols, axis=1)


def _alibi_slopes():
    return [2.0 ** (-ALIBI_MAX * h / N_ATT_HEADS) for h in range(1, N_ATT_HEADS + 1)]


ATT_UNITS = 8


def _attn_kernel(a_ref, bias_ref, o_ref, *, n, dilation):
    nb = n // BLOCK
    win = min(2 * BLOCK, n)
    ones = jnp.ones((win, LANE), BF16)
    lane = lax.broadcasted_iota(I32, (BLOCK, LANE), 1)

    def head_units(u):
        r = u // nb
        i = u % nb
        q0 = pl.multiple_of(i * BLOCK, BLOCK)
        k0 = pl.multiple_of(jnp.clip((i - 1) * BLOCK, 0, n - win), BLOCK)
        t = jnp.minimum(i, 1)
        return [(r, pl.ds(q0, BLOCK), pl.ds(k0, win), t, h) for h in range(HEADS_PER_GROUP)]

    def pair(u2, carry):
        units = sum((head_units(ATT_UNITS * u2 + j) for j in range(ATT_UNITS)), [])
        scores = []
        for r, qrows, krows, t, h in units:
            qs = slice(h * LANE, (h + 1) * LANE)
            ks = slice((HEADS_PER_GROUP + h) * LANE, (HEADS_PER_GROUP + h + 1) * LANE)
            s = lax.dot_general(a_ref[r, qrows, qs], a_ref[r, krows, ks], (((1,), (1,)), ((), ())),
                                preferred_element_type=F32)
            scores.append(s + bias_ref[t, h])
        probs = []
        for s in scores:
            m = jnp.max(s, axis=-1, keepdims=True)
            probs.append((jnp.exp(s - m).astype(BF16), m))
        for (r, qrows, krows, t, h), (p, m) in zip(units, probs):
            qs = slice(h * LANE, (h + 1) * LANE)
            vs = slice((2 * HEADS_PER_GROUP + h) * LANE, (2 * HEADS_PER_GROUP + h + 1) * LANE)
            pv = jnp.dot(p, jnp.concatenate([a_ref[r, krows, vs], ones], axis=1), preferred_element_type=F32)
            den = pv[:, LANE:]
            o_ref[r, qrows, qs] = jnp.where(lane < ATT_HEAD_DIM, pv[:, :LANE] / den, m + jnp.log(den))
        return carry

    lax.fori_loop(0, dilation * nb // ATT_UNITS, pair, 0)


def _attn_bias(n, n_w, d, slopes):
    win = min(2 * BLOCK, n)
    r = np.arange(BLOCK)[:, None]
    c = np.arange(win)[None, :]
    tables = []
    for shift in (0, win - BLOCK):
        dist = shift + r - c
        valid = (dist >= 0) & (dist <= n_w)
        tables.append(np.stack([np.where(valid, -s * (d * dist), NEG_INF) for s in slopes]))
    return jnp.asarray(np.stack(tables), F32)


def _attn_group(att, g):
    window, d = ATT_GROUPS[g]
    B, _, n, _ = att.shape
    assert n % BLOCK == 0 and (d * (n // BLOCK)) % ATT_UNITS == 0
    slopes = _alibi_slopes()[g * HEADS_PER_GROUP:(g + 1) * HEADS_PER_GROUP]
    bias = _attn_bias(n, window // d, d, slopes)
    kern = functools.partial(_attn_kernel, n=n, dilation=d)
    out_spec = pl.BlockSpec((None, d, n, ATT_T_WIDTH), lambda b: (b, 0, 0, 0))
    return pl.pallas_call(
        kern,
        out_shape=jax.ShapeDtypeStruct((B, d, n, ATT_T_WIDTH), F32),
        grid=(B,),
        in_specs=[pl.BlockSpec((None, d, n, GROUP_COLS), lambda b: (b, 0, 0, 0)),
                  pl.BlockSpec(bias.shape, lambda b: (0, 0, 0, 0))],
        out_specs=out_spec,
        compiler_params=_cparams(("arbitrary",), 48),
        name=f"attn{g}",
    )(att, bias)


def _ret_tables(S):
    half = RET_KEY_DIM // 2
    inv_freq = 1.0 / (ROPE_BASE ** (jnp.arange(half, dtype=F32) / half))
    ang = jnp.arange(S).astype(F32)[:, None] * inv_freq[None]
    cos = jnp.cos(ang)
    sin = jnp.sin(ang)
    cos_t = jnp.tile(jnp.concatenate([cos, cos], axis=-1), (1, RET_HEADS))
    sin_t = jnp.tile(jnp.concatenate([-sin, sin], axis=-1), (1, RET_HEADS))
    log_gamma = jnp.log1p(-jnp.exp2(-5.0 - jnp.arange(RET_HEADS, dtype=F32)))
    idx = jnp.arange(RET_CHUNK, dtype=F32)
    rel = idx[:, None] - idx[None, :]
    decay = jnp.where(rel[None] >= 0, jnp.exp(jnp.maximum(rel, 0.0)[None] * log_gamma[:, None, None]), 0.0)
    zeta = jnp.exp((RET_CHUNK - 1 - idx)[None] * log_gamma[:, None])
    xi = jnp.exp((idx + 1)[None] * log_gamma[:, None])
    zeta_t = jnp.repeat(zeta.T, RET_KEY_DIM, axis=1) * (RET_KEY_DIM ** -0.5)
    xi_t = jnp.repeat(xi.T, RET_KEY_DIM, axis=1)
    g_chunk = jnp.exp(RET_CHUNK * log_gamma)
    g_t = jnp.broadcast_to(g_chunk[:, None, None], (RET_HEADS, 1, RET_VAL_DIM))
    return cos_t, sin_t, decay, zeta_t, xi_t, g_t


def _rot_half_swap(t):
    w = t.shape[-1]
    lane = lax.broadcasted_iota(I32, t.shape, 1) % RET_KEY_DIM
    half = RET_KEY_DIM // 2
    return jnp.where(lane < half, pltpu.roll(t, w - half, 1), pltpu.roll(t, half, 1))


RET_ROWS = 1024
RET_PAIR = 4


def _retn_kernel(q_ref, k_ref, v_ref, g_ref, cos_ref, sin_ref, decay_ref, zeta_ref, xi_ref, gch_ref,
                 o_ref, state_ref):
    C = RET_CHUNK

    @pl.when(pl.program_id(1) == 0)
    def _():
        state_ref[...] = jnp.zeros_like(state_ref)

    def prep(c):
        rows = pl.ds(pl.multiple_of(c * C, C), C)
        cos = cos_ref[rows, :]
        sin = sin_ref[rows, :]
        q = q_ref[rows, :].astype(F32)
        k = k_ref[rows, :].astype(F32)
        q = q * cos + _rot_half_swap(q) * sin
        k = k * cos + _rot_half_swap(k) * sin
        qb = q.astype(BF16)
        kb = (k * (RET_KEY_DIM ** -0.5)).astype(BF16)
        q_xi = (q * xi_ref[...]).astype(BF16)
        kz_t = (k * zeta_ref[...]).T
        return rows, qb, kb, q_xi, kz_t

    def group(u, carry):
        chunks = [prep(u * RET_PAIR + i) for i in range(RET_PAIR)]
        ksl = [slice(h * RET_KEY_DIM, (h + 1) * RET_KEY_DIM) for h in range(RET_HEADS)]
        vsl = [slice(h * RET_VAL_DIM, (h + 1) * RET_VAL_DIM) for h in range(RET_HEADS)]
        units = [(c, h) for h in range(RET_HEADS) for c in range(RET_PAIR)]
        scores, kvs = {}, {}
        for c, h in units:
            rows, qb, kb, q_xi, kz_t = chunks[c]
            v = v_ref[rows, vsl[h]]
            scores[c, h] = lax.dot_general(qb[:, ksl[h]], kb[:, ksl[h]], (((1,), (1,)), ((), ())),
                                           preferred_element_type=F32)
            kvs[c, h] = jnp.dot(kz_t[ksl[h], :].astype(BF16), v, preferred_element_type=F32)
        states = {}
        for h in range(RET_HEADS):
            st = state_ref[h]
            for c in range(RET_PAIR):
                states[c, h] = st.astype(BF16)
                st = st * gch_ref[h] + kvs[c, h]
            state_ref[h] = st
        outs = {}
        for c, h in units:
            rows, qb, kb, q_xi, kz_t = chunks[c]
            s = (scores[c, h] * decay_ref[h]).astype(BF16)
            outs[c, h] = (jnp.dot(s, v_ref[rows, vsl[h]], preferred_element_type=F32)
                          + jnp.dot(q_xi[:, ksl[h]], states[c, h], preferred_element_type=F32))
        for c, h in units:
            rows = chunks[c][0]
            o = outs[c, h]
            o = o * lax.rsqrt(jnp.mean(o * o, axis=-1, keepdims=True) + EPS)
            o_ref[rows, vsl[h]] = (_silu(g_ref[rows, vsl[h]].astype(F32)) * o).astype(o_ref.dtype)
        return carry

    lax.fori_loop(0, q_ref.shape[0] // (C * RET_PAIR), group, 0)


def _retention(main3, B, S):
    C = RET_CHUNK
    RS = RET_ROWS
    assert S % RS == 0 and RS % (C * RET_PAIR) == 0
    cos_t, sin_t, decay, zeta_t, xi_t, g_t = _ret_tables(S)
    const2 = lambda b, c: (0, 0)
    const3 = lambda b, c: (0, 0, 0)
    qk0 = 4 * D_MODEL // RET_QK_WIDTH
    out = pl.pallas_call(
        _retn_kernel,
        out_shape=jax.ShapeDtypeStruct((B, S, RET_V_WIDTH), BF16),
        grid=(B, S // RS),
        in_specs=[
            pl.BlockSpec((None, RS, RET_QK_WIDTH), lambda b, c: (b, c, qk0)),
            pl.BlockSpec((None, RS, RET_QK_WIDTH), lambda b, c: (b, c, qk0 + 1)),
            pl.BlockSpec((None, RS, RET_V_WIDTH), lambda b, c: (b, c, 0)),
            pl.BlockSpec((None, RS, RET_V_WIDTH), lambda b, c: (b, c, 1)),
            pl.BlockSpec((RS, RET_QK_WIDTH), lambda b, c: (c, 0)),
            pl.BlockSpec((RS, RET_QK_WIDTH), lambda b, c: (c, 0)),
            pl.BlockSpec((RET_HEADS, C, C), const3),
            pl.BlockSpec((C, RET_QK_WIDTH), const2),
            pl.BlockSpec((C, RET_QK_WIDTH), const2),
            pl.BlockSpec((RET_HEADS, 1, RET_VAL_DIM), const3),
        ],
        out_specs=pl.BlockSpec((None, RS, RET_V_WIDTH), lambda b, c: (b, c, 0)),
        scratch_shapes=[pltpu.VMEM((RET_HEADS, RET_KEY_DIM, RET_VAL_DIM), F32)],
        compiler_params=_cparams(("arbitrary", "arbitrary"), 48),
        name="retn",
    )(main3, main3, main3, main3, cos_t, sin_t, decay, zeta_t, xi_t, g_t)
    return out.reshape(B * S, RET_V_WIDTH)


N_META = 6


def _unstride(ref, scr_ref, slab0, d, tm):
    if d == 1:
        return [ref[0, :, s * LANE:(s + 1) * LANE] for s in range(HEADS_PER_GROUP)]
    for r in range(d):
        for s in range(HEADS_PER_GROUP):
            scr_ref[slab0 + s, pl.ds(r, tm // d, stride=d), :] = ref[r, :, s * LANE:(s + 1) * LANE]
    return [scr_ref[slab0 + s] for s in range(HEADS_PER_GROUP)]


def _postmix_kernel(o0_ref, o1_ref, o2_ref, ret_ref, ga_ref, gr_ref, x_ref,
                    wa_ref, wr_ref, wo_ref, npost_ref, g1_ref, npre_ref, sc2_ref, sh2_ref, *rest, with_router):
    if with_router:
        wrt_ref, brt_ref, tri_ref, xo_ref, h_ref, meta_ref, metat_ref, cnt_ref, scr_ref, carry_ref = rest
    else:
        xo_ref, h_ref, scr_ref = rest
    tm = x_ref.shape[0]
    groups = []
    slab = 0
    for (_, d), o_ref in zip(ATT_GROUPS, (o0_ref, o1_ref, o2_ref)):
        groups.append(_unstride(o_ref, scr_ref, slab, d, tm))
        if d > 1:
            slab += HEADS_PER_GROUP
    head_lane = lax.broadcasted_iota(I32, (tm, LANE), 1)
    slots = []
    for hd in range(HEADS_PER_GROUP):
        outs = [grp[hd] for grp in groups]
        lses = [pltpu.roll(o, LANE // 2, 1) for o in outs]
        mx = functools.reduce(jnp.maximum, lses)
        es = [jnp.exp(l - mx) for l in lses]
        num = sum(e * o for e, o in zip(es, outs))
        slots.append(jnp.where(head_lane < ATT_HEAD_DIM, num / sum(es), 0.0))
    att = jnp.concatenate(slots, axis=1)
    a = jnp.dot(att.astype(BF16), wa_ref[...], preferred_element_type=F32)
    r = jnp.dot(ret_ref[...], wr_ref[...], preferred_element_type=F32)
    merged = jax.nn.sigmoid(ga_ref[...].astype(F32)) * a + jax.nn.sigmoid(gr_ref[...].astype(F32)) * r
    y = jnp.dot(merged.astype(BF16), wo_ref[...], preferred_element_type=F32)
    x = x_ref[...] + g1_ref[...] * _rms(y, npost_ref[...])
    xo_ref[...] = x
    h = _rms(x, npre_ref[...]) * (1.0 + sc2_ref[...]) + sh2_ref[...]
    if not with_router:
        h_ref[...] = h.astype(h_ref.dtype)
        return
    _to_token_tiles(h_ref, h)

    @pl.when(pl.program_id(0) == 0)
    def _():
        carry_ref[...] = jnp.zeros_like(carry_ref)

    h_hi = h.astype(BF16)
    h_lo = (h - h_hi.astype(F32)).astype(BF16)
    w = wrt_ref[...]
    w_hi = w.astype(BF16)
    w_lo = (w - w_hi.astype(F32)).astype(BF16)
    lg = (jnp.dot(h_hi, w_hi, preferred_element_type=F32) + jnp.dot(h_hi, w_lo, preferred_element_type=F32)
          + jnp.dot(h_lo, w_hi, preferred_element_type=F32)) + brt_ref[...]
    lane = lax.broadcasted_iota(I32, lg.shape, 1).astype(F32)
    lg = jnp.where(lane < N_EXPERTS, lg, -jnp.inf)
    m1 = jnp.max(lg, axis=-1, keepdims=True)
    i1 = jnp.min(jnp.where(lg == m1, lane, float(LANE)), axis=-1, keepdims=True)
    lg2 = jnp.where(lane == i1, -jnp.inf, lg)
    m2 = jnp.max(lg2, axis=-1, keepdims=True)
    i2 = jnp.min(jnp.where(lg2 == m2, lane, float(LANE)), axis=-1, keepdims=True)
    t = jnp.exp(m2 - m1)
    w1 = 1.0 / (1.0 + t)
    w2 = t / (1.0 + t)
    oh = jnp.where((lane == i1) | (lane == i2), 1.0, 0.0)
    pos = jnp.dot(tri_ref[...], oh.astype(BF16), preferred_element_type=F32) + carry_ref[...]
    r1 = jnp.sum(jnp.where(lane == i1, pos, 0.0), axis=-1, keepdims=True)
    r2 = jnp.sum(jnp.where(lane == i2, pos, 0.0), axis=-1, keepdims=True)
    carry = carry_ref[...] + jnp.sum(oh, axis=0, keepdims=True)
    carry_ref[...] = carry
    cnt_ref[...] = jnp.broadcast_to(carry, cnt_ref.shape)
    meta = jnp.zeros(lg.shape, F32)
    for j, val in enumerate((i1, i2, r1, r2, w1, w2)):
        meta = jnp.where(lane == j, val, meta)
    meta_ref[...] = meta
    metat_ref[...] = meta.T[:metat_ref.shape[0], :]


def _postmix(outs, ret, main, x2, wa, wr, wo, npost, npre, mod, layer, S, router=None, tm=512):
    R, D = x2.shape
    tpb = S // tm
    row = lambda w: pl.BlockSpec((tm, w), lambda i: (i, 0))
    const = lambda a: pl.BlockSpec(a.shape, lambda i: (0, 0))

    def att_spec(d):
        return pl.BlockSpec((None, d, tm // d, ATT_T_WIDTH), lambda i: (i // tpb, 0, i % tpb, 0))

    specs_att = [att_spec(d) for _, d in ATT_GROUPS]
    in_specs = (specs_att + [
        row(RET_V_WIDTH),
        pl.BlockSpec((tm, D), lambda i: (i, 2)),
        pl.BlockSpec((tm, D), lambda i: (i, 3)),
        row(D), const(wa), const(wr), const(wo), const(npost),
        _mod_spec(layer, 2, tpb), const(npre), _mod_spec(layer, 4, tpb), _mod_spec(layer, 3, tpb)])
    args = list(outs) + [ret, main, main, x2, wa, wr, wo, npost, mod, npre, mod, mod]
    n_slabs = sum(HEADS_PER_GROUP for _, d in ATT_GROUPS if d > 1)
    scratch = [pltpu.VMEM((n_slabs, tm, LANE), F32)]
    if router is None:
        out_shape = [jax.ShapeDtypeStruct((R, D), F32), jax.ShapeDtypeStruct((R, D), BF16)]
        out_specs = [row(D), row(D)]
        sem = ("arbitrary",)
    else:
        tri = jnp.asarray(np.tril(np.ones((tm, tm), np.float32), -1), BF16)
        router = tuple(router) + (tri,)
        in_specs += [const(a) for a in router]
        args += list(router)
        out_shape = [jax.ShapeDtypeStruct((R, D), F32), jax.ShapeDtypeStruct((R * TOK_ROWS, LANE), F32),
                     jax.ShapeDtypeStruct((R, LANE), F32), jax.ShapeDtypeStruct((8, R), F32),
                     jax.ShapeDtypeStruct((8, LANE), F32)]
        out_specs = [row(D), pl.BlockSpec((tm * TOK_ROWS, LANE), lambda i: (i, 0)), row(LANE),
                     pl.BlockSpec((8, tm), lambda i: (0, i)), pl.BlockSpec((8, LANE), lambda i: (0, 0))]
        scratch.append(pltpu.VMEM((1, LANE), F32))
        sem = ("arbitrary",)
    return pl.pallas_call(
        functools.partial(_postmix_kernel, with_router=router is not None),
        out_shape=tuple(out_shape),
        grid=(R // tm,),
        in_specs=in_specs,
        out_specs=tuple(out_specs),
        scratch_shapes=scratch,
        compiler_params=_cparams(sem, 56),
        name="postmix",
    )(*args)


def _swiglu(hb, wg_ref, wu_ref, wd_ref):
    y = None
    for c in range(D_FF // MXU_N):
        cols = slice(c * MXU_N, (c + 1) * MXU_N)
        act = (_silu(jnp.dot(hb, wg_ref[:, cols], preferred_element_type=F32))
               * jnp.dot(hb, wu_ref[:, cols], preferred_element_type=F32))
        part = jnp.dot(act.astype(BF16), wd_ref[cols, :], preferred_element_type=F32)
        y = part if y is None else y + part
    return y


def _ffn_kernel(h_ref, x_ref, wg_ref, wu_ref, wd_ref, npost_ref, g2_ref, o_ref):
    y = _swiglu(h_ref[...], wg_ref, wu_ref, wd_ref)
    o_ref[...] = x_ref[...] + g2_ref[...] * _rms(y, npost_ref[...])


def _ffn(h2, x2, wg, wu, wd, npost, mod, layer, S, tm=512):
    R, D = x2.shape
    tpb = S // tm
    resident = lambda a: pl.BlockSpec(a.shape, lambda i: (0, 0), pipeline_mode=pl.Buffered(1))
    return pl.pallas_call(
        _ffn_kernel,
        out_shape=jax.ShapeDtypeStruct((R, D), F32),
        grid=(R // tm,),
        in_specs=[
            pl.BlockSpec((tm, D), lambda i: (i, 0)),
            pl.BlockSpec((tm, D), lambda i: (i, 0)),
            resident(wg), resident(wu), resident(wd),
            pl.BlockSpec((1, D), lambda i: (0, 0)),
            _mod_spec(layer, 5, tpb),
        ],
        out_specs=pl.BlockSpec((tm, D), lambda i: (i, 0)),
        compiler_params=_cparams(("arbitrary",), 48),
        name="ffn",
    )(h2, x2, wg, wu, wd, npost, mod)


MOE_TM = 512
ROW_UNROLL = 8
DMA_QUEUES = 2


def _invperm_kernel(n_ref, dest_ref, src_ref, zeros_ref, sem):
    zeros_ref[...] = jnp.zeros_like(zeros_ref)
    fill = pltpu.make_async_copy(zeros_ref, src_ref, sem)
    fill.start()
    fill.wait()

    def put(g, c):
        for u in range(ROW_UNROLL):
            a = g * ROW_UNROLL + u
            src_ref[dest_ref[a]] = a
        return c

    lax.fori_loop(0, n_ref[0], put, 0)


def _invperm(dest, P):
    assert dest.shape[0] % ROW_UNROLL == 0
    trips = jnp.array([dest.shape[0] // ROW_UNROLL], I32)
    return pl.pallas_call(
        _invperm_kernel,
        out_shape=jax.ShapeDtypeStruct((P,), I32),
        in_specs=[pl.BlockSpec(memory_space=pltpu.SMEM), pl.BlockSpec(memory_space=pltpu.SMEM)],
        out_specs=pl.BlockSpec(memory_space=pltpu.SMEM),
        scratch_shapes=[pltpu.VMEM((P,), I32), pltpu.SemaphoreType.DMA],
        name="invperm",
    )(trips, dest)


TOK_ROWS = D_MODEL // LANE


def _to_token_tiles(ref, val):
    tm = val.shape[0]
    for s in range(TOK_ROWS):
        ref[pl.ds(s, tm, stride=TOK_ROWS), :] = val[:, s * LANE:(s + 1) * LANE]


def _token_tile_slabs(read, tm):
    return [read(pl.ds(s, tm, stride=TOK_ROWS)) for s in range(TOK_ROWS)]


def _expert_kernel(te_ref, nu_ref, src_ref, h_hbm, wg_ref, wu_ref, wd_ref, o_ref, hb_ref, hbuf_ref, sem, *, n_tok):
    j = pl.program_id(0)
    nu = nu_ref[0]
    tm = hb_ref.shape[0]

    def row_copy(row, slot, i):
        src = h_hbm.at[pl.ds(pl.multiple_of(row * TOK_ROWS, TOK_ROWS), TOK_ROWS), :]
        dst = hbuf_ref.at[slot, pl.ds(pl.multiple_of(i * TOK_ROWS, TOK_ROWS), TOK_ROWS), :]
        return pltpu.make_async_copy(src, dst, sem.at[slot])

    def issue(tile, slot):
        def body(g, c):
            for k in range(DMA_QUEUES):
                i = g * DMA_QUEUES + k
                row_copy(src_ref[tile * tm + i] & (n_tok - 1), slot, i).start(priority=k)
            return c

        lax.fori_loop(0, tm // DMA_QUEUES, body, 0, unroll=ROW_UNROLL // DMA_QUEUES)

    def wait(slot):
        def body(i, c):
            row_copy(0, slot, 0).wait()
            return c

        lax.fori_loop(0, tm, body, 0, unroll=ROW_UNROLL)

    slot = j % 2

    @pl.when(j == 0)
    def _():
        issue(0, 0)

    @pl.when(j <= nu)
    def _():
        wait(slot)

    @pl.when(j < nu)
    def _():
        for s, slab in enumerate(_token_tile_slabs(lambda rows: hbuf_ref[slot, rows, :], tm)):
            hb_ref[:, s * LANE:(s + 1) * LANE] = slab.astype(BF16)
        issue(jnp.minimum(j + 1, pl.num_programs(0) - 1), 1 - slot)
        _to_token_tiles(o_ref, _swiglu(hb_ref[...], wg_ref, wu_ref, wd_ref))

    @pl.when((j == pl.num_programs(0) - 1) & (j < nu))
    def _():
        wait(1 - slot)

    @pl.when(j >= nu)
    def _():
        o_ref[...] = jnp.zeros_like(o_ref)


def _experts(tile_expert, n_used, src, h2t, wg, wu, wd, n_tiles, tm):
    R, D = h2t.shape[0] // TOK_ROWS, D_MODEL
    assert R & (R - 1) == 0
    expert = lambda j, te, nu: te[jnp.minimum(j, nu[0] - 1)]
    return pl.pallas_call(
        functools.partial(_expert_kernel, n_tok=R),
        out_shape=jax.ShapeDtypeStruct((n_tiles * tm * TOK_ROWS, LANE), F32),
        grid_spec=pltpu.PrefetchScalarGridSpec(
            num_scalar_prefetch=3,
            grid=(n_tiles,),
            in_specs=[
                pl.BlockSpec(memory_space=pl.ANY),
                pl.BlockSpec((None, D, D_FF), lambda j, te, nu, src: (expert(j, te, nu), 0, 0)),
                pl.BlockSpec((None, D, D_FF), lambda j, te, nu, src: (expert(j, te, nu), 0, 0)),
                pl.BlockSpec((None, D_FF, D), lambda j, te, nu, src: (expert(j, te, nu), 0, 0)),
            ],
            out_specs=pl.BlockSpec((tm * TOK_ROWS, LANE), lambda j, te, nu, src: (j, 0)),
            scratch_shapes=[pltpu.VMEM((tm, D), BF16), pltpu.VMEM((2, tm * TOK_ROWS, LANE), F32),
                            pltpu.SemaphoreType.DMA((2,))],
        ),
        compiler_params=_cparams(("arbitrary",), 56),
        name="moe_experts",
    )(tile_expert, n_used, src, h2t, wg, wu, wd)


def _combine_kernel(d1_ref, d2_ref, y_hbm, meta_ref, x_ref, npost_ref, g2_ref, o_ref, ya_ref, yb_ref, sem):
    tm = x_ref.shape[0]
    step = pl.program_id(0)
    slot = step % 2

    def row_copy(row, dst_ref, s, r):
        src = y_hbm.at[pl.ds(pl.multiple_of(row * TOK_ROWS, TOK_ROWS), TOK_ROWS), :]
        dst = dst_ref.at[s, pl.ds(pl.multiple_of(r * TOK_ROWS, TOK_ROWS), TOK_ROWS), :]
        return pltpu.make_async_copy(src, dst, sem.at[s])

    def issue(tile, s):
        def body(r, c):
            t = tile * tm + r
            row_copy(d1_ref[t], ya_ref, s, r).start(priority=0)
            row_copy(d2_ref[t], yb_ref, s, r).start(priority=1)
            return c

        lax.fori_loop(0, tm, body, 0, unroll=ROW_UNROLL)

    @pl.when(step == 0)
    def _():
        issue(0, 0)

    @pl.when(step + 1 < pl.num_programs(0))
    def _():
        issue(step + 1, 1 - slot)

    def wait(r, c):
        row_copy(0, ya_ref, slot, 0).wait()
        row_copy(0, yb_ref, slot, 0).wait()
        return c

    lax.fori_loop(0, tm, wait, 0, unroll=ROW_UNROLL)
    meta = meta_ref[...]
    w1, w2 = meta[:, 4:5], meta[:, 5:6]
    slabs_a = _token_tile_slabs(lambda rows: ya_ref[slot, rows, :], tm)
    slabs_b = _token_tile_slabs(lambda rows: yb_ref[slot, rows, :], tm)
    y = jnp.concatenate([w1 * a + w2 * b for a, b in zip(slabs_a, slabs_b)], axis=1)
    o_ref[...] = x_ref[...] + g2_ref[...] * _rms(y, npost_ref[...])


def _combine(dest1, dest2, sorted_y, meta, x2, npost, mod, layer, S, tm=512):
    R, D = x2.shape
    tpb = S // tm
    return pl.pallas_call(
        _combine_kernel,
        out_shape=jax.ShapeDtypeStruct((R, D), F32),
        grid_spec=pltpu.PrefetchScalarGridSpec(
            num_scalar_prefetch=2,
            grid=(R // tm,),
            in_specs=[
                pl.BlockSpec(memory_space=pl.ANY),
                pl.BlockSpec((tm, LANE), lambda i, *_: (i, 0)),
                pl.BlockSpec((tm, D), lambda i, *_: (i, 0)),
                pl.BlockSpec((1, D), lambda i, *_: (0, 0)),
                _mod_spec(layer, 5, tpb),
            ],
            out_specs=pl.BlockSpec((tm, D), lambda i, *_: (i, 0)),
            scratch_shapes=[pltpu.VMEM((2, tm * TOK_ROWS, LANE), F32), pltpu.VMEM((2, tm * TOK_ROWS, LANE), F32),
                            pltpu.SemaphoreType.DMA((2,))],
        ),
        compiler_params=_cparams(("arbitrary",), 40),
        name="moe_combine",
    )(dest1, dest2, sorted_y, meta, x2, npost, mod)


def _moe(h2, meta, meta_t, counts, x2, wg, wu, wd, npost, mod, layer, S):
    R, D = x2.shape
    tm = MOE_TM
    n_tiles = (TOP_K * R + N_EXPERTS * (tm - 1)) // tm + 1
    i1, i2, r1, r2 = (meta_t[j].astype(I32) for j in range(4))
    cnt = counts[0, :N_EXPERTS].astype(I32)
    padded = (cnt + tm - 1) // tm * tm
    end = jnp.cumsum(padded)
    start = end - padded
    eye = jnp.arange(N_EXPERTS, dtype=I32)[:, None]
    dest1 = jnp.sum(jnp.where(i1[None, :] == eye, start[:, None], 0), axis=0) + r1
    dest2 = jnp.sum(jnp.where(i2[None, :] == eye, start[:, None], 0), axis=0) + r2
    tile_row0 = jnp.arange(n_tiles, dtype=I32) * tm
    tile_expert = jnp.minimum(jnp.sum(tile_row0[:, None] >= end[None, :], axis=1), N_EXPERTS - 1).astype(I32)
    n_used = (end[-1:] // tm).astype(I32)
    src = _invperm(jnp.concatenate([dest1, dest2]), n_tiles * tm)
    sorted_y = _experts(tile_expert, n_used, src, h2, wg, wu, wd, n_tiles, tm)
    return _combine(dest1, dest2, sorted_y, meta, x2, npost, mod, layer, S)


def kernel(x, c, w_ada, b_ada, norm_pre_mix, w_in, w_att_proj, w_ret_proj, w_mix_out, norm_post_mix, norm_pre_ffn, w_ffn_gate, w_ffn_up, w_ffn_down, w_router, b_router, w_exp_gate, w_exp_up, w_exp_down, norm_post_ffn):
    B, S, D = x.shape
    R = B * S
    mod = _ada(c, w_ada, b_ada)
    x2 = x.reshape(R, D)
    row = lambda v: v.reshape(1, D)
    for layer in range(DEPTH):
        main, *atts = _inproj(x2, row(norm_pre_mix[layer]), mod, layer, _prep_w_in(w_in[layer]), B, S)
        outs = [_attn_group(atts[g], g) for g in range(len(ATT_GROUPS))]
        ret = _retention(main.reshape(B, S, MAIN_COLS), B, S)
        j = layer // 2
        router = None
        if layer % 2 == 1:
            wrt = jnp.zeros((D, LANE), F32).at[:, :N_EXPERTS].set(w_router[j])
            brt = jnp.zeros((1, LANE), F32).at[0, :N_EXPERTS].set(b_router[j])
            router = (wrt, brt)
        wa = jnp.pad(w_att_proj[layer].reshape(HEADS_PER_GROUP, ATT_HEAD_DIM, D),
                     ((0, 0), (0, LANE - ATT_HEAD_DIM), (0, 0))).reshape(ATT_T_WIDTH, D)
        res = _postmix(outs, ret, main, x2, wa.astype(BF16), w_ret_proj[layer].astype(BF16),
                       w_mix_out[layer].astype(BF16), row(norm_post_mix[layer]), row(norm_pre_ffn[layer]),
                       mod, layer, S, router=router)
        if layer % 2 == 0:
            x2, h2 = res
            x2 = _ffn(h2, x2, w_ffn_gate[j].astype(BF16), w_ffn_up[j].astype(BF16), w_ffn_down[j].astype(BF16),
                      row(norm_post_ffn[layer]), mod, layer, S)
        else:
            x2, h2, meta, meta_t, counts = res
            x2 = _moe(h2, meta, meta_t, counts, x2, w_exp_gate[j].astype(BF16), w_exp_up[j].astype(BF16),
                      w_exp_down[j].astype(BF16), row(norm_post_ffn[layer]), mod, layer, S)
    return x2.reshape(B, S, D)
```

```python
import functools

import numpy as np
import jax
import jax.numpy as jnp
from jax import lax
from jax.experimental import pallas as pl
from jax.experimental.pallas import tpu as pltpu

F32 = jnp.float32
BF16 = jnp.bfloat16
I32 = jnp.int32

D_MODEL = 1024
DEPTH = 2
ATT_GROUPS = ((128, 1), (512, 4), (2048, 16))
HEADS_PER_GROUP = 4
N_ATT_HEADS = HEADS_PER_GROUP * len(ATT_GROUPS)
ATT_HEAD_DIM = 64
ATT_WIDTH = N_ATT_HEADS * ATT_HEAD_DIM
ATT_OUT_WIDTH = HEADS_PER_GROUP * ATT_HEAD_DIM
BLOCK = 128
ALIBI_MAX = 8.0
NEG_INF = -1e30
RET_HEADS = 8
RET_KEY_DIM = 64
RET_VAL_DIM = 128
RET_QK_WIDTH = RET_HEADS * RET_KEY_DIM
RET_V_WIDTH = RET_HEADS * RET_VAL_DIM
RET_CHUNK = 128
ROPE_BASE = 10000.0
D_FF = 2816
N_EXPERTS = 8
TOP_K = 2
EPS = 1e-6
IN_SIZES = (ATT_WIDTH, ATT_WIDTH, ATT_WIDTH, RET_QK_WIDTH, RET_QK_WIDTH, RET_V_WIDTH, RET_V_WIDTH, D_MODEL, D_MODEL)
IN_COLS = sum(IN_SIZES)

LANE = 128
MAIN_COLS = 4 * D_MODEL + 2 * RET_QK_WIDTH
GROUP_SRC_COLS = 3 * ATT_OUT_WIDTH
GROUP_SLOTS = 3 * HEADS_PER_GROUP
GROUP_COLS = GROUP_SLOTS * LANE
ATT_T_WIDTH = HEADS_PER_GROUP * LANE
MXU_N = 256

MiB = 1024 * 1024


def _cparams(sem, vmem_mib):
    return pltpu.CompilerParams(dimension_semantics=sem, vmem_limit_bytes=vmem_mib * MiB)


def _rms(x, gain):
    return x * lax.rsqrt(jnp.mean(x * x, axis=-1, keepdims=True) + EPS) * gain


def _silu(x):
    return x * jax.nn.sigmoid(x)


def _ada_kernel(c_ref, w_ref, b_ref, o_ref):
    ca = _silu(c_ref[...])
    o_ref[0] = jnp.dot(ca, w_ref[0], preferred_element_type=F32, precision=lax.Precision.HIGHEST) + b_ref[0]


def _ada(c, w_ada, b_ada):
    B, D = c.shape
    out = pl.pallas_call(
        _ada_kernel,
        out_shape=jax.ShapeDtypeStruct((DEPTH * 6, B, D), F32),
        grid=(DEPTH, 6),
        in_specs=[
            pl.BlockSpec((B, D), lambda l, k: (0, 0)),
            pl.BlockSpec((1, D, D), lambda l, k: (l, 0, k)),
            pl.BlockSpec((1, 1, D), lambda l, k: (l, 0, k)),
        ],
        out_specs=pl.BlockSpec((1, B, D), lambda l, k: (l * 6 + k, 0, 0)),
        compiler_params=_cparams(("arbitrary", "arbitrary"), 32),
        name="ada",
    )(c, w_ada, b_ada.reshape(DEPTH, 1, 6 * D))
    return out.reshape(DEPTH, 6, B, 1, D)


def _mod_spec(layer, k, tiles_per_batch):
    return pl.BlockSpec((None, None, None, 1, D_MODEL),
                        lambda i, *_: (layer, k, i // tiles_per_batch, 0, 0))


def _inproj_kernel(x_ref, gain_ref, sc_ref, sh_ref, w_ref, main_ref, a0_ref, a1_ref, a2_ref, scr_ref):
    tm = x_ref.shape[0]
    h = _rms(x_ref[...], gain_ref[...]) * (1.0 + sc_ref[...]) + sh_ref[...]
    hb = h.astype(BF16)
    for c in range(MAIN_COLS // MXU_N):
        sl = slice(c * MXU_N, (c + 1) * MXU_N)
        main_ref[:, sl] = jnp.dot(hb, w_ref[:, sl], preferred_element_type=F32).astype(BF16)
    att_refs = (a0_ref, a1_ref, a2_ref)
    heads_per_slab = LANE // ATT_HEAD_DIM

    def store_heads(att_ref, r, slot0, val):
        lane = lax.broadcasted_iota(I32, val.shape, 1)
        for k in range(heads_per_slab):
            head = val if k == 0 else pltpu.roll(val, LANE - k * ATT_HEAD_DIM, 1)
            c0 = (slot0 + k) * LANE
            att_ref[r, :, c0:c0 + LANE] = jnp.where(lane < ATT_HEAD_DIM, head, 0.0).astype(BF16)

    slab = 0
    for g, (_, d) in enumerate(ATT_GROUPS):
        for seg in range(3):
            col = MAIN_COLS + g * GROUP_SRC_COLS + seg * ATT_OUT_WIDTH
            res = jnp.dot(hb, w_ref[:, col:col + ATT_OUT_WIDTH], preferred_element_type=F32)
            for s in range(ATT_OUT_WIDTH // LANE):
                part = res[:, s * LANE:(s + 1) * LANE]
                slot0 = seg * HEADS_PER_GROUP + s * heads_per_slab
                if d == 1:
                    store_heads(att_refs[g], 0, slot0, part)
                    continue
                scr_ref[slab] = part
                for r in range(d):
                    store_heads(att_refs[g], r, slot0, scr_ref[slab, pl.ds(r, tm // d, stride=d), :])
                slab += 1


def _inproj(x2, gain, mod, layer, w_in_b, B, S, tm=512):
    R, D = x2.shape
    tpb = S // tm
    n_slabs = sum(3 * (ATT_OUT_WIDTH // LANE) for _, d in ATT_GROUPS if d > 1)

    def att_shape(d):
        return jax.ShapeDtypeStruct((B, d, S // d, GROUP_COLS), BF16)

    def att_spec(d):
        return pl.BlockSpec((None, d, tm // d, GROUP_COLS), lambda i: (i // tpb, 0, i % tpb, 0))

    return pl.pallas_call(
        _inproj_kernel,
        out_shape=(jax.ShapeDtypeStruct((R, MAIN_COLS), BF16),) + tuple(att_shape(d) for _, d in ATT_GROUPS),
        grid=(R // tm,),
        in_specs=[
            pl.BlockSpec((tm, D), lambda i: (i, 0)),
            pl.BlockSpec((1, D), lambda i: (0, 0)),
            _mod_spec(layer, 1, tpb),
            _mod_spec(layer, 0, tpb),
            pl.BlockSpec((D, IN_COLS), lambda i: (0, 0), pipeline_mode=pl.Buffered(1)),
        ],
        out_specs=(pl.BlockSpec((tm, MAIN_COLS), lambda i: (i, 0)),) + tuple(att_spec(d) for _, d in ATT_GROUPS),
        scratch_shapes=[pltpu.VMEM((n_slabs, tm, LANE), F32)],
        compiler_params=_cparams(("arbitrary",), 56),
        name="inproj",
    )(x2, gain, mod, mod, w_in_b)


def _prep_w_in(w):
    b = np.concatenate([[0], np.cumsum(IN_SIZES)])
    w = w.astype(BF16)
    seg = lambda s: w[:, b[s]:b[s + 1]]
    cols = [seg(5), seg(6), seg(7), seg(8), seg(3), seg(4)]
    q_a, k_a, v_a = seg(0) * (ATT_HEAD_DIM ** -0.5), seg(1), seg(2)
    for g in range(len(ATT_GROUPS)):
        gs = slice(g * ATT_OUT_WIDTH, (g + 1) * ATT_OUT_WIDTH)
        cols += [q_a[:, gs], k_a[:, gs], v_a[:, gs]]
    return jnp.concatenate(cols, axis=1)


def _alibi_slopes():
    return [2.0 ** (-ALIBI_MAX * h / N_ATT_HEADS) for h in range(1, N_ATT_HEADS + 1)]


ATT_UNITS = 16


def _attn_kernel(a_ref, bias_ref, o_ref, *, n, dilation):
    nb = n // BLOCK
    win = min(2 * BLOCK, n)
    ones = jnp.ones((win, LANE), BF16)
    lane = lax.broadcasted_iota(I32, (BLOCK, LANE), 1)

    def head_units(u):
        r = u // nb
        i = u % nb
        q0 = pl.multiple_of(i * BLOCK, BLOCK)
        k0 = pl.multiple_of(jnp.clip((i - 1) * BLOCK, 0, n - win), BLOCK)
        t = jnp.minimum(i, 1)
        return [(r, pl.ds(q0, BLOCK), pl.ds(k0, win), t, h) for h in range(HEADS_PER_GROUP)]

    def pair(u2, carry):
        units = sum((head_units(ATT_UNITS * u2 + j) for j in range(ATT_UNITS)), [])
        scores = []
        for r, qrows, krows, t, h in units:
            qs = slice(h * LANE, (h + 1) * LANE)
            ks = slice((HEADS_PER_GROUP + h) * LANE, (HEADS_PER_GROUP + h + 1) * LANE)
            s = lax.dot_general(a_ref[r, qrows, qs], a_ref[r, krows, ks], (((1,), (1,)), ((), ())),
                                preferred_element_type=F32)
            scores.append(s + bias_ref[t, h])
        probs = []
        for s in scores:
            m = jnp.max(s, axis=-1, keepdims=True)
            probs.append((jnp.exp(s - m).astype(BF16), m))
        for (r, qrows, krows, t, h), (p, m) in zip(units, probs):
            qs = slice(h * LANE, (h + 1) * LANE)
            vs = slice((2 * HEADS_PER_GROUP + h) * LANE, (2 * HEADS_PER_GROUP + h + 1) * LANE)
            pv = jnp.dot(p, jnp.concatenate([a_ref[r, krows, vs], ones], axis=1), preferred_element_type=F32)
            den = pv[:, LANE:]
            o_ref[r, qrows, qs] = jnp.where(lane < ATT_HEAD_DIM, pv[:, :LANE] / den, m + jnp.log(den))
        return carry

    lax.fori_loop(0, dilation * nb // ATT_UNITS, pair, 0)


def _attn_bias(n, n_w, d, slopes):
    win = min(2 * BLOCK, n)
    r = np.arange(BLOCK)[:, None]
    c = np.arange(win)[None, :]
    tables = []
    for shift in (0, win - BLOCK):
        dist = shift + r - c
        valid = (dist >= 0) & (dist <= n_w)
        tables.append(np.stack([np.where(valid, -s * (d * dist), NEG_INF) for s in slopes]))
    return jnp.asarray(np.stack(tables), F32)


def _attn_group(att, g):
    window, d = ATT_GROUPS[g]
    B, _, n, _ = att.shape
    assert n % BLOCK == 0 and (d * (n // BLOCK)) % ATT_UNITS == 0
    slopes = _alibi_slopes()[g * HEADS_PER_GROUP:(g + 1) * HEADS_PER_GROUP]
    bias = _attn_bias(n, window // d, d, slopes)
    kern = functools.partial(_attn_kernel, n=n, dilation=d)
    out_spec = pl.BlockSpec((None, d, n, ATT_T_WIDTH), lambda b: (b, 0, 0, 0))
    return pl.pallas_call(
        kern,
        out_shape=jax.ShapeDtypeStruct((B, d, n, ATT_T_WIDTH), F32),
        grid=(B,),
        in_specs=[pl.BlockSpec((None, d, n, GROUP_COLS), lambda b: (b, 0, 0, 0)),
                  pl.BlockSpec(bias.shape, lambda b: (0, 0, 0, 0))],
        out_specs=out_spec,
        compiler_params=_cparams(("arbitrary",), 48),
        name=f"attn{g}",
    )(att, bias)


def _ret_tables(S):
    half = RET_KEY_DIM // 2
    inv_freq = 1.0 / (ROPE_BASE ** (jnp.arange(half, dtype=F32) / half))
    ang = jnp.arange(S).astype(F32)[:, None] * inv_freq[None]
    cos = jnp.cos(ang)
    sin = jnp.sin(ang)
    cos_t = jnp.tile(jnp.concatenate([cos, cos], axis=-1), (1, RET_HEADS))
    sin_t = jnp.tile(jnp.concatenate([-sin, sin], axis=-1), (1, RET_HEADS))
    log_gamma = jnp.log1p(-jnp.exp2(-5.0 - jnp.arange(RET_HEADS, dtype=F32)))
    idx = jnp.arange(RET_CHUNK, dtype=F32)
    rel = idx[:, None] - idx[None, :]
    decay = jnp.where(rel[None] >= 0, jnp.exp(jnp.maximum(rel, 0.0)[None] * log_gamma[:, None, None]), 0.0)
    zeta = jnp.exp((RET_CHUNK - 1 - idx)[None] * log_gamma[:, None])
    xi = jnp.exp((idx + 1)[None] * log_gamma[:, None])
    zeta_t = jnp.repeat(zeta.T, RET_KEY_DIM, axis=1) * (RET_KEY_DIM ** -0.5)
    xi_t = jnp.repeat(xi.T, RET_KEY_DIM, axis=1)
    g_chunk = jnp.exp(RET_CHUNK * log_gamma)
    g_t = jnp.broadcast_to(g_chunk[:, None, None], (RET_HEADS, 1, RET_VAL_DIM))
    return cos_t, sin_t, decay, zeta_t, xi_t, g_t


def _rot_half_swap(t):
    w = t.shape[-1]
    lane = lax.broadcasted_iota(I32, t.shape, 1) % RET_KEY_DIM
    half = RET_KEY_DIM // 2
    return jnp.where(lane < half, pltpu.roll(t, w - half, 1), pltpu.roll(t, half, 1))


RET_ROWS = 1024
RET_PAIR = 4


def _retn_kernel(q_ref, k_ref, v_ref, g_ref, cos_ref, sin_ref, decay_ref, zeta_ref, xi_ref, gch_ref,
                 o_ref, state_ref):
    C = RET_CHUNK

    @pl.when(pl.program_id(1) == 0)
    def _():
        state_ref[...] = jnp.zeros_like(state_ref)

    def prep(c):
        rows = pl.ds(pl.multiple_of(c * C, C), C)
        cos = cos_ref[rows, :]
        sin = sin_ref[rows, :]
        q = q_ref[rows, :].astype(F32)
        k = k_ref[rows, :].astype(F32)
        q = q * cos + _rot_half_swap(q) * sin
        k = k * cos + _rot_half_swap(k) * sin
        qb = q.astype(BF16)
        kb = (k * (RET_KEY_DIM ** -0.5)).astype(BF16)
        q_xi = (q * xi_ref[...]).astype(BF16)
        kz_t = (k * zeta_ref[...]).T
        return rows, qb, kb, q_xi, kz_t

    def group(u, carry):
        chunks = [prep(u * RET_PAIR + i) for i in range(RET_PAIR)]
        ksl = [slice(h * RET_KEY_DIM, (h + 1) * RET_KEY_DIM) for h in range(RET_HEADS)]
        vsl = [slice(h * RET_VAL_DIM, (h + 1) * RET_VAL_DIM) for h in range(RET_HEADS)]
        units = [(c, h) for h in range(RET_HEADS) for c in range(RET_PAIR)]
        scores, kvs = {}, {}
        for c, h in units:
            rows, qb, kb, q_xi, kz_t = chunks[c]
            v = v_ref[rows, vsl[h]]
            scores[c, h] = lax.dot_general(qb[:, ksl[h]], kb[:, ksl[h]], (((1,), (1,)), ((), ())),
                                           preferred_element_type=F32)
            kvs[c, h] = jnp.dot(kz_t[ksl[h], :].astype(BF16), v, preferred_element_type=F32)
        states = {}
        for h in range(RET_HEADS):
            st = state_ref[h]
            for c in range(RET_PAIR):
                states[c, h] = st.astype(BF16)
                st = st * gch_ref[h] + kvs[c, h]
            state_ref[h] = st
        outs = {}
        for c, h in units:
            rows, qb, kb, q_xi, kz_t = chunks[c]
            s = (scores[c, h] * decay_ref[h]).astype(BF16)
            outs[c, h] = (jnp.dot(s, v_ref[rows, vsl[h]], preferred_element_type=F32)
                          + jnp.dot(q_xi[:, ksl[h]], states[c, h], preferred_element_type=F32))
        for c, h in units:
            rows = chunks[c][0]
            o = outs[c, h]
            o = o * lax.rsqrt(jnp.mean(o * o, axis=-1, keepdims=True) + EPS)
            o_ref[rows, vsl[h]] = (_silu(g_ref[rows, vsl[h]].astype(F32)) * o).astype(o_ref.dtype)
        return carry

    lax.fori_loop(0, q_ref.shape[0] // (C * RET_PAIR), group, 0)


def _retention(main3, B, S):
    C = RET_CHUNK
    RS = RET_ROWS
    assert S % RS == 0 and RS % (C * RET_PAIR) == 0
    cos_t, sin_t, decay, zeta_t, xi_t, g_t = _ret_tables(S)
    const2 = lambda b, c: (0, 0)
    const3 = lambda b, c: (0, 0, 0)
    qk0 = 4 * D_MODEL // RET_QK_WIDTH
    out = pl.pallas_call(
        _retn_kernel,
        out_shape=jax.ShapeDtypeStruct((B, S, RET_V_WIDTH), BF16),
        grid=(B, S // RS),
        in_specs=[
            pl.BlockSpec((None, RS, RET_QK_WIDTH), lambda b, c: (b, c, qk0)),
            pl.BlockSpec((None, RS, RET_QK_WIDTH), lambda b, c: (b, c, qk0 + 1)),
            pl.BlockSpec((None, RS, RET_V_WIDTH), lambda b, c: (b, c, 0)),
            pl.BlockSpec((None, RS, RET_V_WIDTH), lambda b, c: (b, c, 1)),
            pl.BlockSpec((RS, RET_QK_WIDTH), lambda b, c: (c, 0)),
            pl.BlockSpec((RS, RET_QK_WIDTH), lambda b, c: (c, 0)),
            pl.BlockSpec((RET_HEADS, C, C), const3),
            pl.BlockSpec((C, RET_QK_WIDTH), const2),
            pl.BlockSpec((C, RET_QK_WIDTH), const2),
            pl.BlockSpec((RET_HEADS, 1, RET_VAL_DIM), const3),
        ],
        out_specs=pl.BlockSpec((None, RS, RET_V_WIDTH), lambda b, c: (b, c, 0)),
        scratch_shapes=[pltpu.VMEM((RET_HEADS, RET_KEY_DIM, RET_VAL_DIM), F32)],
        compiler_params=_cparams(("arbitrary", "arbitrary"), 48),
        name="retn",
    )(main3, main3, main3, main3, cos_t, sin_t, decay, zeta_t, xi_t, g_t)
    return out.reshape(B * S, RET_V_WIDTH)


N_META = 6
POSTMIX_PARTS = 2


def _unstride(ref, scr_ref, slab0, d, tm):
    if d == 1:
        return [ref[0, :, s * LANE:(s + 1) * LANE] for s in range(HEADS_PER_GROUP)]
    for r in range(d):
        for s in range(HEADS_PER_GROUP):
            scr_ref[slab0 + s, pl.ds(r, tm // d, stride=d), :] = ref[r, :, s * LANE:(s + 1) * LANE]
    return [scr_ref[slab0 + s] for s in range(HEADS_PER_GROUP)]


def _postmix_kernel(o0_ref, o1_ref, o2_ref, ret_ref, ga_ref, gr_ref, x_ref,
                    wa_ref, wr_ref, wo_ref, npost_ref, g1_ref, npre_ref, sc2_ref, sh2_ref, *rest, with_router):
    if with_router:
        wrt_ref, brt_ref, tri_ref, xo_ref, h_ref, meta_ref, metat_ref, cnt_ref, scr_ref, carry_ref = rest
    else:
        xo_ref, h_ref, scr_ref = rest
    tm = x_ref.shape[0]
    groups = []
    slab = 0
    for (_, d), o_ref in zip(ATT_GROUPS, (o0_ref, o1_ref, o2_ref)):
        groups.append(_unstride(o_ref, scr_ref, slab, d, tm))
        if d > 1:
            slab += HEADS_PER_GROUP
    pr = tm // POSTMIX_PARTS
    head_lane = lax.broadcasted_iota(I32, (pr, LANE), 1)
    h_parts = []
    for part in range(POSTMIX_PARTS):
        rows = slice(part * pr, (part + 1) * pr)
        slots = []
        for hd in range(HEADS_PER_GROUP):
            outs = [grp[hd][rows] for grp in groups]
            lses = [pltpu.roll(o, LANE // 2, 1) for o in outs]
            mx = functools.reduce(jnp.maximum, lses)
            es = [jnp.exp(l - mx) for l in lses]
            num = sum(e * o for e, o in zip(es, outs))
            slots.append(jnp.where(head_lane < ATT_HEAD_DIM, num / sum(es), 0.0))
        att = jnp.concatenate(slots, axis=1)
        a = jnp.dot(att.astype(BF16), wa_ref[...], preferred_element_type=F32)
        r = jnp.dot(ret_ref[rows, :], wr_ref[...], preferred_element_type=F32)
        merged = (jax.nn.sigmoid(ga_ref[rows, :].astype(F32)) * a
                  + jax.nn.sigmoid(gr_ref[rows, :].astype(F32)) * r)
        y = jnp.dot(merged.astype(BF16), wo_ref[...], preferred_element_type=F32)
        x = x_ref[rows, :] + g1_ref[...] * _rms(y, npost_ref[...])
        xo_ref[rows, :] = x
        h_parts.append(_rms(x, npre_ref[...]) * (1.0 + sc2_ref[...]) + sh2_ref[...])
    h = jnp.concatenate(h_parts, axis=0)
    if not with_router:
        h_ref[...] = h.astype(h_ref.dtype)
        return
    _to_token_tiles(h_ref, h)

    @pl.when(pl.program_id(0) == 0)
    def _():
        carry_ref[...] = jnp.zeros_like(carry_ref)

    h_hi = h.astype(BF16)
    h_lo = (h - h_hi.astype(F32)).astype(BF16)
    w = wrt_ref[...]
    w_hi = w.astype(BF16)
    w_lo = (w - w_hi.astype(F32)).astype(BF16)
    lg = (jnp.dot(h_hi, w_hi, preferred_element_type=F32) + jnp.dot(h_hi, w_lo, preferred_element_type=F32)
          + jnp.dot(h_lo, w_hi, preferred_element_type=F32)) + brt_ref[...]
    lane = lax.broadcasted_iota(I32, lg.shape, 1).astype(F32)
    lg = jnp.where(lane < N_EXPERTS, lg, -jnp.inf)
    m1 = jnp.max(lg, axis=-1, keepdims=True)
    i1 = jnp.min(jnp.where(lg == m1, lane, float(LANE)), axis=-1, keepdims=True)
    lg2 = jnp.where(lane == i1, -jnp.inf, lg)
    m2 = jnp.max(lg2, axis=-1, keepdims=True)
    i2 = jnp.min(jnp.where(lg2 == m2, lane, float(LANE)), axis=-1, keepdims=True)
    t = jnp.exp(m2 - m1)
    w1 = 1.0 / (1.0 + t)
    w2 = t / (1.0 + t)
    oh = jnp.where((lane == i1) | (lane == i2), 1.0, 0.0)
    pos = jnp.dot(tri_ref[...], oh.astype(BF16), preferred_element_type=F32) + carry_ref[...]
    r1 = jnp.sum(jnp.where(lane == i1, pos, 0.0), axis=-1, keepdims=True)
    r2 = jnp.sum(jnp.where(lane == i2, pos, 0.0), axis=-1, keepdims=True)
    carry = carry_ref[...] + jnp.sum(oh, axis=0, keepdims=True)
    carry_ref[...] = carry
    cnt_ref[...] = jnp.broadcast_to(carry, cnt_ref.shape)
    meta = jnp.zeros(lg.shape, F32)
    for j, val in enumerate((i1, i2, r1, r2, w1, w2)):
        meta = jnp.where(lane == j, val, meta)
    meta_ref[...] = meta
    metat_ref[...] = meta.T[:metat_ref.shape[0], :]


def _postmix(outs, ret, main, x2, wa, wr, wo, npost, npre, mod, layer, S, router=None, tm=512):
    R, D = x2.shape
    tpb = S // tm
    row = lambda w: pl.BlockSpec((tm, w), lambda i: (i, 0))
    const = lambda a: pl.BlockSpec(a.shape, lambda i: (0, 0))

    def att_spec(d):
        return pl.BlockSpec((None, d, tm // d, ATT_T_WIDTH), lambda i: (i // tpb, 0, i % tpb, 0))

    specs_att = [att_spec(d) for _, d in ATT_GROUPS]
    in_specs = (specs_att + [
        row(RET_V_WIDTH),
        pl.BlockSpec((tm, D), lambda i: (i, 2)),
        pl.BlockSpec((tm, D), lambda i: (i, 3)),
        row(D), const(wa), const(wr), const(wo), const(npost),
        _mod_spec(layer, 2, tpb), const(npre), _mod_spec(layer, 4, tpb), _mod_spec(layer, 3, tpb)])
    args = list(outs) + [ret, main, main, x2, wa, wr, wo, npost, mod, npre, mod, mod]
    n_slabs = sum(HEADS_PER_GROUP for _, d in ATT_GROUPS if d > 1)
    scratch = [pltpu.VMEM((n_slabs, tm, LANE), F32)]
    if router is None:
        out_shape = [jax.ShapeDtypeStruct((R, D), F32), jax.ShapeDtypeStruct((R, D), BF16)]
        out_specs = [row(D), row(D)]
        sem = ("arbitrary",)
    else:
        tri = jnp.asarray(np.tril(np.ones((tm, tm), np.float32), -1), BF16)
        router = tuple(router) + (tri,)
        in_specs += [const(a) for a in router]
        args += list(router)
        out_shape = [jax.ShapeDtypeStruct((R, D), F32), jax.ShapeDtypeStruct((R * TOK_ROWS, LANE), F32),
                     jax.ShapeDtypeStruct((R, LANE), F32), jax.ShapeDtypeStruct((8, R), F32),
                     jax.ShapeDtypeStruct((8, LANE), F32)]
        out_specs = [row(D), pl.BlockSpec((tm * TOK_ROWS, LANE), lambda i: (i, 0)), row(LANE),
                     pl.BlockSpec((8, tm), lambda i: (0, i)), pl.BlockSpec((8, LANE), lambda i: (0, 0))]
        scratch.append(pltpu.VMEM((1, LANE), F32))
        sem = ("arbitrary",)
    return pl.pallas_call(
        functools.partial(_postmix_kernel, with_router=router is not None),
        out_shape=tuple(out_shape),
        grid=(R // tm,),
        in_specs=in_specs,
        out_specs=tuple(out_specs),
        scratch_shapes=scratch,
        compiler_params=_cparams(sem, 56),
        name="postmix",
    )(*args)


def _swiglu(hb, wg_ref, wu_ref, wd_ref):
    y = None
    for c in range(D_FF // MXU_N):
        cols = slice(c * MXU_N, (c + 1) * MXU_N)
        act = (_silu(jnp.dot(hb, wg_ref[:, cols], preferred_element_type=F32))
               * jnp.dot(hb, wu_ref[:, cols], preferred_element_type=F32))
        part = jnp.dot(act.astype(BF16), wd_ref[cols, :], preferred_element_type=F32)
        y = part if y is None else y + part
    return y


def _ffn_kernel(h_ref, x_ref, wg_ref, wu_ref, wd_ref, npost_ref, g2_ref, o_ref):
    y = _swiglu(h_ref[...], wg_ref, wu_ref, wd_ref)
    o_ref[...] = x_ref[...] + g2_ref[...] * _rms(y, npost_ref[...])


def _ffn(h2, x2, wg, wu, wd, npost, mod, layer, S, tm=512):
    R, D = x2.shape
    tpb = S // tm
    resident = lambda a: pl.BlockSpec(a.shape, lambda i: (0, 0), pipeline_mode=pl.Buffered(1))
    return pl.pallas_call(
        _ffn_kernel,
        out_shape=jax.ShapeDtypeStruct((R, D), F32),
        grid=(R // tm,),
        in_specs=[
            pl.BlockSpec((tm, D), lambda i: (i, 0)),
            pl.BlockSpec((tm, D), lambda i: (i, 0)),
            resident(wg), resident(wu), resident(wd),
            pl.BlockSpec((1, D), lambda i: (0, 0)),
            _mod_spec(layer, 5, tpb),
        ],
        out_specs=pl.BlockSpec((tm, D), lambda i: (i, 0)),
        compiler_params=_cparams(("arbitrary",), 48),
        name="ffn",
    )(h2, x2, wg, wu, wd, npost, mod)


MOE_TM = 512
ROW_UNROLL = 8
DMA_QUEUES = 2


def _invperm_kernel(n_ref, dest_ref, src_ref, zeros_ref, sem):
    zeros_ref[...] = jnp.zeros_like(zeros_ref)
    fill = pltpu.make_async_copy(zeros_ref, src_ref, sem)
    fill.start()
    fill.wait()

    def put(g, c):
        for u in range(ROW_UNROLL):
            a = g * ROW_UNROLL + u
            src_ref[dest_ref[a]] = a
        return c

    lax.fori_loop(0, n_ref[0], put, 0)


def _invperm(dest, P):
    assert dest.shape[0] % ROW_UNROLL == 0
    trips = jnp.array([dest.shape[0] // ROW_UNROLL], I32)
    return pl.pallas_call(
        _invperm_kernel,
        out_shape=jax.ShapeDtypeStruct((P,), I32),
        in_specs=[pl.BlockSpec(memory_space=pltpu.SMEM), pl.BlockSpec(memory_space=pltpu.SMEM)],
        out_specs=pl.BlockSpec(memory_space=pltpu.SMEM),
        scratch_shapes=[pltpu.VMEM((P,), I32), pltpu.SemaphoreType.DMA],
        name="invperm",
    )(trips, dest)


TOK_ROWS = D_MODEL // LANE


def _to_token_tiles(ref, val):
    tm = val.shape[0]
    for s in range(TOK_ROWS):
        ref[pl.ds(s, tm, stride=TOK_ROWS), :] = val[:, s * LANE:(s + 1) * LANE]


def _token_tile_slabs(read, tm):
    return [read(pl.ds(s, tm, stride=TOK_ROWS)) for s in range(TOK_ROWS)]


def _expert_kernel(te_ref, nu_ref, src_ref, h_hbm, wg_ref, wu_ref, wd_ref, o_ref, hb_ref, hbuf_ref, sem, *, n_tok):
    j = pl.program_id(0)
    nu = nu_ref[0]
    tm = hb_ref.shape[0]

    def row_copy(row, slot, i):
        src = h_hbm.at[pl.ds(pl.multiple_of(row * TOK_ROWS, TOK_ROWS), TOK_ROWS), :]
        dst = hbuf_ref.at[slot, pl.ds(pl.multiple_of(i * TOK_ROWS, TOK_ROWS), TOK_ROWS), :]
        return pltpu.make_async_copy(src, dst, sem.at[slot])

    def issue(tile, slot):
        def body(g, c):
            for k in range(DMA_QUEUES):
                i = g * DMA_QUEUES + k
                row_copy(src_ref[tile * tm + i] & (n_tok - 1), slot, i).start(priority=k)
            return c

        lax.fori_loop(0, tm // DMA_QUEUES, body, 0, unroll=ROW_UNROLL // DMA_QUEUES)

    def wait(slot):
        def body(i, c):
            row_copy(0, slot, 0).wait()
            return c

        lax.fori_loop(0, tm, body, 0, unroll=ROW_UNROLL)

    slot = j % 2

    @pl.when(j == 0)
    def _():
        issue(0, 0)

    @pl.when(j <= nu)
    def _():
        wait(slot)

    @pl.when(j < nu)
    def _():
        for s, slab in enumerate(_token_tile_slabs(lambda rows: hbuf_ref[slot, rows, :], tm)):
            hb_ref[:, s * LANE:(s + 1) * LANE] = slab.astype(BF16)
        issue(jnp.minimum(j + 1, pl.num_programs(0) - 1), 1 - slot)
        _to_token_tiles(o_ref, _swiglu(hb_ref[...], wg_ref, wu_ref, wd_ref))

    @pl.when((j == pl.num_programs(0) - 1) & (j < nu))
    def _():
        wait(1 - slot)

    @pl.when(j >= nu)
    def _():
        o_ref[...] = jnp.zeros_like(o_ref)


def _experts(tile_expert, n_used, src, h2t, wg, wu, wd, n_tiles, tm):
    R, D = h2t.shape[0] // TOK_ROWS, D_MODEL
    assert R & (R - 1) == 0
    expert = lambda j, te, nu: te[jnp.minimum(j, nu[0] - 1)]
    return pl.pallas_call(
        functools.partial(_expert_kernel, n_tok=R),
        out_shape=jax.ShapeDtypeStruct((n_tiles * tm * TOK_ROWS, LANE), F32),
        grid_spec=pltpu.PrefetchScalarGridSpec(
            num_scalar_prefetch=3,
            grid=(n_tiles,),
            in_specs=[
                pl.BlockSpec(memory_space=pl.ANY),
                pl.BlockSpec((None, D, D_FF), lambda j, te, nu, src: (expert(j, te, nu), 0, 0)),
                pl.BlockSpec((None, D, D_FF), lambda j, te, nu, src: (expert(j, te, nu), 0, 0)),
                pl.BlockSpec((None, D_FF, D), lambda j, te, nu, src: (expert(j, te, nu), 0, 0)),
            ],
            out_specs=pl.BlockSpec((tm * TOK_ROWS, LANE), lambda j, te, nu, src: (j, 0)),
            scratch_shapes=[pltpu.VMEM((tm, D), BF16), pltpu.VMEM((2, tm * TOK_ROWS, LANE), F32),
                            pltpu.SemaphoreType.DMA((2,))],
        ),
        compiler_params=_cparams(("arbitrary",), 56),
        name="moe_experts",
    )(tile_expert, n_used, src, h2t, wg, wu, wd)


def _combine_kernel(d1_ref, d2_ref, y_hbm, meta_ref, x_ref, npost_ref, g2_ref, o_ref, ya_ref, yb_ref, sem):
    tm = x_ref.shape[0]
    step = pl.program_id(0)
    slot = step % 2

    def row_copy(row, dst_ref, s, r):
        src = y_hbm.at[pl.ds(pl.multiple_of(row * TOK_ROWS, TOK_ROWS), TOK_ROWS), :]
        dst = dst_ref.at[s, pl.ds(pl.multiple_of(r * TOK_ROWS, TOK_ROWS), TOK_ROWS), :]
        return pltpu.make_async_copy(src, dst, sem.at[s])

    def issue(tile, s):
        def body(r, c):
            t = tile * tm + r
            row_copy(d1_ref[t], ya_ref, s, r).start(priority=0)
            row_copy(d2_ref[t], yb_ref, s, r).start(priority=1)
            return c

        lax.fori_loop(0, tm, body, 0, unroll=ROW_UNROLL)

    @pl.when(step == 0)
    def _():
        issue(0, 0)

    @pl.when(step + 1 < pl.num_programs(0))
    def _():
        issue(step + 1, 1 - slot)

    def wait(r, c):
        row_copy(0, ya_ref, slot, 0).wait()
        row_copy(0, yb_ref, slot, 0).wait()
        return c

    lax.fori_loop(0, tm, wait, 0, unroll=ROW_UNROLL)
    meta = meta_ref[...]
    w1, w2 = meta[:, 4:5], meta[:, 5:6]
    slabs_a = _token_tile_slabs(lambda rows: ya_ref[slot, rows, :], tm)
    slabs_b = _token_tile_slabs(lambda rows: yb_ref[slot, rows, :], tm)
    y = jnp.concatenate([w1 * a + w2 * b for a, b in zip(slabs_a, slabs_b)], axis=1)
    o_ref[...] = x_ref[...] + g2_ref[...] * _rms(y, npost_ref[...])


def _combine(dest1, dest2, sorted_y, meta, x2, npost, mod, layer, S, tm=512):
    R, D = x2.shape
    tpb = S // tm
    return pl.pallas_call(
        _combine_kernel,
        out_shape=jax.ShapeDtypeStruct((R, D), F32),
        grid_spec=pltpu.PrefetchScalarGridSpec(
            num_scalar_prefetch=2,
            grid=(R // tm,),
            in_specs=[
                pl.BlockSpec(memory_space=pl.ANY),
                pl.BlockSpec((tm, LANE), lambda i, *_: (i, 0)),
                pl.BlockSpec((tm, D), lambda i, *_: (i, 0)),
                pl.BlockSpec((1, D), lambda i, *_: (0, 0)),
                _mod_spec(layer, 5, tpb),
            ],
            out_specs=pl.BlockSpec((tm, D), lambda i, *_: (i, 0)),
            scratch_shapes=[pltpu.VMEM((2, tm * TOK_ROWS, LANE), F32), pltpu.VMEM((2, tm * TOK_ROWS, LANE), F32),
                            pltpu.SemaphoreType.DMA((2,))],
        ),
        compiler_params=_cparams(("arbitrary",), 40),
        name="moe_combine",
    )(dest1, dest2, sorted_y, meta, x2, npost, mod)


def _moe(h2, meta, meta_t, counts, x2, wg, wu, wd, npost, mod, layer, S):
    R, D = x2.shape
    tm = MOE_TM
    n_tiles = (TOP_K * R + N_EXPERTS * (tm - 1)) // tm + 1
    i1, i2, r1, r2 = (meta_t[j].astype(I32) for j in range(4))
    cnt = counts[0, :N_EXPERTS].astype(I32)
    padded = (cnt + tm - 1) // tm * tm
    end = jnp.cumsum(padded)
    start = end - padded
    eye = jnp.arange(N_EXPERTS, dtype=I32)[:, None]
    dest1 = jnp.sum(jnp.where(i1[None, :] == eye, start[:, None], 0), axis=0) + r1
    dest2 = jnp.sum(jnp.where(i2[None, :] == eye, start[:, None], 0), axis=0) + r2
    tile_row0 = jnp.arange(n_tiles, dtype=I32) * tm
    tile_expert = jnp.minimum(jnp.sum(tile_row0[:, None] >= end[None, :], axis=1), N_EXPERTS - 1).astype(I32)
    n_used = (end[-1:] // tm).astype(I32)
    src = _invperm(jnp.concatenate([dest1, dest2]), n_tiles * tm)
    sorted_y = _experts(tile_expert, n_used, src, h2, wg, wu, wd, n_tiles, tm)
    return _combine(dest1, dest2, sorted_y, meta, x2, npost, mod, layer, S)


def kernel(x, c, w_ada, b_ada, norm_pre_mix, w_in, w_att_proj, w_ret_proj, w_mix_out, norm_post_mix, norm_pre_ffn, w_ffn_gate, w_ffn_up, w_ffn_down, w_router, b_router, w_exp_gate, w_exp_up, w_exp_down, norm_post_ffn):
    B, S, D = x.shape
    R = B * S
    mod = _ada(c, w_ada, b_ada)
    x2 = x.reshape(R, D)
    row = lambda v: v.reshape(1, D)
    for layer in range(DEPTH):
        main, *atts = _inproj(x2, row(norm_pre_mix[layer]), mod, layer, _prep_w_in(w_in[layer]), B, S)
        outs = [_attn_group(atts[g], g) for g in range(len(ATT_GROUPS))]
        ret = _retention(main.reshape(B, S, MAIN_COLS), B, S)
        j = layer // 2
        router = None
        if layer % 2 == 1:
            wrt = jnp.zeros((D, LANE), F32).at[:, :N_EXPERTS].set(w_router[j])
            brt = jnp.zeros((1, LANE), F32).at[0, :N_EXPERTS].set(b_router[j])
            router = (wrt, brt)
        wa = jnp.pad(w_att_proj[layer].reshape(HEADS_PER_GROUP, ATT_HEAD_DIM, D),
                     ((0, 0), (0, LANE - ATT_HEAD_DIM), (0, 0))).reshape(ATT_T_WIDTH, D)
        res = _postmix(outs, ret, main, x2, wa.astype(BF16), w_ret_proj[layer].astype(BF16),
                       w_mix_out[layer].astype(BF16), row(norm_post_mix[layer]), row(norm_pre_ffn[layer]),
                       mod, layer, S, router=router)
        if layer % 2 == 0:
            x2, h2 = res
            x2 = _ffn(h2, x2, w_ffn_gate[j].astype(BF16), w_ffn_up[j].astype(BF16), w_ffn_down[j].astype(BF16),
                      row(norm_post_ffn[layer]), mod, layer, S)
        else:
            x2, h2, meta, meta_t, counts = res
            x2 = _moe(h2, meta, meta_t, counts, x2, w_exp_gate[j].astype(BF16), w_exp_up[j].astype(BF16),
                      w_exp_down[j].astype(BF16), row(norm_post_ffn[layer]), mod, layer, S)
    return x2.reshape(B, S, D)
```

```python
import functools

import numpy as np
import jax
import jax.numpy as jnp
from jax import lax
from jax.experimental import pallas as pl
from jax.experimental.pallas import tpu as pltpu

F32 = jnp.float32
BF16 = jnp.bfloat16
I32 = jnp.int32

D_MODEL = 1024
DEPTH = 2
ATT_GROUPS = ((128, 1), (512, 4), (2048, 16))
HEADS_PER_GROUP = 4
N_ATT_HEADS = HEADS_PER_GROUP * len(ATT_GROUPS)
ATT_HEAD_DIM = 64
ATT_WIDTH = N_ATT_HEADS * ATT_HEAD_DIM
ATT_OUT_WIDTH = HEADS_PER_GROUP * ATT_HEAD_DIM
BLOCK = 128
ALIBI_MAX = 8.0
NEG_INF = -1e30
RET_HEADS = 8
RET_KEY_DIM = 64
RET_VAL_DIM = 128
RET_QK_WIDTH = RET_HEADS * RET_KEY_DIM
RET_V_WIDTH = RET_HEADS * RET_VAL_DIM
RET_CHUNK = 128
ROPE_BASE = 10000.0
D_FF = 2816
N_EXPERTS = 8
TOP_K = 2
EPS = 1e-6
IN_SIZES = (ATT_WIDTH, ATT_WIDTH, ATT_WIDTH, RET_QK_WIDTH, RET_QK_WIDTH, RET_V_WIDTH, RET_V_WIDTH, D_MODEL, D_MODEL)
IN_COLS = sum(IN_SIZES)

LANE = 128
MAIN_COLS = 4 * D_MODEL + 2 * RET_QK_WIDTH
GROUP_SRC_COLS = 3 * ATT_OUT_WIDTH
GROUP_SLOTS = 3 * HEADS_PER_GROUP
GROUP_COLS = GROUP_SLOTS * LANE
ATT_T_WIDTH = HEADS_PER_GROUP * LANE
MXU_N = 256

MiB = 1024 * 1024


def _cparams(sem, vmem_mib):
    return pltpu.CompilerParams(dimension_semantics=sem, vmem_limit_bytes=vmem_mib * MiB)


def _rms(x, gain):
    return x * lax.rsqrt(jnp.mean(x * x, axis=-1, keepdims=True) + EPS) * gain


def _silu(x):
    return x * jax.nn.sigmoid(x)


def _ada_kernel(c_ref, w_ref, b_ref, o_ref):
    ca = _silu(c_ref[...])
    o_ref[0] = jnp.dot(ca, w_ref[0], preferred_element_type=F32, precision=lax.Precision.HIGHEST) + b_ref[0]


def _ada(c, w_ada, b_ada):
    B, D = c.shape
    out = pl.pallas_call(
        _ada_kernel,
        out_shape=jax.ShapeDtypeStruct((DEPTH * 6, B, D), F32),
        grid=(DEPTH, 6),
        in_specs=[
            pl.BlockSpec((B, D), lambda l, k: (0, 0)),
            pl.BlockSpec((1, D, D), lambda l, k: (l, 0, k)),
            pl.BlockSpec((1, 1, D), lambda l, k: (l, 0, k)),
        ],
        out_specs=pl.BlockSpec((1, B, D), lambda l, k: (l * 6 + k, 0, 0)),
        compiler_params=_cparams(("arbitrary", "arbitrary"), 32),
        name="ada",
    )(c, w_ada, b_ada.reshape(DEPTH, 1, 6 * D))
    return out.reshape(DEPTH, 6, B, 1, D)


def _mod_spec(layer, k, tiles_per_batch):
    return pl.BlockSpec((None, None, None, 1, D_MODEL),
                        lambda i, *_: (layer, k, i // tiles_per_batch, 0, 0))


def _inproj_kernel(x_ref, gain_ref, sc_ref, sh_ref, w_ref, main_ref, a0_ref, a1_ref, a2_ref, scr_ref):
    tm = x_ref.shape[0]
    h = _rms(x_ref[...], gain_ref[...]) * (1.0 + sc_ref[...]) + sh_ref[...]
    hb = h.astype(BF16)
    for c in range(MAIN_COLS // MXU_N):
        sl = slice(c * MXU_N, (c + 1) * MXU_N)
        main_ref[:, sl] = jnp.dot(hb, w_ref[:, sl], preferred_element_type=F32).astype(BF16)
    att_refs = (a0_ref, a1_ref, a2_ref)
    heads_per_slab = LANE // ATT_HEAD_DIM

    def store_heads(att_ref, r, slot0, val):
        lane = lax.broadcasted_iota(I32, val.shape, 1)
        for k in range(heads_per_slab):
            head = val if k == 0 else pltpu.roll(val, LANE - k * ATT_HEAD_DIM, 1)
            c0 = (slot0 + k) * LANE
            att_ref[r, :, c0:c0 + LANE] = jnp.where(lane < ATT_HEAD_DIM, head, 0.0).astype(BF16)

    slab = 0
    for g, (_, d) in enumerate(ATT_GROUPS):
        for seg in range(3):
            col = MAIN_COLS + g * GROUP_SRC_COLS + seg * ATT_OUT_WIDTH
            res = jnp.dot(hb, w_ref[:, col:col + ATT_OUT_WIDTH], preferred_element_type=F32)
            for s in range(ATT_OUT_WIDTH // LANE):
                part = res[:, s * LANE:(s + 1) * LANE]
                slot0 = seg * HEADS_PER_GROUP + s * heads_per_slab
                if d == 1:
                    store_heads(att_refs[g], 0, slot0, part)
                    continue
                scr_ref[slab] = part
                for r in range(d):
                    store_heads(att_refs[g], r, slot0, scr_ref[slab, pl.ds(r, tm // d, stride=d), :])
                slab += 1


def _inproj(x2, gain, mod, layer, w_in_b, B, S, tm=512):
    R, D = x2.shape
    tpb = S // tm
    n_slabs = sum(3 * (ATT_OUT_WIDTH // LANE) for _, d in ATT_GROUPS if d > 1)

    def att_shape(d):
        return jax.ShapeDtypeStruct((B, d, S // d, GROUP_COLS), BF16)

    def att_spec(d):
        return pl.BlockSpec((None, d, tm // d, GROUP_COLS), lambda i: (i // tpb, 0, i % tpb, 0))

    return pl.pallas_call(
        _inproj_kernel,
        out_shape=(jax.ShapeDtypeStruct((R, MAIN_COLS), BF16),) + tuple(att_shape(d) for _, d in ATT_GROUPS),
        grid=(R // tm,),
        in_specs=[
            pl.BlockSpec((tm, D), lambda i: (i, 0)),
            pl.BlockSpec((1, D), lambda i: (0, 0)),
            _mod_spec(layer, 1, tpb),
            _mod_spec(layer, 0, tpb),
            pl.BlockSpec((D, IN_COLS), lambda i: (0, 0), pipeline_mode=pl.Buffered(1)),
        ],
        out_specs=(pl.BlockSpec((tm, MAIN_COLS), lambda i: (i, 0)),) + tuple(att_spec(d) for _, d in ATT_GROUPS),
        scratch_shapes=[pltpu.VMEM((n_slabs, tm, LANE), F32)],
        compiler_params=_cparams(("arbitrary",), 56),
        name="inproj",
    )(x2, gain, mod, mod, w_in_b)


def _prep_w_in(w):
    b = np.concatenate([[0], np.cumsum(IN_SIZES)])
    w = w.astype(BF16)
    seg = lambda s: w[:, b[s]:b[s + 1]]
    cols = [seg(5), seg(6), seg(7), seg(8), seg(3), seg(4)]
    q_a, k_a, v_a = seg(0) * (ATT_HEAD_DIM ** -0.5), seg(1), seg(2)
    for g in range(len(ATT_GROUPS)):
        gs = slice(g * ATT_OUT_WIDTH, (g + 1) * ATT_OUT_WIDTH)
        cols += [q_a[:, gs], k_a[:, gs], v_a[:, gs]]
    return jnp.concatenate(cols, axis=1)


def _alibi_slopes():
    return [2.0 ** (-ALIBI_MAX * h / N_ATT_HEADS) for h in range(1, N_ATT_HEADS + 1)]


ATT_UNITS = 16


def _attn_kernel(a_ref, bias_ref, o_ref, *, n, dilation):
    nb = n // BLOCK
    win = min(2 * BLOCK, n)
    ones = jnp.ones((win, LANE), BF16)
    lane = lax.broadcasted_iota(I32, (BLOCK, LANE), 1)

    def head_units(u):
        r = u // nb
        i = u % nb
        q0 = pl.multiple_of(i * BLOCK, BLOCK)
        k0 = pl.multiple_of(jnp.clip((i - 1) * BLOCK, 0, n - win), BLOCK)
        t = jnp.minimum(i, 1)
        return [(r, pl.ds(q0, BLOCK), pl.ds(k0, win), t, h) for h in range(HEADS_PER_GROUP)]

    def pair(u2, carry):
        units = sum((head_units(ATT_UNITS * u2 + j) for j in range(ATT_UNITS)), [])
        scores = []
        for r, qrows, krows, t, h in units:
            qs = slice(h * LANE, (h + 1) * LANE)
            ks = slice((HEADS_PER_GROUP + h) * LANE, (HEADS_PER_GROUP + h + 1) * LANE)
            s = lax.dot_general(a_ref[r, qrows, qs], a_ref[r, krows, ks], (((1,), (1,)), ((), ())),
                                preferred_element_type=F32)
            scores.append(s + bias_ref[t, h])
        probs = []
        for s in scores:
            m = jnp.max(s, axis=-1, keepdims=True)
            probs.append((jnp.exp(s - m).astype(BF16), m))
        for (r, qrows, krows, t, h), (p, m) in zip(units, probs):
            qs = slice(h * LANE, (h + 1) * LANE)
            vs = slice((2 * HEADS_PER_GROUP + h) * LANE, (2 * HEADS_PER_GROUP + h + 1) * LANE)
            pv = jnp.dot(p, jnp.concatenate([a_ref[r, krows, vs], ones], axis=1), preferred_element_type=F32)
            den = pv[:, LANE:]
            o_ref[r, qrows, qs] = jnp.where(lane < ATT_HEAD_DIM, pv[:, :LANE] / den, m + jnp.log(den))
        return carry

    lax.fori_loop(0, dilation * nb // ATT_UNITS, pair, 0)


def _attn_bias(n, n_w, d, slopes):
    win = min(2 * BLOCK, n)
    r = np.arange(BLOCK)[:, None]
    c = np.arange(win)[None, :]
    tables = []
    for shift in (0, win - BLOCK):
        dist = shift + r - c
        valid = (dist >= 0) & (dist <= n_w)
        tables.append(np.stack([np.where(valid, -s * (d * dist), NEG_INF) for s in slopes]))
    return jnp.asarray(np.stack(tables), F32)


def _attn_group(att, g):
    window, d = ATT_GROUPS[g]
    B, _, n, _ = att.shape
    assert n % BLOCK == 0 and (d * (n // BLOCK)) % ATT_UNITS == 0
    slopes = _alibi_slopes()[g * HEADS_PER_GROUP:(g + 1) * HEADS_PER_GROUP]
    bias = _attn_bias(n, window // d, d, slopes)
    kern = functools.partial(_attn_kernel, n=n, dilation=d)
    out_spec = pl.BlockSpec((None, d, n, ATT_T_WIDTH), lambda b: (b, 0, 0, 0))
    return pl.pallas_call(
        kern,
        out_shape=jax.ShapeDtypeStruct((B, d, n, ATT_T_WIDTH), F32),
        grid=(B,),
        in_specs=[pl.BlockSpec((None, d, n, GROUP_COLS), lambda b: (b, 0, 0, 0)),
                  pl.BlockSpec(bias.shape, lambda b: (0, 0, 0, 0))],
        out_specs=out_spec,
        compiler_params=_cparams(("arbitrary",), 48),
        name=f"attn{g}",
    )(att, bias)


def _ret_tables(S):
    half = RET_KEY_DIM // 2
    inv_freq = 1.0 / (ROPE_BASE ** (jnp.arange(half, dtype=F32) / half))
    ang = jnp.arange(S).astype(F32)[:, None] * inv_freq[None]
    cos = jnp.cos(ang)
    sin = jnp.sin(ang)
    cos_t = jnp.tile(jnp.concatenate([cos, cos], axis=-1), (1, RET_HEADS))
    sin_t = jnp.tile(jnp.concatenate([-sin, sin], axis=-1), (1, RET_HEADS))
    log_gamma = jnp.log1p(-jnp.exp2(-5.0 - jnp.arange(RET_HEADS, dtype=F32)))
    idx = jnp.arange(RET_CHUNK, dtype=F32)
    rel = idx[:, None] - idx[None, :]
    decay = jnp.where(rel[None] >= 0, jnp.exp(jnp.maximum(rel, 0.0)[None] * log_gamma[:, None, None]), 0.0)
    zeta = jnp.exp((RET_CHUNK - 1 - idx)[None] * log_gamma[:, None])
    xi = jnp.exp((idx + 1)[None] * log_gamma[:, None])
    zeta_t = jnp.repeat(zeta.T, RET_KEY_DIM, axis=1) * (RET_KEY_DIM ** -0.5)
    xi_t = jnp.repeat(xi.T, RET_KEY_DIM, axis=1)
    g_chunk = jnp.exp(RET_CHUNK * log_gamma)
    g_t = jnp.broadcast_to(g_chunk[:, None, None], (RET_HEADS, 1, RET_VAL_DIM))
    return cos_t, sin_t, decay, zeta_t, xi_t, g_t


def _rot_half_swap(t):
    w = t.shape[-1]
    lane = lax.broadcasted_iota(I32, t.shape, 1) % RET_KEY_DIM
    half = RET_KEY_DIM // 2
    return jnp.where(lane < half, pltpu.roll(t, w - half, 1), pltpu.roll(t, half, 1))


RET_ROWS = 1024
RET_PAIR = 4


def _retn_kernel(q_ref, k_ref, v_ref, g_ref, cos_ref, sin_ref, decay_ref, zeta_ref, xi_ref, gch_ref,
                 o_ref, state_ref):
    C = RET_CHUNK

    @pl.when(pl.program_id(1) == 0)
    def _():
        state_ref[...] = jnp.zeros_like(state_ref)

    def prep(c):
        rows = pl.ds(pl.multiple_of(c * C, C), C)
        cos = cos_ref[rows, :]
        sin = sin_ref[rows, :]
        q = q_ref[rows, :].astype(F32)
        k = k_ref[rows, :].astype(F32)
        q = q * cos + _rot_half_swap(q) * sin
        k = k * cos + _rot_half_swap(k) * sin
        qb = q.astype(BF16)
        kb = (k * (RET_KEY_DIM ** -0.5)).astype(BF16)
        q_xi = (q * xi_ref[...]).astype(BF16)
        kz_t = (k * zeta_ref[...]).T
        return rows, qb, kb, q_xi, kz_t

    def group(u, carry):
        chunks = [prep(u * RET_PAIR + i) for i in range(RET_PAIR)]
        ksl = [slice(h * RET_KEY_DIM, (h + 1) * RET_KEY_DIM) for h in range(RET_HEADS)]
        vsl = [slice(h * RET_VAL_DIM, (h + 1) * RET_VAL_DIM) for h in range(RET_HEADS)]
        units = [(c, h) for h in range(RET_HEADS) for c in range(RET_PAIR)]
        scores, kvs = {}, {}
        for c, h in units:
            rows, qb, kb, q_xi, kz_t = chunks[c]
            v = v_ref[rows, vsl[h]]
            scores[c, h] = lax.dot_general(qb[:, ksl[h]], kb[:, ksl[h]], (((1,), (1,)), ((), ())),
                                           preferred_element_type=F32)
            kvs[c, h] = jnp.dot(kz_t[ksl[h], :].astype(BF16), v, preferred_element_type=F32)
        states = {}
        for h in range(RET_HEADS):
            st = state_ref[h]
            for c in range(RET_PAIR):
                states[c, h] = st.astype(BF16)
                st = st * gch_ref[h] + kvs[c, h]
            state_ref[h] = st
        outs = {}
        for c, h in units:
            rows, qb, kb, q_xi, kz_t = chunks[c]
            s = (scores[c, h] * decay_ref[h]).astype(BF16)
            outs[c, h] = (jnp.dot(s, v_ref[rows, vsl[h]], preferred_element_type=F32)
                          + jnp.dot(q_xi[:, ksl[h]], states[c, h], preferred_element_type=F32))
        for c, h in units:
            rows = chunks[c][0]
            o = outs[c, h]
            o = o * lax.rsqrt(jnp.mean(o * o, axis=-1, keepdims=True) + EPS)
            o_ref[rows, vsl[h]] = (_silu(g_ref[rows, vsl[h]].astype(F32)) * o).astype(o_ref.dtype)
        return carry

    lax.fori_loop(0, q_ref.shape[0] // (C * RET_PAIR), group, 0)


def _retention(main3, B, S):
    C = RET_CHUNK
    RS = RET_ROWS
    assert S % RS == 0 and RS % (C * RET_PAIR) == 0
    cos_t, sin_t, decay, zeta_t, xi_t, g_t = _ret_tables(S)
    const2 = lambda b, c: (0, 0)
    const3 = lambda b, c: (0, 0, 0)
    qk0 = 4 * D_MODEL // RET_QK_WIDTH
    out = pl.pallas_call(
        _retn_kernel,
        out_shape=jax.ShapeDtypeStruct((B, S, RET_V_WIDTH), BF16),
        grid=(B, S // RS),
        in_specs=[
            pl.BlockSpec((None, RS, RET_QK_WIDTH), lambda b, c: (b, c, qk0)),
            pl.BlockSpec((None, RS, RET_QK_WIDTH), lambda b, c: (b, c, qk0 + 1)),
            pl.BlockSpec((None, RS, RET_V_WIDTH), lambda b, c: (b, c, 0)),
            pl.BlockSpec((None, RS, RET_V_WIDTH), lambda b, c: (b, c, 1)),
            pl.BlockSpec((RS, RET_QK_WIDTH), lambda b, c: (c, 0)),
            pl.BlockSpec((RS, RET_QK_WIDTH), lambda b, c: (c, 0)),
            pl.BlockSpec((RET_HEADS, C, C), const3),
            pl.BlockSpec((C, RET_QK_WIDTH), const2),
            pl.BlockSpec((C, RET_QK_WIDTH), const2),
            pl.BlockSpec((RET_HEADS, 1, RET_VAL_DIM), const3),
        ],
        out_specs=pl.BlockSpec((None, RS, RET_V_WIDTH), lambda b, c: (b, c, 0)),
        scratch_shapes=[pltpu.VMEM((RET_HEADS, RET_KEY_DIM, RET_VAL_DIM), F32)],
        compiler_params=_cparams(("arbitrary", "arbitrary"), 48),
        name="retn",
    )(main3, main3, main3, main3, cos_t, sin_t, decay, zeta_t, xi_t, g_t)
    return out.reshape(B * S, RET_V_WIDTH)


N_META = 6
POSTMIX_PARTS = 2


def _unstride(ref, scr_ref, slab0, d, tm):
    if d == 1:
        return [ref[0, :, s * LANE:(s + 1) * LANE] for s in range(HEADS_PER_GROUP)]
    for r in range(d):
        for s in range(HEADS_PER_GROUP):
            scr_ref[slab0 + s, pl.ds(r, tm // d, stride=d), :] = ref[r, :, s * LANE:(s + 1) * LANE]
    return [scr_ref[slab0 + s] for s in range(HEADS_PER_GROUP)]


def _postmix_kernel(o0_ref, o1_ref, o2_ref, ret_ref, ga_ref, gr_ref, x_ref,
                    wa_ref, wr_ref, wo_ref, npost_ref, g1_ref, npre_ref, sc2_ref, sh2_ref, *rest, with_router):
    if with_router:
        wrt_ref, brt_ref, tri_ref, xo_ref, h_ref, meta_ref, metat_ref, cnt_ref, scr_ref, carry_ref = rest
    else:
        xo_ref, h_ref, scr_ref = rest
    tm = x_ref.shape[0]
    groups = []
    slab = 0
    for (_, d), o_ref in zip(ATT_GROUPS, (o0_ref, o1_ref, o2_ref)):
        groups.append(_unstride(o_ref, scr_ref, slab, d, tm))
        if d > 1:
            slab += HEADS_PER_GROUP
    pr = tm // POSTMIX_PARTS
    head_lane = lax.broadcasted_iota(I32, (pr, LANE), 1)
    h_parts = []
    for part in range(POSTMIX_PARTS):
        rows = slice(part * pr, (part + 1) * pr)
        slots = []
        for hd in range(HEADS_PER_GROUP):
            outs = [grp[hd][rows] for grp in groups]
            lses = [pltpu.roll(o, LANE // 2, 1) for o in outs]
            mx = functools.reduce(jnp.maximum, lses)
            es = [jnp.exp(l - mx) for l in lses]
            num = sum(e * o for e, o in zip(es, outs))
            slots.append(jnp.where(head_lane < ATT_HEAD_DIM, num / sum(es), 0.0))
        att = jnp.concatenate(slots, axis=1)
        a = jnp.dot(att.astype(BF16), wa_ref[...], preferred_element_type=F32)
        r = jnp.dot(ret_ref[rows, :], wr_ref[...], preferred_element_type=F32)
        merged = (jax.nn.sigmoid(ga_ref[rows, :].astype(F32)) * a
                  + jax.nn.sigmoid(gr_ref[rows, :].astype(F32)) * r)
        y = jnp.dot(merged.astype(BF16), wo_ref[...], preferred_element_type=F32)
        x = x_ref[rows, :] + g1_ref[...] * _rms(y, npost_ref[...])
        xo_ref[rows, :] = x
        h_parts.append(_rms(x, npre_ref[...]) * (1.0 + sc2_ref[...]) + sh2_ref[...])
    h = jnp.concatenate(h_parts, axis=0)
    if not with_router:
        h_ref[...] = h.astype(h_ref.dtype)
        return
    _to_token_tiles(h_ref, h)

    @pl.when(pl.program_id(0) == 0)
    def _():
        carry_ref[...] = jnp.zeros_like(carry_ref)

    h_hi = h.astype(BF16)
    h_lo = (h - h_hi.astype(F32)).astype(BF16)
    w = wrt_ref[...]
    w_hi = w.astype(BF16)
    w_lo = (w - w_hi.astype(F32)).astype(BF16)
    lg = (jnp.dot(h_hi, w_hi, preferred_element_type=F32) + jnp.dot(h_hi, w_lo, preferred_element_type=F32)
          + jnp.dot(h_lo, w_hi, preferred_element_type=F32)) + brt_ref[...]
    lane = lax.broadcasted_iota(I32, lg.shape, 1).astype(F32)
    lg = jnp.where(lane < N_EXPERTS, lg, -jnp.inf)
    m1 = jnp.max(lg, axis=-1, keepdims=True)
    i1 = jnp.min(jnp.where(lg == m1, lane, float(LANE)), axis=-1, keepdims=True)
    lg2 = jnp.where(lane == i1, -jnp.inf, lg)
    m2 = jnp.max(lg2, axis=-1, keepdims=True)
    i2 = jnp.min(jnp.where(lg2 == m2, lane, float(LANE)), axis=-1, keepdims=True)
    t = jnp.exp(m2 - m1)
    w1 = 1.0 / (1.0 + t)
    w2 = t / (1.0 + t)
    oh = jnp.where((lane == i1) | (lane == i2), 1.0, 0.0)
    pos = jnp.dot(tri_ref[...], oh.astype(BF16), preferred_element_type=F32) + carry_ref[...]
    r1 = jnp.sum(jnp.where(lane == i1, pos, 0.0), axis=-1, keepdims=True)
    r2 = jnp.sum(jnp.where(lane == i2, pos, 0.0), axis=-1, keepdims=True)
    carry = carry_ref[...] + jnp.sum(oh, axis=0, keepdims=True)
    carry_ref[...] = carry
    cnt_ref[...] = jnp.broadcast_to(carry, cnt_ref.shape)
    meta = jnp.zeros(lg.shape, F32)
    for j, val in enumerate((i1, i2, r1, r2, w1, w2)):
        meta = jnp.where(lane == j, val, meta)
    meta_ref[...] = meta
    metat_ref[...] = meta.T[:metat_ref.shape[0], :]


def _postmix(outs, ret, main, x2, wa, wr, wo, npost, npre, mod, layer, S, router=None, tm=512):
    R, D = x2.shape
    tpb = S // tm
    row = lambda w: pl.BlockSpec((tm, w), lambda i: (i, 0))
    const = lambda a: pl.BlockSpec(a.shape, lambda i: (0, 0))

    def att_spec(d):
        return pl.BlockSpec((None, d, tm // d, ATT_T_WIDTH), lambda i: (i // tpb, 0, i % tpb, 0))

    specs_att = [att_spec(d) for _, d in ATT_GROUPS]
    in_specs = (specs_att + [
        row(RET_V_WIDTH),
        pl.BlockSpec((tm, D), lambda i: (i, 2)),
        pl.BlockSpec((tm, D), lambda i: (i, 3)),
        row(D), const(wa), const(wr), const(wo), const(npost),
        _mod_spec(layer, 2, tpb), const(npre), _mod_spec(layer, 4, tpb), _mod_spec(layer, 3, tpb)])
    args = list(outs) + [ret, main, main, x2, wa, wr, wo, npost, mod, npre, mod, mod]
    n_slabs = sum(HEADS_PER_GROUP for _, d in ATT_GROUPS if d > 1)
    scratch = [pltpu.VMEM((n_slabs, tm, LANE), F32)]
    if router is None:
        out_shape = [jax.ShapeDtypeStruct((R, D), F32), jax.ShapeDtypeStruct((R, D), BF16)]
        out_specs = [row(D), row(D)]
        sem = ("arbitrary",)
    else:
        tri = jnp.asarray(np.tril(np.ones((tm, tm), np.float32), -1), BF16)
        router = tuple(router) + (tri,)
        in_specs += [const(a) for a in router]
        args += list(router)
        out_shape = [jax.ShapeDtypeStruct((R, D), F32), jax.ShapeDtypeStruct((R * TOK_ROWS, LANE), F32),
                     jax.ShapeDtypeStruct((R, LANE), F32), jax.ShapeDtypeStruct((8, R), F32),
                     jax.ShapeDtypeStruct((8, LANE), F32)]
        out_specs = [row(D), pl.BlockSpec((tm * TOK_ROWS, LANE), lambda i: (i, 0)), row(LANE),
                     pl.BlockSpec((8, tm), lambda i: (0, i)), pl.BlockSpec((8, LANE), lambda i: (0, 0))]
        scratch.append(pltpu.VMEM((1, LANE), F32))
        sem = ("arbitrary",)
    return pl.pallas_call(
        functools.partial(_postmix_kernel, with_router=router is not None),
        out_shape=tuple(out_shape),
        grid=(R // tm,),
        in_specs=in_specs,
        out_specs=tuple(out_specs),
        scratch_shapes=scratch,
        compiler_params=_cparams(sem, 56),
        name="postmix",
    )(*args)


def _swiglu(hb, wg_ref, wu_ref, wd_ref):
    y = None
    for c in range(D_FF // MXU_N):
        cols = slice(c * MXU_N, (c + 1) * MXU_N)
        act = (_silu(jnp.dot(hb, wg_ref[:, cols], preferred_element_type=F32))
               * jnp.dot(hb, wu_ref[:, cols], preferred_element_type=F32))
        part = jnp.dot(act.astype(BF16), wd_ref[cols, :], preferred_element_type=F32)
        y = part if y is None else y + part
    return y


def _ffn_kernel(h_ref, x_ref, wg_ref, wu_ref, wd_ref, npost_ref, g2_ref, o_ref):
    y = _swiglu(h_ref[...], wg_ref, wu_ref, wd_ref)
    o_ref[...] = x_ref[...] + g2_ref[...] * _rms(y, npost_ref[...])


def _ffn(h2, x2, wg, wu, wd, npost, mod, layer, S, tm=512):
    R, D = x2.shape
    tpb = S // tm
    resident = lambda a: pl.BlockSpec(a.shape, lambda i: (0, 0), pipeline_mode=pl.Buffered(1))
    return pl.pallas_call(
        _ffn_kernel,
        out_shape=jax.ShapeDtypeStruct((R, D), F32),
        grid=(R // tm,),
        in_specs=[
            pl.BlockSpec((tm, D), lambda i: (i, 0)),
            pl.BlockSpec((tm, D), lambda i: (i, 0)),
            resident(wg), resident(wu), resident(wd),
            pl.BlockSpec((1, D), lambda i: (0, 0)),
            _mod_spec(layer, 5, tpb),
        ],
        out_specs=pl.BlockSpec((tm, D), lambda i: (i, 0)),
        compiler_params=_cparams(("arbitrary",), 48),
        name="ffn",
    )(h2, x2, wg, wu, wd, npost, mod)


MOE_TM = 512
ROW_UNROLL = 8
DMA_QUEUES = 2


def _invperm_kernel(n_ref, dest_ref, src_ref, zeros_ref, sem, *, n_tok):
    zeros_ref[...] = jnp.zeros_like(zeros_ref)
    fill = pltpu.make_async_copy(zeros_ref, src_ref, sem)
    fill.start()
    fill.wait()

    def put(g, c):
        for u in range(ROW_UNROLL):
            a = g * ROW_UNROLL + u
            src_ref[dest_ref[a]] = (a & (n_tok - 1)) * TOK_ROWS
        return c

    lax.fori_loop(0, n_ref[0], put, 0)


def _invperm(dest, P):
    assert dest.shape[0] % ROW_UNROLL == 0
    trips = jnp.array([dest.shape[0] // ROW_UNROLL], I32)
    n_tok = dest.shape[0] // TOP_K
    assert n_tok & (n_tok - 1) == 0
    return pl.pallas_call(
        functools.partial(_invperm_kernel, n_tok=n_tok),
        out_shape=jax.ShapeDtypeStruct((P,), I32),
        in_specs=[pl.BlockSpec(memory_space=pltpu.SMEM), pl.BlockSpec(memory_space=pltpu.SMEM)],
        out_specs=pl.BlockSpec(memory_space=pltpu.SMEM),
        scratch_shapes=[pltpu.VMEM((P,), I32), pltpu.SemaphoreType.DMA],
        name="invperm",
    )(trips, dest)


TOK_ROWS = D_MODEL // LANE


def _to_token_tiles(ref, val):
    tm = val.shape[0]
    for s in range(TOK_ROWS):
        ref[pl.ds(s, tm, stride=TOK_ROWS), :] = val[:, s * LANE:(s + 1) * LANE]


def _token_tile_slabs(read, tm):
    return [read(pl.ds(s, tm, stride=TOK_ROWS)) for s in range(TOK_ROWS)]


def _expert_kernel(te_ref, nu_ref, src_ref, h_hbm, wg_ref, wu_ref, wd_ref, o_ref, hb_ref, hbuf_ref, sem):
    j = pl.program_id(0)
    nu = nu_ref[0]
    tm = hb_ref.shape[0]

    def row_copy(row0, slot, i):
        src = h_hbm.at[pl.ds(pl.multiple_of(row0, TOK_ROWS), TOK_ROWS), :]
        dst = hbuf_ref.at[slot, pl.ds(pl.multiple_of(i * TOK_ROWS, TOK_ROWS), TOK_ROWS), :]
        return pltpu.make_async_copy(src, dst, sem.at[slot])

    def issue(tile, slot):
        def body(g, c):
            for k in range(DMA_QUEUES):
                i = g * DMA_QUEUES + k
                row_copy(src_ref[tile * tm + i], slot, i).start(priority=k)
            return c

        lax.fori_loop(0, tm // DMA_QUEUES, body, 0, unroll=ROW_UNROLL // DMA_QUEUES)

    def wait(slot):
        def body(i, c):
            row_copy(0, slot, 0).wait()
            return c

        lax.fori_loop(0, tm, body, 0, unroll=ROW_UNROLL)

    slot = j % 2

    @pl.when(j == 0)
    def _():
        issue(0, 0)

    @pl.when(j <= nu)
    def _():
        wait(slot)

    @pl.when(j < nu)
    def _():
        for s, slab in enumerate(_token_tile_slabs(lambda rows: hbuf_ref[slot, rows, :], tm)):
            hb_ref[:, s * LANE:(s + 1) * LANE] = slab.astype(BF16)
        issue(jnp.minimum(j + 1, pl.num_programs(0) - 1), 1 - slot)
        _to_token_tiles(o_ref, _swiglu(hb_ref[...], wg_ref, wu_ref, wd_ref))

    @pl.when((j == pl.num_programs(0) - 1) & (j < nu))
    def _():
        wait(1 - slot)

    @pl.when(j >= nu)
    def _():
        o_ref[...] = jnp.zeros_like(o_ref)


def _experts(tile_expert, n_used, src, h2t, wg, wu, wd, n_tiles, tm):
    R, D = h2t.shape[0] // TOK_ROWS, D_MODEL
    assert R & (R - 1) == 0
    expert = lambda j, te, nu: te[jnp.minimum(j, nu[0] - 1)]
    return pl.pallas_call(
        _expert_kernel,
        out_shape=jax.ShapeDtypeStruct((n_tiles * tm * TOK_ROWS, LANE), F32),
        grid_spec=pltpu.PrefetchScalarGridSpec(
            num_scalar_prefetch=3,
            grid=(n_tiles,),
            in_specs=[
                pl.BlockSpec(memory_space=pl.ANY),
                pl.BlockSpec((None, D, D_FF), lambda j, te, nu, src: (expert(j, te, nu), 0, 0)),
                pl.BlockSpec((None, D, D_FF), lambda j, te, nu, src: (expert(j, te, nu), 0, 0)),
                pl.BlockSpec((None, D_FF, D), lambda j, te, nu, src: (expert(j, te, nu), 0, 0)),
            ],
            out_specs=pl.BlockSpec((tm * TOK_ROWS, LANE), lambda j, te, nu, src: (j, 0)),
            scratch_shapes=[pltpu.VMEM((tm, D), BF16), pltpu.VMEM((2, tm * TOK_ROWS, LANE), F32),
                            pltpu.SemaphoreType.DMA((2,))],
        ),
        compiler_params=_cparams(("arbitrary",), 56),
        name="moe_experts",
    )(tile_expert, n_used, src, h2t, wg, wu, wd)


def _combine_kernel(d1_ref, d2_ref, y_hbm, meta_ref, x_ref, npost_ref, g2_ref, o_ref, ya_ref, yb_ref, sem):
    tm = x_ref.shape[0]
    step = pl.program_id(0)
    slot = step % 2

    def row_copy(row, dst_ref, s, r):
        src = y_hbm.at[pl.ds(pl.multiple_of(row * TOK_ROWS, TOK_ROWS), TOK_ROWS), :]
        dst = dst_ref.at[s, pl.ds(pl.multiple_of(r * TOK_ROWS, TOK_ROWS), TOK_ROWS), :]
        return pltpu.make_async_copy(src, dst, sem.at[s])

    def issue(tile, s):
        def body(r, c):
            t = tile * tm + r
            row_copy(d1_ref[t], ya_ref, s, r).start(priority=0)
            row_copy(d2_ref[t], yb_ref, s, r).start(priority=1)
            return c

        lax.fori_loop(0, tm, body, 0, unroll=ROW_UNROLL)

    @pl.when(step == 0)
    def _():
        issue(0, 0)

    @pl.when(step + 1 < pl.num_programs(0))
    def _():
        issue(step + 1, 1 - slot)

    def wait(r, c):
        row_copy(0, ya_ref, slot, 0).wait()
        row_copy(0, yb_ref, slot, 0).wait()
        return c

    lax.fori_loop(0, tm, wait, 0, unroll=ROW_UNROLL)
    meta = meta_ref[...]
    w1, w2 = meta[:, 4:5], meta[:, 5:6]
    slabs_a = _token_tile_slabs(lambda rows: ya_ref[slot, rows, :], tm)
    slabs_b = _token_tile_slabs(lambda rows: yb_ref[slot, rows, :], tm)
    y = jnp.concatenate([w1 * a + w2 * b for a, b in zip(slabs_a, slabs_b)], axis=1)
    o_ref[...] = x_ref[...] + g2_ref[...] * _rms(y, npost_ref[...])


def _combine(dest1, dest2, sorted_y, meta, x2, npost, mod, layer, S, tm=512):
    R, D = x2.shape
    tpb = S // tm
    return pl.pallas_call(
        _combine_kernel,
        out_shape=jax.ShapeDtypeStruct((R, D), F32),
        grid_spec=pltpu.PrefetchScalarGridSpec(
            num_scalar_prefetch=2,
            grid=(R // tm,),
            in_specs=[
                pl.BlockSpec(memory_space=pl.ANY),
                pl.BlockSpec((tm, LANE), lambda i, *_: (i, 0)),
                pl.BlockSpec((tm, D), lambda i, *_: (i, 0)),
                pl.BlockSpec((1, D), lambda i, *_: (0, 0)),
                _mod_spec(layer, 5, tpb),
            ],
            out_specs=pl.BlockSpec((tm, D), lambda i, *_: (i, 0)),
            scratch_shapes=[pltpu.VMEM((2, tm * TOK_ROWS, LANE), F32), pltpu.VMEM((2, tm * TOK_ROWS, LANE), F32),
                            pltpu.SemaphoreType.DMA((2,))],
        ),
        compiler_params=_cparams(("arbitrary",), 40),
        name="moe_combine",
    )(dest1, dest2, sorted_y, meta, x2, npost, mod)


def _moe(h2, meta, meta_t, counts, x2, wg, wu, wd, npost, mod, layer, S):
    R, D = x2.shape
    tm = MOE_TM
    n_tiles = (TOP_K * R + N_EXPERTS * (tm - 1)) // tm + 1
    i1, i2, r1, r2 = (meta_t[j].astype(I32) for j in range(4))
    cnt = counts[0, :N_EXPERTS].astype(I32)
    padded = (cnt + tm - 1) // tm * tm
    end = jnp.cumsum(padded)
    start = end - padded
    eye = jnp.arange(N_EXPERTS, dtype=I32)[:, None]
    dest1 = jnp.sum(jnp.where(i1[None, :] == eye, start[:, None], 0), axis=0) + r1
    dest2 = jnp.sum(jnp.where(i2[None, :] == eye, start[:, None], 0), axis=0) + r2
    tile_row0 = jnp.arange(n_tiles, dtype=I32) * tm
    tile_expert = jnp.minimum(jnp.sum(tile_row0[:, None] >= end[None, :], axis=1), N_EXPERTS - 1).astype(I32)
    n_used = (end[-1:] // tm).astype(I32)
    src = _invperm(jnp.concatenate([dest1, dest2]), n_tiles * tm)
    sorted_y = _experts(tile_expert, n_used, src, h2, wg, wu, wd, n_tiles, tm)
    return _combine(dest1, dest2, sorted_y, meta, x2, npost, mod, layer, S)


def kernel(x, c, w_ada, b_ada, norm_pre_mix, w_in, w_att_proj, w_ret_proj, w_mix_out, norm_post_mix, norm_pre_ffn, w_ffn_gate, w_ffn_up, w_ffn_down, w_router, b_router, w_exp_gate, w_exp_up, w_exp_down, norm_post_ffn):
    B, S, D = x.shape
    R = B * S
    mod = _ada(c, w_ada, b_ada)
    x2 = x.reshape(R, D)
    row = lambda v: v.reshape(1, D)
    for layer in range(DEPTH):
        main, *atts = _inproj(x2, row(norm_pre_mix[layer]), mod, layer, _prep_w_in(w_in[layer]), B, S)
        outs = [_attn_group(atts[g], g) for g in range(len(ATT_GROUPS))]
        ret = _retention(main.reshape(B, S, MAIN_COLS), B, S)
        j = layer // 2
        router = None
        if layer % 2 == 1:
            wrt = jnp.zeros((D, LANE), F32).at[:, :N_EXPERTS].set(w_router[j])
            brt = jnp.zeros((1, LANE), F32).at[0, :N_EXPERTS].set(b_router[j])
            router = (wrt, brt)
        wa = jnp.pad(w_att_proj[layer].reshape(HEADS_PER_GROUP, ATT_HEAD_DIM, D),
                     ((0, 0), (0, LANE - ATT_HEAD_DIM), (0, 0))).reshape(ATT_T_WIDTH, D)
        res = _postmix(outs, ret, main, x2, wa.astype(BF16), w_ret_proj[layer].astype(BF16),
                       w_mix_out[layer].astype(BF16), row(norm_post_mix[layer]), row(norm_pre_ffn[layer]),
                       mod, layer, S, router=router)
        if layer % 2 == 0:
            x2, h2 = res
            x2 = _ffn(h2, x2, w_ffn_gate[j].astype(BF16), w_ffn_up[j].astype(BF16), w_ffn_down[j].astype(BF16),
                      row(norm_post_ffn[layer]), mod, layer, S)
        else:
            x2, h2, meta, meta_t, counts = res
            x2 = _moe(h2, meta, meta_t, counts, x2, w_exp_gate[j].astype(BF16), w_exp_up[j].astype(BF16),
                      w_exp_down[j].astype(BF16), row(norm_post_ffn[layer]), mod, layer, S)
    return x2.reshape(B, S, D)
```
